```python
import math
import jax
import jax.numpy as jnp
from jax import lax
import numpy as np

D_MODEL = 1024
BATCH = 2
SEQ = 8192
DEPTH = 2

GRID_W = 64
CTX_LEN = 256
DN_HEADS = 8
DN_DK = 128
DN_DV = 128
DN_CONV = 5
DN_CHUNK = 64
NA_HEADS = 8
NA_DH = 64
NA_WIN_R = 8
NA_WIN_W = 16
ROPE_BASE = 10000.0
D_FF_DENSE = 2816
N_EXPERTS = 8
TOP_K = 2
D_FF_EXPERT = 3584
EPS = 1e-6

DN_QK = DN_HEADS * DN_DK
DN_V = DN_HEADS * DN_DV
NA_W = NA_HEADS * NA_DH
IN_SIZES = (DN_QK, DN_QK, DN_V, DN_V, 4 * DN_HEADS, NA_W, NA_W, NA_W, D_MODEL, D_MODEL)
D_IN = sum(IN_SIZES)
N_DENSE = (DEPTH + 1) // 2
N_MOE = DEPTH // 2

kernel_name = 'hybrid_deltanet_natten_moe_block'

F32 = jnp.float32


def split_cols(p, sizes):
    out, start = [], 0
    for s in sizes:
        out.append(p[..., start:start + s])
        start += s
    return out


def rmsnorm(x, g):
    xf = x.astype(F32)
    y = xf * lax.rsqrt(jnp.mean(xf * xf, axis=-1, keepdims=True) + EPS)
    return (y * g.astype(F32)).astype(x.dtype)


def l2norm(x):
    return x * lax.rsqrt(jnp.sum(x * x, axis=-1, keepdims=True) + EPS)


def short_conv(x, w):
    k = w.shape[-1]
    pad = k // 2
    t = x.shape[1]
    xp = jnp.pad(x, ((0, 0), (pad, pad), (0, 0)))
    out = xp[:, 0:t] * w[:, 0]
    for i in range(1, k):
        out = out + xp[:, i:i + t] * w[:, i]
    return out


def axial_rope(t_len, dh):
    t = jnp.arange(t_len)
    row = (t // GRID_W).astype(F32)
    col = (t % GRID_W).astype(F32)
    n_freq = dh // 4
    inv = ROPE_BASE ** (-jnp.arange(n_freq, dtype=F32) / n_freq)
    ang = jnp.concatenate([row[:, None] * inv, col[:, None] * inv], axis=-1)
    return jnp.cos(ang), jnp.sin(ang)


def apply_rope(x, cos, sin):
    x1, x2 = jnp.split(x, 2, axis=-1)
    c = cos[:, None, :]
    s = sin[:, None, :]
    return jnp.concatenate([x1 * c - x2 * s, x1 * s + x2 * c], axis=-1)


def gated_delta_chunked(q, k, v, g, beta, s0):
    b_, h_, t_, _ = q.shape
    dv = v.shape[-1]
    c_ = DN_CHUNK
    n_ = t_ // c_
    rs = lambda a: a.reshape((b_, h_, n_, c_) + a.shape[3:])
    q, k, v, g, beta = rs(q), rs(k), rs(v), rs(g), rs(beta)
    gc = jnp.cumsum(g, axis=-1)
    tril = jnp.tril(jnp.ones((c_, c_), dtype=bool))
    strict = jnp.tril(jnp.ones((c_, c_), dtype=bool), -1)
    diff = gc[..., :, None] - gc[..., None, :]
    decay_mat = jnp.where(tril, jnp.exp(jnp.where(tril, diff, 0.0)), 0.0)
    kb = k * beta[..., None]
    vb = v * beta[..., None]
    t_mat = jnp.where(strict, jnp.einsum('bhncd,bhnjd->bhncj', kb, k) * decay_mat, 0.0)
    a_mat = t_mat + jnp.eye(c_, dtype=q.dtype)
    u = lax.linalg.triangular_solve(a_mat, vb, left_side=True, lower=True, unit_diagonal=True)
    w = lax.linalg.triangular_solve(a_mat, kb * jnp.exp(gc)[..., None], left_side=True, lower=True,
                                    unit_diagonal=True)
    attn = jnp.einsum('bhncd,bhnjd->bhncj', q, k) * decay_mat
    g_last = gc[..., -1]
    k_dec = k * jnp.exp(g_last[..., None] - gc)[..., None]
    q_dec = q * jnp.exp(gc)[..., None]

    def step(state, xs):
        u_c, w_c, attn_c, q_c, k_c, gl = xs
        v_new = u_c - jnp.einsum('bhcd,bhde->bhce', w_c, state)
        o_c = jnp.einsum('bhcd,bhde->bhce', q_c, state) + jnp.einsum('bhcj,bhje->bhce', attn_c, v_new)
        state = state * jnp.exp(gl)[..., None, None] + jnp.einsum('bhcd,bhce->bhde', k_c, v_new)
        return state, o_c

    mv = lambda a: jnp.moveaxis(a, 2, 0)
    s_fin, o = lax.scan(step, s0, (mv(u), mv(w), mv(attn), mv(q_dec), mv(k_dec), mv(g_last)))
    o = jnp.moveaxis(o, 0, 2).reshape(b_, h_, t_, dv)
    return o, s_fin


def dn_stream(q, k, v, ab, conv_w, a_log, dt_bias, rope):
    b_, t_, _ = q.shape
    qkv = jax.nn.silu(short_conv(jnp.concatenate([q, k, v], axis=-1), conv_w)).astype(F32)
    q, k, v = split_cols(qkv, (DN_QK, DN_QK, DN_V))
    q = l2norm(q.reshape(b_, t_, DN_HEADS, DN_DK))
    k = l2norm(k.reshape(b_, t_, DN_HEADS, DN_DK))
    if rope is not None:
        q = apply_rope(q, rope[0], rope[1])
        k = apply_rope(k, rope[0], rope[1])
    q = q * (DN_DK ** -0.5)
    v = v.reshape(b_, t_, DN_HEADS, DN_DV)
    ab = ab.astype(F32)
    a = ab[..., :2 * DN_HEADS].reshape(b_, t_, 2, DN_HEADS)
    bl = ab[..., 2 * DN_HEADS:].reshape(b_, t_, 2, DN_HEADS)
    g = -jnp.exp(a_log.astype(F32)) * jax.nn.softplus(a + dt_bias.astype(F32))
    beta = jax.nn.sigmoid(bl)
    to_bht = lambda t: jnp.moveaxis(t, 1, 2)
    to_dbht = lambda t: jnp.transpose(t, (2, 0, 3, 1))
    return to_bht(q), to_bht(k), to_bht(v), to_dbht(g), to_dbht(beta)


def dn_output(o, z, norm_w):
    b_, h_, t_, dv = o.shape
    o = jnp.moveaxis(o, 1, 2)
    o = o * lax.rsqrt(jnp.mean(o * o, axis=-1, keepdims=True) + EPS) * norm_w.astype(F32)
    o = o.reshape(b_, t_, h_ * dv) * jax.nn.silu(z.astype(F32))
    return o.astype(z.dtype)


def neighbourhood_attention(q, k, v, kc, vc, rpb):
    b_, s_, h_, dh = q.shape
    rows = s_ // GRID_W
    kr = min(NA_WIN_R, rows)
    kw = NA_WIN_W
    scale = dh ** -0.5
    r = jnp.arange(rows)
    col = jnp.arange(GRID_W)
    r0 = jnp.clip(r - kr // 2, 0, rows - kr)
    key_rows = r0[:, None] + jnp.arange(kr)[None, :]
    c0 = jnp.clip(col - kw // 2, 0, GRID_W - kw)
    col_in = (col[None, :] >= c0[:, None]) & (col[None, :] < c0[:, None] + kw)
    qg = q.reshape(b_, rows, GRID_W, h_, dh)
    kg = k.reshape(b_, rows, GRID_W, h_, dh)[:, key_rows]
    vg = v.reshape(b_, rows, GRID_W, h_, dh)[:, key_rows].reshape(b_, rows, kr * GRID_W, h_, dh)
    roff = key_rows - r[:, None] + (NA_WIN_R - 1)
    coff = jnp.clip(col[None, :] - col[:, None] + (NA_WIN_W - 1), 0, 2 * NA_WIN_W - 2)
    bias = rpb[:, roff[:, None, :, None], coff[None, :, None, :]].astype(F32)
    s_loc = jnp.einsum('brqhd,brikhd->bhrqik', qg, kg).astype(F32) * scale + bias[None]
    s_loc = jnp.where(col_in[:, None, :], s_loc, -jnp.inf).reshape(b_, h_, rows, GRID_W, kr * GRID_W)
    s_ctx = jnp.einsum('brqhd,bchd->bhrqc', qg, kc).astype(F32) * scale
    p = jax.nn.softmax(jnp.concatenate([s_loc, s_ctx], axis=-1), axis=-1).astype(v.dtype)
    p_loc = p[..., :kr * GRID_W]
    p_ctx = p[..., kr * GRID_W:]
    o = (jnp.einsum('bhrqk,brkhd->brqhd', p_loc, vg)
         + jnp.einsum('bhrqc,bchd->brqhd', p_ctx, vc))
    return o.reshape(b_, s_, h_ * dh)


def context_attention(q, k, v):
    b_, c_, h_, dh = q.shape
    s = jnp.einsum('bqhd,bkhd->bhqk', q, k).astype(F32) * (dh ** -0.5)
    p = jax.nn.softmax(s, axis=-1).astype(v.dtype)
    return jnp.einsum('bhqk,bkhd->bqhd', p, v).reshape(b_, c_, h_ * dh)


def merge_branches(dn_o, na_o, gate_dn, gate_na, w_pa, w_pb, w_out):
    y = jax.nn.sigmoid(gate_dn) * (dn_o @ w_pa) + jax.nn.sigmoid(gate_na) * (na_o @ w_pb)
    return y @ w_out


def hybrid_mixer(h, hc, w_in, conv_w, a_log, dt_bias, dn_norm_w, rpb, w_pa, w_pb, w_out, cos, sin,
                 need_ctx):
    b_ = h.shape[0]
    p = split_cols(h @ w_in, IN_SIZES)
    pc = split_cols(hc @ w_in, IN_SIZES)
    ql, kl, vl, gl, bl = dn_stream(p[0], p[1], p[2], p[4], conv_w, a_log, dt_bias, (cos, sin))
    qc, kc, vc, gc, bc = dn_stream(pc[0], pc[1], pc[2], pc[4], conv_w, a_log, dt_bias, None)
    s0 = jnp.zeros((b_, DN_HEADS, DN_DK, DN_DV), F32)
    flip = lambda t: jnp.flip(t, axis=2)
    oc_f, s_f = gated_delta_chunked(qc, kc, vc, gc[0], bc[0], s0)
    oc_b, s_b = gated_delta_chunked(flip(qc), flip(kc), flip(vc), flip(gc[1]), flip(bc[1]), s0)
    ol_f, _ = gated_delta_chunked(ql, kl, vl, gl[0], bl[0], s_f)
    ol_b, _ = gated_delta_chunked(flip(ql), flip(kl), flip(vl), flip(gl[1]), flip(bl[1]), s_b)
    dn_lat = dn_output(ol_f + flip(ol_b), p[3], dn_norm_w)
    heads = lambda t: t.reshape(t.shape[0], t.shape[1], NA_HEADS, NA_DH)
    kcn, vcn = heads(pc[6]), heads(pc[7])
    na_lat = neighbourhood_attention(heads(p[5]), heads(p[6]), heads(p[7]), kcn, vcn, rpb)
    y = merge_branches(dn_lat, na_lat, p[8], p[9], w_pa, w_pb, w_out)
    if not need_ctx:
        return y, None
    dn_ctx = dn_output(oc_f + flip(oc_b), pc[3], dn_norm_w)
    na_ctx = context_attention(heads(pc[5]), kcn, vcn)
    yc = merge_branches(dn_ctx, na_ctx, pc[8], pc[9], w_pa, w_pb, w_out)
    return y, yc


def swiglu(h, w1, w3, w2):
    return (jax.nn.silu(h @ w1) * (h @ w3)) @ w2


def moe_swiglu(h, router, w1, w3, w2):
    logits = (h @ router).astype(F32)
    top_v, top_i = lax.top_k(logits, TOP_K)
    top_w = jax.nn.softmax(top_v, axis=-1)
    gate = jnp.sum(jax.nn.one_hot(top_i, N_EXPERTS, dtype=F32) * top_w[..., None], axis=-2)
    out = jnp.zeros_like(h)
    for e in range(N_EXPERTS):
        out = out + gate[..., e:e + 1].astype(h.dtype) * swiglu(h, w1[e], w3[e], w2[e])
    return out


def setup_inputs(seed: int = 0) -> dict:
    key = jax.random.key(seed)
    ks = iter(jax.random.split(key, 32))
    nrm = lambda shape, scale: jax.random.normal(next(ks), shape, F32) * scale
    gain = lambda shape: 1.0 + nrm(shape, 0.02)
    d = D_MODEL
    inp = {}
    inp['x'] = nrm((BATCH, SEQ, d), 1.0)
    inp['c'] = nrm((BATCH, d), 1.0)
    inp['ctx'] = nrm((BATCH, CTX_LEN, d), 1.0)
    inp['c_ctx'] = nrm((d,), 1.0)
    inp['ada_w'] = nrm((DEPTH, d, 6 * d), 0.5 * d ** -0.5)
    inp['ada_b'] = nrm((DEPTH, 6 * d), 0.02)
    inp['norm_mix_pre'] = gain((DEPTH, d))
    inp['norm_mix_post'] = gain((DEPTH, d))
    inp['norm_ffn_pre'] = gain((DEPTH, d))
    inp['norm_ffn_post'] = gain((DEPTH, d))
    inp['w_in'] = nrm((DEPTH, d, D_IN), d ** -0.5)
    inp['dn_conv'] = nrm((DEPTH, 2 * DN_QK + DN_V, DN_CONV), DN_CONV ** -0.5)
    inp['dn_a_log'] = jnp.log(jax.random.uniform(next(ks), (DEPTH, 2, DN_HEADS), F32, 1.0, 16.0))
    dt = jnp.exp(jax.random.uniform(next(ks), (DEPTH, 2, DN_HEADS), F32, math.log(1e-3), math.log(1e-1)))
    inp['dn_dt_bias'] = dt + jnp.log(-jnp.expm1(-dt))
    inp['dn_norm'] = gain((DEPTH, DN_DV))
    inp['na_rpb'] = nrm((DEPTH, NA_HEADS, 2 * NA_WIN_R - 1, 2 * NA_WIN_W - 1), 0.05)
    inp['w_branch_dn'] = nrm((DEPTH, DN_V, d), DN_V ** -0.5)
    inp['w_branch_na'] = nrm((DEPTH, NA_W, d), NA_W ** -0.5)
    inp['w_out'] = nrm((DEPTH, d, d), d ** -0.5)
    inp['ffn_w1'] = nrm((N_DENSE, d, D_FF_DENSE), d ** -0.5)
    inp['ffn_w3'] = nrm((N_DENSE, d, D_FF_DENSE), d ** -0.5)
    inp['ffn_w2'] = nrm((N_DENSE, D_FF_DENSE, d), D_FF_DENSE ** -0.5)
    inp['moe_router'] = nrm((N_MOE, d, N_EXPERTS), d ** -0.5)
    inp['moe_w1'] = nrm((N_MOE, N_EXPERTS, d, D_FF_EXPERT), d ** -0.5)
    inp['moe_w3'] = nrm((N_MOE, N_EXPERTS, d, D_FF_EXPERT), d ** -0.5)
    inp['moe_w2'] = nrm((N_MOE, N_EXPERTS, D_FF_EXPERT, d), D_FF_EXPERT ** -0.5)
    return inp


def reference(x, c, ctx, c_ctx, ada_w, ada_b, norm_mix_pre, norm_mix_post, norm_ffn_pre, norm_ffn_post,
              w_in, dn_conv, dn_a_log, dn_dt_bias, dn_norm, na_rpb, w_branch_dn, w_branch_na, w_out,
              ffn_w1, ffn_w3, ffn_w2, moe_router, moe_w1, moe_w3, moe_w2):
    seq_len = x.shape[1]
    cos, sin = axial_rope(seq_len, DN_DK)
    xc = ctx
    for l in range(DEPTH):
        last = l == DEPTH - 1
        mod = jax.nn.silu(c) @ ada_w[l] + ada_b[l]
        sh1, sc1, g1, sh2, sc2, g2 = jnp.split(mod[:, None, :], 6, axis=-1)
        cmod = jax.nn.silu(c_ctx) @ ada_w[l] + ada_b[l]
        csh1, csc1, cg1, csh2, csc2, cg2 = jnp.split(cmod, 6)

        h = rmsnorm(x, norm_mix_pre[l]) * (1 + sc1) + sh1
        hc = rmsnorm(xc, norm_mix_pre[l]) * (1 + csc1) + csh1
        y, yc = hybrid_mixer(h, hc, w_in[l], dn_conv[l], dn_a_log[l], dn_dt_bias[l], dn_norm[l], na_rpb[l],
                             w_branch_dn[l], w_branch_na[l], w_out[l], cos, sin, not last)
        x = x + g1 * rmsnorm(y, norm_mix_post[l])

        def channel_mixer(t):
            if l % 2 == 0:
                return swiglu(t, ffn_w1[l // 2], ffn_w3[l // 2], ffn_w2[l // 2])
            return moe_swiglu(t, moe_router[l // 2], moe_w1[l // 2], moe_w3[l // 2], moe_w2[l // 2])

        h = rmsnorm(x, norm_ffn_pre[l]) * (1 + sc2) + sh2
        x = x + g2 * rmsnorm(channel_mixer(h), norm_ffn_post[l])
        if not last:
            xc = xc + cg1 * rmsnorm(yc, norm_mix_post[l])
            hc = rmsnorm(xc, norm_ffn_pre[l]) * (1 + csc2) + csh2
            xc = xc + cg2 * rmsnorm(channel_mixer(hc), norm_ffn_post[l])
    return x
```

```python
import functools
import math

import jax
import jax.numpy as jnp
from jax import lax
from jax.experimental import pallas as pl
from jax.experimental.pallas import tpu as pltpu

F32 = jnp.float32
BF16 = jnp.bfloat16

GRID_W = 64
DN_HEADS = 8
DN_DK = 128
DN_CONV = 5
DN_CHUNK = 64
NA_HEADS = 8
NA_DH = 64
NA_WIN_R = 8
NA_WIN_W = 16
ROPE_BASE = 10000.0
N_EXPERTS = 8
EPS = 1e-6

LANES = 128
SUBLANES = 8
VMEM_LIMIT = 56 * 1024 * 1024

SCAN_TILE = 256
NA_QROWS = 4
NA_KROWS = NA_QROWS + 8
MOE_TM = 512


def _cparams(sem):
    return pltpu.CompilerParams(dimension_semantics=sem, vmem_limit_bytes=VMEM_LIMIT)


def _bdot(a, b):
    return jnp.dot(a.astype(BF16), b.astype(BF16), preferred_element_type=F32)


def _dot_nt(a, b):
    return lax.dot_general(a.astype(BF16), b.astype(BF16), (((1,), (1,)), ((), ())),
                           preferred_element_type=F32)


def _dot_tn(a, b):
    return lax.dot_general(a.astype(BF16), b.astype(BF16), (((0,), (0,)), ((), ())),
                           preferred_element_type=F32)


def _split3(x):
    hi = x.astype(BF16)
    r = x - hi.astype(F32)
    mid = r.astype(BF16)
    lo = (r - mid.astype(F32)).astype(BF16)
    return hi, mid, lo


def _rms(x, gain):
    return x * lax.rsqrt(jnp.mean(x * x, axis=-1, keepdims=True) + EPS) * gain


def _mod_kernel(c_ref, w_ref, b_ref, o_ref):
    c = c_ref[...]
    s = c * jax.nn.sigmoid(c)
    o_ref[0] = jnp.dot(s, w_ref[0], precision=lax.Precision.HIGHEST,
                       preferred_element_type=F32) + b_ref[0]


def _mod_vectors(c_rows, ada_w, ada_b):
    depth, d, n = ada_w.shape
    tn = 1536
    return pl.pallas_call(
        _mod_kernel,
        grid=(depth, n // tn),
        in_specs=[pl.BlockSpec((SUBLANES, d), lambda l, j: (0, 0)),
                  pl.BlockSpec((1, d, tn), lambda l, j: (l, 0, j)),
                  pl.BlockSpec((1, 1, tn), lambda l, j: (l, 0, j))],
        out_specs=pl.BlockSpec((1, SUBLANES, tn), lambda l, j: (l, 0, j)),
        out_shape=jax.ShapeDtypeStruct((depth, SUBLANES, n), F32),
        compiler_params=_cparams(("arbitrary", "arbitrary")),
        name="mod_vectors",
    )(c_rows, ada_w, ada_b.reshape(depth, 1, n))


def _inproj_kernel(x_ref, g_ref, sc_ref, sh_ref, w_ref, wab_hi_ref, wab_lo_ref,
                   o_ref, ab_ref, h_scr, hlo_scr):
    j = pl.program_id(1)

    @pl.when(j == 0)
    def _():
        h = _rms(x_ref[...], g_ref[...]) * (1.0 + sc_ref[...]) + sh_ref[...]
        hi = h.astype(BF16)
        lo = (h - hi.astype(F32)).astype(BF16)
        h_scr[...] = hi
        hlo_scr[...] = lo
        ab_ref[...] = (jnp.dot(hi, wab_hi_ref[...], preferred_element_type=F32)
                       + jnp.dot(lo, wab_hi_ref[...], preferred_element_type=F32)
                       + jnp.dot(hi, wab_lo_ref[...], preferred_element_type=F32))

    o_ref[...] = jnp.dot(h_scr[...], w_ref[...], preferred_element_type=F32)


def _in_proj(x2, gain, mod, mod_row, sc_blk, sh_blk, w_main, wab_hi, wab_lo, tm, tn):
    rows, d = x2.shape
    n = w_main.shape[1]
    return pl.pallas_call(
        _inproj_kernel,
        grid=(rows // tm, n // tn),
        in_specs=[pl.BlockSpec((tm, d), lambda i, j: (i, 0)),
                  pl.BlockSpec((1, d), lambda i, j: (0, 0)),
                  pl.BlockSpec((None, 1, d), lambda i, j: (mod_row(i), 0, sc_blk)),
                  pl.BlockSpec((None, 1, d), lambda i, j: (mod_row(i), 0, sh_blk)),
                  pl.BlockSpec((d, tn), lambda i, j: (0, j)),
                  pl.BlockSpec((d, LANES), lambda i, j: (0, 0)),
                  pl.BlockSpec((d, LANES), lambda i, j: (0, 0))],
        out_specs=[pl.BlockSpec((tm, tn), lambda i, j: (i, j)),
                   pl.BlockSpec((tm, LANES), lambda i, j: (i, 0))],
        out_shape=[jax.ShapeDtypeStruct((rows, n), F32),
                   jax.ShapeDtypeStruct((rows, LANES), F32)],
        scratch_shapes=[pltpu.VMEM((tm, d), BF16), pltpu.VMEM((tm, d), BF16)],
        compiler_params=_cparams(("arbitrary", "arbitrary")),
        name="in_proj",
    )(x2, gain, mod, mod, w_main, wab_hi, wab_lo)


def _prep_kernel(rope, n_tiles,
                 q_ref, qp_ref, qn_ref, k_ref, kp_ref, kn_ref, v_ref, vp_ref, vn_ref,
                 cw_ref, ab_ref, gpar_ref, cos_ref, sin_ref,
                 qo_ref, ko_ref, vo_ref, gb_ref):
    t = pl.program_id(1)
    h = pl.program_id(2)
    tt = q_ref.shape[0]
    first = t == 0
    last = t == n_tiles - 1

    def conv_silu(m_ref, p_ref, n_ref, w):
        prev = jnp.where(first, 0.0, p_ref[...])
        nxt = jnp.where(last, 0.0, n_ref[...])
        xp = jnp.concatenate([prev, m_ref[...], nxt], axis=0)
        pad = DN_CONV // 2
        acc = xp[SUBLANES - pad:SUBLANES - pad + tt] * w[0:1]
        for i in range(1, DN_CONV):
            o = SUBLANES - pad + i
            acc = acc + xp[o:o + tt] * w[i:i + 1]
        return acc * jax.nn.sigmoid(acc)

    def l2n(x):
        return x * lax.rsqrt(jnp.sum(x * x, axis=-1, keepdims=True) + EPS)

    def rot(x):
        if not rope:
            return x
        return x * cos_ref[...] + pltpu.roll(x, LANES // 2, 1) * sin_ref[...]

    cw = cw_ref[...]
    q = rot(l2n(conv_silu(q_ref, qp_ref, qn_ref, cw[0]))) * (DN_DK ** -0.5)
    k = rot(l2n(conv_silu(k_ref, kp_ref, kn_ref, cw[1])))
    v = conv_silu(v_ref, vp_ref, vn_ref, cw[2])
    qo_ref[...] = q
    ko_ref[...] = k
    vo_ref[...] = v

    @pl.when(h == 0)
    def _():
        ab = ab_ref[...]
        lane = lax.broadcasted_iota(jnp.int32, ab.shape, 1)
        row = lax.broadcasted_iota(jnp.int32, ab.shape, 0) % DN_CHUNK
        xg = ab + gpar_ref[1:2]
        sp = jnp.maximum(xg, 0.0) + jnp.log1p(jnp.exp(-jnp.abs(xg)))
        g = gpar_ref[0:1] * sp
        beta = jax.nn.sigmoid(ab)
        pre = g
        suf = g
        s = 1
        while s < DN_CHUNK:
            pre = pre + jnp.where(row >= s, pltpu.roll(pre, s, 0), 0.0)
            suf = suf + jnp.where(row < DN_CHUNK - s, pltpu.roll(suf, tt - s, 0), 0.0)
            s *= 2
        nh = DN_HEADS
        gb_ref[...] = jnp.where(lane < nh, pre, jnp.where(lane < 2 * nh, suf, beta))


def _dn_prep(p, ab, conv_w3, gpar, cos2, sin2, batch, seq, rope):
    rows = p.shape[0]
    tt = SCAN_TILE
    n_tiles = seq // tt
    hb = tt // SUBLANES
    nblk8 = rows // SUBLANES
    nh = DN_HEADS

    def main(col0):
        return pl.BlockSpec((tt, LANES), lambda b, t, h: (b * n_tiles + t, col0 + h))

    def prev(col0):
        return pl.BlockSpec((SUBLANES, LANES),
                            lambda b, t, h: (jnp.maximum((b * n_tiles + t) * hb - 1, 0), col0 + h))

    def nxt(col0):
        return pl.BlockSpec((SUBLANES, LANES),
                            lambda b, t, h: (jnp.minimum((b * n_tiles + t + 1) * hb, nblk8 - 1), col0 + h))

    in_specs = []
    for c0 in (0, nh, 2 * nh):
        in_specs += [main(c0), prev(c0), nxt(c0)]
    in_specs += [
        pl.BlockSpec((3, SUBLANES, LANES), lambda b, t, h: (0, 0, h)),
        pl.BlockSpec((tt, LANES), lambda b, t, h: (b * n_tiles + t, 0)),
        pl.BlockSpec((SUBLANES, LANES), lambda b, t, h: (0, 0)),
        pl.BlockSpec((tt, LANES), lambda b, t, h: (t, 0)),
        pl.BlockSpec((tt, LANES), lambda b, t, h: (t, 0)),
    ]
    out_full = pl.BlockSpec((tt, LANES), lambda b, t, h: (b * n_tiles + t, h))
    d = nh * LANES
    return pl.pallas_call(
        functools.partial(_prep_kernel, rope, n_tiles),
        grid=(batch, n_tiles, nh),
        in_specs=in_specs,
        out_specs=[out_full, out_full, out_full,
                   pl.BlockSpec((tt, LANES), lambda b, t, h: (b * n_tiles + t, 0))],
        out_shape=[jax.ShapeDtypeStruct((rows, d), F32)] * 3
        + [jax.ShapeDtypeStruct((rows, LANES), F32)],
        compiler_params=_cparams(("arbitrary", "arbitrary", "arbitrary")),
        name="dn_prep_rope" if rope else "dn_prep",
    )(p, p, p, p, p, p, p, p, p, conv_w3, ab, gpar, cos2, sin2)


def _tri_inverse(tm, ri, ci):
    eye = (ri == ci).astype(F32)
    p = -jnp.where((ri // 8) == (ci // 8), tm, 0.0)
    p2 = _bdot(p, p)
    p4 = _bdot(p2, p2)
    x = eye + p
    x = x + _bdot(p2, x)
    x = x + _bdot(p4, x)
    s = 8
    while s < DN_CHUNK:
        off = jnp.where(((ri // (2 * s)) == (ci // (2 * s))) & ((ri // s) != (ci // s)), tm, 0.0)
        x = x - _bdot(x, _bdot(off, x))
        s *= 2
    return x


def _scan_kernel(n_steps,
                 qf_ref, kf_ref, vf_ref, gf_ref, qb_ref, kb_ref, vb_ref, gbb_ref, s0_ref,
                 of_ref, ob_ref, sfin_ref, s_scr):
    h = pl.program_id(1)
    step = pl.program_id(2)
    c = DN_CHUNK
    n_chunks = qf_ref.shape[0] // c

    @pl.when(step == 0)
    def _():
        s_scr[...] = s0_ref[...]

    ri = lax.broadcasted_iota(jnp.int32, (c, c), 0)
    ci = lax.broadcasted_iota(jnp.int32, (c, c), 1)
    lane = lax.broadcasted_iota(jnp.int32, (c, LANES), 1)
    ones_row = jnp.ones((c, 3 * LANES), BF16)

    def pick(tile, idx):
        return jnp.sum(jnp.where(lane == idx, tile, 0.0), axis=1, keepdims=True)

    def chunk(direction, q_ref, k_ref, v_ref, g_ref, o_ref, cidx):
        sl = pl.ds(cidx * c, c)
        q = q_ref[sl, :]
        k = k_ref[sl, :]
        v = v_ref[sl, :]
        gt = g_ref[sl, :]
        gc = pick(gt, direction * DN_HEADS + h)
        beta = pick(gt, (2 + direction) * DN_HEADS + h)
        if direction == 0:
            incl, strict, edge = ri >= ci, ri > ci, c - 1
        else:
            incl, strict, edge = ri <= ci, ri < ci, 0
        g_last = jnp.sum(jnp.where(lax.broadcasted_iota(jnp.int32, (c, 1), 0) == edge, gc, 0.0),
                         axis=0, keepdims=True)
        gcl = jnp.where(lane == 0, gc, 0.0)
        hi, mid, lo = _split3(gcl)
        gc_row = lax.dot_general(ones_row, jnp.concatenate([hi, mid, lo], axis=1),
                                 (((1,), (1,)), ((), ())), preferred_element_type=F32)
        dec = jnp.where(incl, jnp.exp(jnp.where(incl, gc - gc_row, 0.0)), 0.0)
        kbeta = k * beta
        tmat = jnp.where(strict, _dot_nt(kbeta, k) * dec, 0.0)
        attn = _dot_nt(q, k) * dec
        ainv = _tri_inverse(tmat, ri, ci)
        egc = jnp.exp(gc)
        uw = _bdot(ainv, jnp.concatenate([v * beta, kbeta * egc], axis=1))
        u = uw[:, :LANES]
        w = uw[:, LANES:]
        k_dec = k * jnp.exp(g_last - gc)
        q_dec = q * egc
        state = s_scr[direction]
        ws_qs = _bdot(jnp.concatenate([w, q_dec], axis=0), state)
        v_new = u - ws_qs[:c]
        o_ref[sl, :] = ws_qs[c:] + _bdot(attn, v_new)
        s_scr[direction] = state * jnp.exp(g_last) + _dot_tn(k_dec, v_new)

    for i in range(n_chunks):
        chunk(0, qf_ref, kf_ref, vf_ref, gf_ref, of_ref, i)
        chunk(1, qb_ref, kb_ref, vb_ref, gbb_ref, ob_ref, n_chunks - 1 - i)

    @pl.when(step == n_steps - 1)
    def _():
        sfin_ref[...] = s_scr[...]


def _dn_scan(qn, kn, vv, gb, s0, batch, seq):
    rows, d = qn.shape
    tt = SCAN_TILE
    n_steps = seq // tt
    nh = DN_HEADS

    def fwd(col):
        return pl.BlockSpec((tt, LANES), lambda b, h, s: (b * n_steps + s, h if col else 0))

    def bwd(col):
        return pl.BlockSpec((tt, LANES), lambda b, h, s: (b * n_steps + n_steps - 1 - s, h if col else 0))

    st_spec = pl.BlockSpec((None, None, 2, LANES, LANES), lambda b, h, s: (b, h, 0, 0, 0))
    return pl.pallas_call(
        functools.partial(_scan_kernel, n_steps),
        grid=(batch, nh, n_steps),
        in_specs=[fwd(True), fwd(True), fwd(True), fwd(False),
                  bwd(True), bwd(True), bwd(True), bwd(False), st_spec],
        out_specs=[fwd(True), bwd(True), st_spec],
        out_shape=[jax.ShapeDtypeStruct((rows, d), F32), jax.ShapeDtypeStruct((rows, d), F32),
                   jax.ShapeDtypeStruct((batch, nh, 2, LANES, LANES), F32)],
        scratch_shapes=[pltpu.VMEM((2, LANES, LANES), F32)],
        compiler_params=_cparams(("arbitrary", "arbitrary", "arbitrary")),
        name="dn_scan",
    )(qn, kn, vv, gb, qn, kn, vv, gb, s0)


def _na_kernel(n_rows, q_ref, k_ref, v_ref, kc_ref, vc_ref, bias_ref, o_ref):
    t = pl.program_id(2)
    kw = NA_KROWS * GRID_W
    ks = jnp.clip(t * NA_QROWS - NA_WIN_R // 2, 0, n_rows - NA_KROWS)
    start = pl.multiple_of(ks * GRID_W, GRID_W)
    q = q_ref[...] * (NA_DH ** -0.5)
    kwin = k_ref[pl.ds(start, kw), :].astype(BF16)
    vwin = v_ref[pl.ds(start, kw), :].astype(BF16)
    kc = kc_ref[...].astype(BF16)
    vc = vc_ref[...].astype(BF16)
    lane = lax.broadcasted_iota(jnp.int32, q.shape, 1)
    outs = []
    for hh in range(2):
        sel = (lane < NA_DH) if hh == 0 else (lane >= NA_DH)
        qh = jnp.where(sel, q, 0.0).astype(BF16)
        s_loc = lax.dot_general(qh, kwin, (((1,), (1,)), ((), ())),
                                preferred_element_type=F32) + bias_ref[hh]
        s_ctx = lax.dot_general(qh, kc, (((1,), (1,)), ((), ())), preferred_element_type=F32)
        m = jnp.maximum(jnp.max(s_loc, axis=1, keepdims=True), jnp.max(s_ctx, axis=1, keepdims=True))
        p_loc = jnp.exp(s_loc - m)
        p_ctx = jnp.exp(s_ctx - m)
        denom = jnp.sum(p_loc, axis=1, keepdims=True) + jnp.sum(p_ctx, axis=1, keepdims=True)
        o = (jnp.dot(p_loc.astype(BF16), vwin, preferred_element_type=F32)
             + jnp.dot(p_ctx.astype(BF16), vc, preferred_element_type=F32)) / denom
        outs.append(o)
    o_ref[...] = jnp.where(lane < NA_DH, outs[0], outs[1])


def _na_attention(p, pc, bias, batch, seq, ctx_len, q_col, k_col, v_col):
    rows = p.shape[0]
    n_rows = seq // GRID_W
    qt = NA_QROWS * GRID_W
    n_tiles = n_rows // NA_QROWS
    kw = NA_KROWS * GRID_W
    n_pairs = NA_HEADS // 2

    def geom(t):
        return jnp.where(t == 0, 0, jnp.where(t == n_tiles - 1, 2, 1))

    return pl.pallas_call(
        functools.partial(_na_kernel, n_rows),
        grid=(batch, n_pairs, n_tiles),
        in_specs=[pl.BlockSpec((qt, LANES), lambda b, pr, t: (b * n_tiles + t, q_col + pr)),
                  pl.BlockSpec((seq, LANES), lambda b, pr, t: (b, k_col + pr)),
                  pl.BlockSpec((seq, LANES), lambda b, pr, t: (b, v_col + pr)),
                  pl.BlockSpec((ctx_len, LANES), lambda b, pr, t: (b, k_col + pr)),
                  pl.BlockSpec((ctx_len, LANES), lambda b, pr, t: (b, v_col + pr)),
                  pl.BlockSpec((None, 2, qt, kw), lambda b, pr, t: (geom(t), pr, 0, 0))],
        out_specs=pl.BlockSpec((qt, LANES), lambda b, pr, t: (b * n_tiles + t, pr)),
        out_shape=jax.ShapeDtypeStruct((rows, n_pairs * LANES), F32),
        compiler_params=_cparams(("arbitrary", "arbitrary", "arbitrary")),
        name="na_attention",
    )(p, p, p, pc, pc, bias)


def _ctx_attn_kernel(q_ref, k_ref, v_ref, o_ref):
    q = q_ref[...] * (NA_DH ** -0.5)
    k = k_ref[...].astype(BF16)
    v = v_ref[...].astype(BF16)
    lane = lax.broadcasted_iota(jnp.int32, q.shape, 1)
    outs = []
    for hh in range(2):
        sel = (lane < NA_DH) if hh == 0 else (lane >= NA_DH)
        qh = jnp.where(sel, q, 0.0).astype(BF16)
        s = lax.dot_general(qh, k, (((1,), (1,)), ((), ())), preferred_element_type=F32)
        pm = jnp.exp(s - jnp.max(s, axis=1, keepdims=True))
        outs.append(jnp.dot(pm.astype(BF16), v, preferred_element_type=F32)
                    / jnp.sum(pm, axis=1, keepdims=True))
    o_ref[...] = jnp.where(lane < NA_DH, outs[0], outs[1])


def _ctx_attention(pc, batch, ctx_len, q_col, k_col, v_col):
    n_pairs = NA_HEADS // 2
    return pl.pallas_call(
        _ctx_attn_kernel,
        grid=(batch, n_pairs),
        in_specs=[pl.BlockSpec((ctx_len, LANES), lambda b, pr: (b, q_col + pr)),
                  pl.BlockSpec((ctx_len, LANES), lambda b, pr: (b, k_col + pr)),
                  pl.BlockSpec((ctx_len, LANES), lambda b, pr: (b, v_col + pr))],
        out_specs=pl.BlockSpec((ctx_len, LANES), lambda b, pr: (b, pr)),
        out_shape=jax.ShapeDtypeStruct((pc.shape[0], n_pairs * LANES), F32),
        compiler_params=_cparams(("arbitrary", "arbitrary")),
        name="ctx_attention",
    )(pc, pc, pc)


def _na_bias_tables(rpb, n_rows):
    n_tiles = n_rows // NA_QROWS
    col = jnp.arange(GRID_W)
    c0 = jnp.clip(col - NA_WIN_W // 2, 0, GRID_W - NA_WIN_W)
    col_in = (col[None, :] >= c0[:, None]) & (col[None, :] < c0[:, None] + NA_WIN_W)
    coff = jnp.clip(col[None, :] - col[:, None] + (NA_WIN_W - 1), 0, 2 * NA_WIN_W - 2)
    tabs = []
    for t in (0, 1, n_tiles - 1):
        rs = t * NA_QROWS
        ks = min(max(rs - NA_WIN_R // 2, 0), n_rows - NA_KROWS)
        qrow = rs + jnp.arange(NA_QROWS)
        krow = ks + jnp.arange(NA_KROWS)
        r0 = jnp.clip(qrow - NA_WIN_R // 2, 0, n_rows - NA_WIN_R)
        row_in = (krow[None, :] >= r0[:, None]) & (krow[None, :] < r0[:, None] + NA_WIN_R)
        roff = jnp.clip(krow[None, :] - qrow[:, None] + (NA_WIN_R - 1), 0, 2 * NA_WIN_R - 2)
        b = rpb[:, roff[:, None, :, None], coff[None, :, None, :]]
        ok = row_in[:, None, :, None] & col_in[None, :, None, :]
        b = jnp.where(ok[None], b, -jnp.inf)
        tabs.append(b.reshape(rpb.shape[0], NA_QROWS * GRID_W, NA_KROWS * GRID_W))
    return jnp.stack(tabs).astype(F32)


def _merge_kernel(of_ref, ob_ref, z_ref, na_ref, gd_ref, gn_ref, x_ref, dnw_ref, wpa_ref, wpb_ref,
                  wout_ref, gpost_ref, g1_ref, o_ref, dn_scr):
    o = of_ref[...] + ob_ref[...]
    z = z_ref[...]
    for hh in range(DN_HEADS):
        sl = slice(hh * LANES, (hh + 1) * LANES)
        oh = o[:, sl]
        oh = oh * lax.rsqrt(jnp.mean(oh * oh, axis=-1, keepdims=True) + EPS) * dnw_ref[...]
        zh = z[:, sl]
        dn_scr[:, sl] = (oh * (zh * jax.nn.sigmoid(zh))).astype(BF16)
    y = (jax.nn.sigmoid(gd_ref[...]) * jnp.dot(dn_scr[...], wpa_ref[...], preferred_element_type=F32)
         + jax.nn.sigmoid(gn_ref[...]) * _bdot(na_ref[...], wpb_ref[...]))
    out = _bdot(y, wout_ref[...])
    o_ref[...] = x_ref[...] + g1_ref[...] * _rms(out, gpost_ref[...])


def _merge(o_f, o_b, p, na_o, x2, dn_norm, w_pa, w_pb, w_out, gpost, mod, mod_row, g1_blk, tm):
    rows, d = x2.shape
    nw = na_o.shape[1]
    row_blk = lambda c: pl.BlockSpec((tm, d), lambda i: (i, c))
    const = lambda shape: pl.BlockSpec(shape, lambda i: (0,) * len(shape))
    return pl.pallas_call(
        _merge_kernel,
        grid=(rows // tm,),
        in_specs=[row_blk(0), row_blk(0), row_blk(3), pl.BlockSpec((tm, nw), lambda i: (i, 0)),
                  row_blk(4), row_blk(5), row_blk(0),
                  const((1, LANES)), const((d, d)), const((nw, d)), const((d, d)), const((1, d)),
                  pl.BlockSpec((None, 1, d), lambda i: (mod_row(i), 0, g1_blk))],
        out_specs=row_blk(0),
        out_shape=jax.ShapeDtypeStruct((rows, d), F32),
        scratch_shapes=[pltpu.VMEM((tm, d), BF16)],
        compiler_params=_cparams(("arbitrary",)),
        name="merge",
    )(o_f, o_b, p, na_o, p, p, x2, dn_norm, w_pa, w_pb, w_out, gpost, mod)


def _ffn_kernel(n_f, x_ref, gpre_ref, sc_ref, sh_ref, w1_ref, w3_ref, w2_ref, gpost_ref, g2_ref,
                o_ref, h_scr, acc_scr):
    j = pl.program_id(1)

    @pl.when(j == 0)
    def _():
        h = _rms(x_ref[...], gpre_ref[...]) * (1.0 + sc_ref[...]) + sh_ref[...]
        h_scr[...] = h.astype(BF16)
        acc_scr[...] = jnp.zeros_like(acc_scr)

    h = h_scr[...]
    a = jnp.dot(h, w1_ref[...], preferred_element_type=F32)
    b = jnp.dot(h, w3_ref[...], preferred_element_type=F32)
    acc_scr[...] += _bdot(a * jax.nn.sigmoid(a) * b, w2_ref[...])

    @pl.when(j == n_f - 1)
    def _():
        o_ref[...] = x_ref[...] + g2_ref[...] * _rms(acc_scr[...], gpost_ref[...])


def _dense_ffn(x2, gpre, gpost, mod, mod_row, w1, w3, w2, tm, tf):
    rows, d = x2.shape
    f = w1.shape[1]
    n_f = f // tf
    modspec = lambda blk: pl.BlockSpec((None, 1, d), lambda i, j: (mod_row(i), 0, blk))
    return pl.pallas_call(
        functools.partial(_ffn_kernel, n_f),
        grid=(rows // tm, n_f),
        in_specs=[pl.BlockSpec((tm, d), lambda i, j: (i, 0)),
                  pl.BlockSpec((1, d), lambda i, j: (0, 0)),
                  modspec(4), modspec(3),
                  pl.BlockSpec((d, tf), lambda i, j: (0, j)),
                  pl.BlockSpec((d, tf), lambda i, j: (0, j)),
                  pl.BlockSpec((tf, d), lambda i, j: (j, 0)),
                  pl.BlockSpec((1, d), lambda i, j: (0, 0)),
                  modspec(5)],
        out_specs=pl.BlockSpec((tm, d), lambda i, j: (i, 0)),
        out_shape=jax.ShapeDtypeStruct((rows, d), F32),
        scratch_shapes=[pltpu.VMEM((tm, d), BF16), pltpu.VMEM((tm, d), F32)],
        compiler_params=_cparams(("arbitrary", "arbitrary")),
        name="dense_ffn",
    )(x2, gpre, mod, mod, w1, w3, w2, gpost, mod)


def _router_kernel(x_ref, gpre_ref, sc_ref, sh_ref, r_ref, h_ref, gate_ref):
    h = _rms(x_ref[...], gpre_ref[...]) * (1.0 + sc_ref[...]) + sh_ref[...]
    h_ref[...] = h.astype(BF16)
    logits = jnp.dot(h, r_ref[...], precision=lax.Precision.HIGHEST, preferred_element_type=F32)
    lane = lax.broadcasted_iota(jnp.int32, logits.shape, 1)
    neg = -jnp.inf
    l1 = jnp.where(lane < N_EXPERTS, logits, neg)
    m1 = jnp.max(l1, axis=1, keepdims=True)
    i1 = jnp.min(jnp.where(l1 == m1, lane, LANES), axis=1, keepdims=True)
    l2 = jnp.where(lane == i1, neg, l1)
    m2 = jnp.max(l2, axis=1, keepdims=True)
    i2 = jnp.min(jnp.where(l2 == m2, lane, LANES), axis=1, keepdims=True)
    e = jnp.exp(m2 - m1)
    w1 = 1.0 / (1.0 + e)
    w2 = e / (1.0 + e)
    out = jnp.where(lane == 0, i1.astype(F32), 0.0)
    out = jnp.where(lane == 1, i2.astype(F32), out)
    out = jnp.where(lane == 2, w1, out)
    out = jnp.where(lane == 3, w2, out)
    gate_ref[...] = out


def _router(x2, gpre, mod, mod_row, router_pad, tm):
    rows, d = x2.shape
    modspec = lambda blk: pl.BlockSpec((None, 1, d), lambda i: (mod_row(i), 0, blk))
    return pl.pallas_call(
        _router_kernel,
        grid=(rows // tm,),
        in_specs=[pl.BlockSpec((tm, d), lambda i: (i, 0)),
                  pl.BlockSpec((1, d), lambda i: (0, 0)),
                  modspec(4), modspec(3),
                  pl.BlockSpec((d, LANES), lambda i: (0, 0))],
        out_specs=[pl.BlockSpec((tm, d), lambda i: (i, 0)),
                   pl.BlockSpec((tm, LANES), lambda i: (i, 0))],
        out_shape=[jax.ShapeDtypeStruct((rows, d), BF16),
                   jax.ShapeDtypeStruct((rows, LANES), F32)],
        compiler_params=_cparams(("arbitrary",)),
        name="moe_router",
    )(x2, gpre, mod, mod, router_pad)


def _expert_kernel(n_f, te_ref, nv_ref, x_ref, w1_ref, w3_ref, w2_ref, o_ref, acc_scr):
    i = pl.program_id(0)
    j = pl.program_id(1)
    valid = i < nv_ref[0]

    @pl.when(j == 0)
    def _():
        acc_scr[...] = jnp.zeros_like(acc_scr)

    @pl.when(valid)
    def _():
        x = x_ref[...]
        a = jnp.dot(x, w1_ref[...], preferred_element_type=F32)
        b = jnp.dot(x, w3_ref[...], preferred_element_type=F32)
        acc_scr[...] += _bdot(a * jax.nn.sigmoid(a) * b, w2_ref[...])

    @pl.when(j == n_f - 1)
    def _():
        o_ref[...] = acc_scr[...]


def _expert_ffn(xs, tile_expert, n_valid, w1, w3, w2, tf):
    prow, d = xs.shape
    f = w1.shape[2]
    n_f = f // tf
    n_tiles = prow // MOE_TM
    grid_spec = pltpu.PrefetchScalarGridSpec(
        num_scalar_prefetch=2,
        grid=(n_tiles, n_f),
        in_specs=[pl.BlockSpec((MOE_TM, d), lambda i, j, te, nv: (i, 0)),
                  pl.BlockSpec((None, d, tf), lambda i, j, te, nv: (te[i], 0, j)),
                  pl.BlockSpec((None, d, tf), lambda i, j, te, nv: (te[i], 0, j)),
                  pl.BlockSpec((None, tf, d), lambda i, j, te, nv: (te[i], j, 0))],
        out_specs=pl.BlockSpec((MOE_TM, d), lambda i, j, te, nv: (i, 0)),
        scratch_shapes=[pltpu.VMEM((MOE_TM, d), F32)],
    )
    return pl.pallas_call(
        functools.partial(_expert_kernel, n_f),
        grid_spec=grid_spec,
        out_shape=jax.ShapeDtypeStruct((prow, d), F32),
        compiler_params=_cparams(("arbitrary", "arbitrary")),
        name="moe_experts",
    )(tile_expert, n_valid, xs, w1, w3, w2)


def _combine_kernel(y1_ref, y2_ref, gate_ref, x_ref, gpost_ref, g2_ref, o_ref):
    gt = gate_ref[...]
    lane = lax.broadcasted_iota(jnp.int32, gt.shape, 1)
    w1 = jnp.sum(jnp.where(lane == 2, gt, 0.0), axis=1, keepdims=True)
    w2 = jnp.sum(jnp.where(lane == 3, gt, 0.0), axis=1, keepdims=True)
    y = w1 * y1_ref[...] + w2 * y2_ref[...]
    o_ref[...] = x_ref[...] + g2_ref[...] * _rms(y, gpost_ref[...])


def _moe_combine(y1, y2, gates, x2, gpost, mod, mod_row, tm):
    rows, d = x2.shape
    rb = pl.BlockSpec((tm, d), lambda i: (i, 0))
    return pl.pallas_call(
        _combine_kernel,
        grid=(rows // tm,),
        in_specs=[rb, rb, pl.BlockSpec((tm, LANES), lambda i: (i, 0)), rb,
                  pl.BlockSpec((1, d), lambda i: (0, 0)),
                  pl.BlockSpec((None, 1, d), lambda i: (mod_row(i), 0, 5))],
        out_specs=rb,
        out_shape=jax.ShapeDtypeStruct((rows, d), F32),
        compiler_params=_cparams(("arbitrary",)),
        name="moe_combine",
    )(y1, y2, gates, x2, gpost, mod)


def _moe_schedule(gates, n_tokens):
    idx = gates[:, 0:2].astype(jnp.int32)
    flat_e = idx.reshape(-1)
    onehot = (flat_e[:, None] == jnp.arange(N_EXPERTS)[None, :]).astype(jnp.int32)
    csum = jnp.cumsum(onehot, axis=0)
    counts = csum[-1]
    rank = jnp.take_along_axis(csum, flat_e[:, None], axis=1)[:, 0] - 1
    padded = ((counts + MOE_TM - 1) // MOE_TM) * MOE_TM
    ends = jnp.cumsum(padded)
    starts = ends - padded
    dest = starts[flat_e] + rank
    n_rows = 2 * n_tokens + N_EXPERTS * MOE_TM
    n_tiles = n_rows // MOE_TM
    row_token = jnp.zeros((n_rows,), jnp.int32).at[dest].set(jnp.arange(2 * n_tokens, dtype=jnp.int32) // 2)
    tile_start = jnp.arange(n_tiles, dtype=jnp.int32) * MOE_TM
    tile_expert = jnp.minimum(jnp.searchsorted(ends, tile_start, side="right"), N_EXPERTS - 1).astype(jnp.int32)
    n_valid = (ends[-1] // MOE_TM).astype(jnp.int32).reshape(1)
    return row_token, tile_expert, n_valid, dest.reshape(n_tokens, 2)


def _rope_tables(seq):
    t = jnp.arange(seq)
    row = (t // GRID_W).astype(F32)
    col = (t % GRID_W).astype(F32)
    n_freq = DN_DK // 4
    inv = ROPE_BASE ** (-jnp.arange(n_freq, dtype=F32) / n_freq)
    ang = jnp.concatenate([row[:, None] * inv, col[:, None] * inv], axis=-1)
    cos, sin = jnp.cos(ang), jnp.sin(ang)
    return jnp.concatenate([cos, cos], axis=-1), jnp.concatenate([-sin, sin], axis=-1)


def kernel(x, c, ctx, c_ctx, ada_w, ada_b, norm_mix_pre, norm_mix_post, norm_ffn_pre, norm_ffn_post,
           w_in, dn_conv, dn_a_log, dn_dt_bias, dn_norm, na_rpb, w_branch_dn, w_branch_na, w_out,
           ffn_w1, ffn_w3, ffn_w2, moe_router, moe_w1, moe_w3, moe_w2):
    batch, seq, d = x.shape
    ctx_len = ctx.shape[1]
    depth = w_in.shape[0]
    nh = DN_HEADS
    dn_w = nh * DN_DK
    na_w = NA_HEADS * NA_DH
    n_rows = seq // GRID_W
    assert d == dn_w and seq % SCAN_TILE == 0 and ctx_len % SCAN_TILE == 0 and n_rows % NA_QROWS == 0

    c_rows = jnp.zeros((SUBLANES, d), F32).at[:batch].set(c).at[batch].set(c_ctx)
    mod_all = _mod_vectors(c_rows, ada_w, ada_b)
    cos2, sin2 = _rope_tables(seq)
    ones_t = jnp.ones((SCAN_TILE, LANES), F32)

    x2 = x.reshape(batch * seq, d)
    xc2 = ctx.reshape(batch * ctx_len, d)
    lat_tm = 1024
    lat_row = lambda tm: (lambda i: i // (seq // tm))
    ctx_row = lambda i: batch

    q_col, k_col, v_col = (4 * dn_w + 2 * d) // LANES, (4 * dn_w + 2 * d + na_w) // LANES, \
        (4 * dn_w + 2 * d + 2 * na_w) // LANES

    for l in range(depth):
        last = l == depth - 1
        mod = mod_all[l].reshape(SUBLANES, 1, 6 * d)
        wl = w_in[l]
        o_ab = 4 * dn_w
        o_na = o_ab + 4 * nh
        o_gate = o_na + 3 * na_w
        w_main = jnp.concatenate([wl[:, :o_ab], wl[:, o_gate:], wl[:, o_na:o_gate]], axis=1).astype(BF16)
        wab = jnp.pad(wl[:, o_ab:o_na], ((0, 0), (0, LANES - 4 * nh)))
        wab_hi = wab.astype(BF16)
        wab_lo = (wab - wab_hi.astype(F32)).astype(BF16)
        gpre = norm_mix_pre[l].reshape(1, d)
        gpost = norm_mix_post[l].reshape(1, d)

        p, ab = _in_proj(x2, gpre, mod, lat_row(lat_tm), 1, 0, w_main, wab_hi, wab_lo, lat_tm, 512)
        pc, abc = _in_proj(xc2, gpre, mod, ctx_row, 1, 0, w_main, wab_hi, wab_lo, batch * ctx_len, 512)

        conv_w3 = jnp.pad(dn_conv[l].T.reshape(DN_CONV, 3, dn_w).transpose(1, 0, 2),
                          ((0, 0), (0, SUBLANES - DN_CONV), (0, 0)))
        gpar = jnp.zeros((SUBLANES, LANES), F32)
        gpar = gpar.at[0, :2 * nh].set(-jnp.exp(dn_a_log[l].reshape(-1)))
        gpar = gpar.at[1, :2 * nh].set(dn_dt_bias[l].reshape(-1))

        qc_, kc_, vc_, gbc = _dn_prep(pc, abc, conv_w3, gpar, ones_t, ones_t, batch, ctx_len, False)
        ql_, kl_, vl_, gbl = _dn_prep(p, ab, conv_w3, gpar, cos2, sin2, batch, seq, True)
        s0 = jnp.zeros((batch, nh, 2, LANES, LANES), F32)
        oc_f, oc_b, s_ctx = _dn_scan(qc_, kc_, vc_, gbc, s0, batch, ctx_len)
        ol_f, ol_b, _ = _dn_scan(ql_, kl_, vl_, gbl, s_ctx, batch, seq)

        bias = _na_bias_tables(na_rpb[l], n_rows)
        na_lat = _na_attention(p, pc, bias, batch, seq, ctx_len, q_col, k_col, v_col)

        dnw = dn_norm[l].reshape(1, LANES)
        w_pa = w_branch_dn[l].astype(BF16)
        w_pb = w_branch_na[l].astype(BF16)
        w_o = w_out[l].astype(BF16)
        x2 = _merge(ol_f, ol_b, p, na_lat, x2, dnw, w_pa, w_pb, w_o, gpost, mod, lat_row(256), 2, 256)

        gfpre = norm_ffn_pre[l].reshape(1, d)
        gfpost = norm_ffn_post[l].reshape(1, d)
        if l % 2 == 0:
            w1 = ffn_w1[l // 2].astype(BF16)
            w3 = ffn_w3[l // 2].astype(BF16)
            w2 = ffn_w2[l // 2].astype(BF16)
            tf = w1.shape[1] // 2
            x2 = _dense_ffn(x2, gfpre, gfpost, mod, lat_row(512), w1, w3, w2, 512, tf)
        else:
            rpad = jnp.pad(moe_router[l // 2], ((0, 0), (0, LANES - N_EXPERTS)))
            hb, gates = _router(x2, gfpre, mod, lat_row(512), rpad, 512)
            n_tok = batch * seq
            row_token, tile_expert, n_valid, dest = _moe_schedule(gates, n_tok)
            xs = jnp.take(hb, row_token, axis=0)
            ys = _expert_ffn(xs, tile_expert, n_valid, moe_w1[l // 2].astype(BF16),
                             moe_w3[l // 2].astype(BF16), moe_w2[l // 2].astype(BF16), 512)
            y1 = jnp.take(ys, dest[:, 0], axis=0)
            y2 = jnp.take(ys, dest[:, 1], axis=0)
            x2 = _moe_combine(y1, y2, gates, x2, gfpost, mod, lat_row(512), 512)

        if not last:
            na_ctx = _ctx_attention(pc, batch, ctx_len, q_col, k_col, v_col)
            xc2 = _merge(oc_f, oc_b, pc, na_ctx, xc2, dnw, w_pa, w_pb, w_o, gpost, mod, ctx_row, 2, 256)
            if l % 2 == 0:
                xc2 = _dense_ffn(xc2, gfpre, gfpost, mod, ctx_row, w1, w3, w2, 512, tf)
            else:
                raise NotImplementedError("context MoE path is not needed for an even depth")
    return x2.reshape(batch, seq, d)
```

```python
import functools

import numpy as np
import jax
import jax.numpy as jnp
from jax import lax
from jax.experimental import pallas as pl
from jax.experimental.pallas import tpu as pltpu

F32 = jnp.float32
BF16 = jnp.bfloat16

GRID_W = 64
DN_HEADS = 8
DN_DK = 128
DN_CONV = 5
DN_CHUNK = 64
NA_HEADS = 8
NA_DH = 64
NA_WIN_R = 8
NA_WIN_W = 16
ROPE_BASE = 10000.0
N_EXPERTS = 8
EPS = 1e-6

LANES = 128
SUBLANES = 8
VMEM_LIMIT = 56 * 1024 * 1024

SCAN_TILE = 256
SCAN_HEADS = 4
NA_QROWS = 4
NA_KROWS = NA_QROWS + 8
MOE_TM = 512


def _cparams(sem):
    return pltpu.CompilerParams(dimension_semantics=sem, vmem_limit_bytes=VMEM_LIMIT)


def _bdot(a, b):
    return jnp.dot(a.astype(BF16), b.astype(BF16), preferred_element_type=F32)


def _dot_nt(a, b):
    return lax.dot_general(a.astype(BF16), b.astype(BF16), (((1,), (1,)), ((), ())),
                           preferred_element_type=F32)


def _dot_tn(a, b):
    return lax.dot_general(a.astype(BF16), b.astype(BF16), (((0,), (0,)), ((), ())),
                           preferred_element_type=F32)


def _split3(x):
    hi = x.astype(BF16)
    r = x - hi.astype(F32)
    mid = r.astype(BF16)
    lo = (r - mid.astype(F32)).astype(BF16)
    return hi, mid, lo


def _rms(x, gain):
    return x * lax.rsqrt(jnp.mean(x * x, axis=-1, keepdims=True) + EPS) * gain


def _mod_kernel(c_ref, w_ref, b_ref, o_ref):
    c = c_ref[...]
    s = c * jax.nn.sigmoid(c)
    o_ref[0] = jnp.dot(s, w_ref[0], precision=lax.Precision.HIGHEST,
                       preferred_element_type=F32) + b_ref[0]


def _mod_vectors(c_rows, ada_w, ada_b):
    depth, d, n = ada_w.shape
    tn = 1536
    return pl.pallas_call(
        _mod_kernel,
        grid=(depth, n // tn),
        in_specs=[pl.BlockSpec((SUBLANES, d), lambda l, j: (0, 0)),
                  pl.BlockSpec((1, d, tn), lambda l, j: (l, 0, j)),
                  pl.BlockSpec((1, 1, tn), lambda l, j: (l, 0, j))],
        out_specs=pl.BlockSpec((1, SUBLANES, tn), lambda l, j: (l, 0, j)),
        out_shape=jax.ShapeDtypeStruct((depth, SUBLANES, n), F32),
        compiler_params=_cparams(("arbitrary", "arbitrary")),
        name="mod_vectors",
    )(c_rows, ada_w, ada_b.reshape(depth, 1, n))


def _inproj_kernel(x_ref, g_ref, sc_ref, sh_ref, w_ref, wab_hi_ref, wab_lo_ref,
                   o_ref, ab_ref, h_scr, hlo_scr):
    j = pl.program_id(1)

    @pl.when(j == 0)
    def _():
        h = _rms(x_ref[...], g_ref[...]) * (1.0 + sc_ref[...]) + sh_ref[...]
        hi = h.astype(BF16)
        lo = (h - hi.astype(F32)).astype(BF16)
        h_scr[...] = hi
        hlo_scr[...] = lo
        ab_ref[...] = (jnp.dot(hi, wab_hi_ref[...], preferred_element_type=F32)
                       + jnp.dot(lo, wab_hi_ref[...], preferred_element_type=F32)
                       + jnp.dot(hi, wab_lo_ref[...], preferred_element_type=F32))

    o_ref[...] = jnp.dot(h_scr[...], w_ref[...], preferred_element_type=F32)


def _in_proj(x2, gain, mod, mod_row, sc_blk, sh_blk, w_main, wab_hi, wab_lo, tm, tn):
    rows, d = x2.shape
    n = w_main.shape[1]
    return pl.pallas_call(
        _inproj_kernel,
        grid=(rows // tm, n // tn),
        in_specs=[pl.BlockSpec((tm, d), lambda i, j: (i, 0)),
                  pl.BlockSpec((1, d), lambda i, j: (0, 0)),
                  pl.BlockSpec((None, 1, d), lambda i, j: (mod_row(i), 0, sc_blk)),
                  pl.BlockSpec((None, 1, d), lambda i, j: (mod_row(i), 0, sh_blk)),
                  pl.BlockSpec((d, tn), lambda i, j: (0, j)),
                  pl.BlockSpec((d, LANES), lambda i, j: (0, 0)),
                  pl.BlockSpec((d, LANES), lambda i, j: (0, 0))],
        out_specs=[pl.BlockSpec((tm, tn), lambda i, j: (i, j)),
                   pl.BlockSpec((tm, LANES), lambda i, j: (i, 0))],
        out_shape=[jax.ShapeDtypeStruct((rows, n), F32),
                   jax.ShapeDtypeStruct((rows, LANES), F32)],
        scratch_shapes=[pltpu.VMEM((tm, d), BF16), pltpu.VMEM((tm, d), BF16)],
        compiler_params=_cparams(("arbitrary", "arbitrary")),
        name="in_proj",
    )(x2, gain, mod, mod, w_main, wab_hi, wab_lo)


def _prep_kernel(rope, n_tiles,
                 q_ref, qp_ref, qn_ref, k_ref, kp_ref, kn_ref, v_ref, vp_ref, vn_ref,
                 cw_ref, ab_ref, gpar_ref, cos_ref, sin_ref,
                 qo_ref, ko_ref, vo_ref, gb_ref, gbt_ref):
    t = pl.program_id(1)
    tt = q_ref.shape[0]
    first = t == 0
    last = t == n_tiles - 1
    pad = DN_CONV // 2

    def conv_silu(m_ref, p_ref, n_ref, w, sl):
        prev = jnp.where(first, 0.0, p_ref[:, sl])
        nxt = jnp.where(last, 0.0, n_ref[:, sl])
        xp = jnp.concatenate([prev, m_ref[:, sl], nxt], axis=0)
        acc = xp[SUBLANES - pad:SUBLANES - pad + tt] * w[0:1]
        for i in range(1, DN_CONV):
            o = SUBLANES - pad + i
            acc = acc + xp[o:o + tt] * w[i:i + 1]
        return acc * jax.nn.sigmoid(acc)

    def l2n(x):
        return x * lax.rsqrt(jnp.sum(x * x, axis=-1, keepdims=True) + EPS)

    def rot(x):
        if not rope:
            return x
        return x * cos_ref[...] + pltpu.roll(x, LANES // 2, 1) * sin_ref[...]

    for hh in range(DN_HEADS):
        sl = slice(hh * LANES, (hh + 1) * LANES)
        qo_ref[:, sl] = rot(l2n(conv_silu(q_ref, qp_ref, qn_ref, cw_ref[0, :, sl], sl))) * (DN_DK ** -0.5)
        ko_ref[:, sl] = rot(l2n(conv_silu(k_ref, kp_ref, kn_ref, cw_ref[1, :, sl], sl)))
        vo_ref[:, sl] = conv_silu(v_ref, vp_ref, vn_ref, cw_ref[2, :, sl], sl)

    ab = ab_ref[...]
    lane = lax.broadcasted_iota(jnp.int32, ab.shape, 1)
    row = lax.broadcasted_iota(jnp.int32, ab.shape, 0) % DN_CHUNK
    xg = ab + gpar_ref[1:2]
    sp = jnp.maximum(xg, 0.0) + jnp.log1p(jnp.exp(-jnp.abs(xg)))
    g = gpar_ref[0:1] * sp
    beta = jax.nn.sigmoid(ab)
    pre = g
    suf = g
    s = 1
    while s < DN_CHUNK:
        pre = pre + jnp.where(row >= s, pltpu.roll(pre, s, 0), 0.0)
        suf = suf + jnp.where(row < DN_CHUNK - s, pltpu.roll(suf, tt - s, 0), 0.0)
        s *= 2
    nh = DN_HEADS
    gb = jnp.where(lane < nh, pre, jnp.where(lane < 2 * nh, suf, beta))
    gb_ref[...] = gb
    er = lax.broadcasted_iota(jnp.int32, (LANES, 3 * LANES), 0)
    ec = lax.broadcasted_iota(jnp.int32, (LANES, 3 * LANES), 1)
    eye3 = ((ec % LANES) == er).astype(BF16)
    gbt_ref[...] = lax.dot_general(eye3, jnp.concatenate(_split3(gb), axis=1),
                                   (((1,), (1,)), ((), ())), preferred_element_type=F32)


def _dn_prep(p, ab, conv_w3, gpar, cos2, sin2, batch, seq, rope):
    rows = p.shape[0]
    tt = SCAN_TILE
    n_tiles = seq // tt
    hb = tt // SUBLANES
    nblk8 = rows // SUBLANES
    d = DN_HEADS * LANES

    def main(cb):
        return pl.BlockSpec((tt, d), lambda b, t: (b * n_tiles + t, cb))

    def prev(cb):
        return pl.BlockSpec((SUBLANES, d), lambda b, t: (jnp.maximum((b * n_tiles + t) * hb - 1, 0), cb))

    def nxt(cb):
        return pl.BlockSpec((SUBLANES, d),
                            lambda b, t: (jnp.minimum((b * n_tiles + t + 1) * hb, nblk8 - 1), cb))

    in_specs = []
    for cb in range(3):
        in_specs += [main(cb), prev(cb), nxt(cb)]
    in_specs += [
        pl.BlockSpec((3, SUBLANES, d), lambda b, t: (0, 0, 0)),
        pl.BlockSpec((tt, LANES), lambda b, t: (b * n_tiles + t, 0)),
        pl.BlockSpec((SUBLANES, LANES), lambda b, t: (0, 0)),
        pl.BlockSpec((tt, LANES), lambda b, t: (t, 0)),
        pl.BlockSpec((tt, LANES), lambda b, t: (t, 0)),
    ]
    out_full = pl.BlockSpec((tt, d), lambda b, t: (b * n_tiles + t, 0))
    return pl.pallas_call(
        functools.partial(_prep_kernel, rope, n_tiles),
        grid=(batch, n_tiles),
        in_specs=in_specs,
        out_specs=[out_full, out_full, out_full,
                   pl.BlockSpec((tt, LANES), lambda b, t: (b * n_tiles + t, 0)),
                   pl.BlockSpec((LANES, tt), lambda b, t: (0, b * n_tiles + t))],
        out_shape=[jax.ShapeDtypeStruct((rows, d), F32)] * 3
        + [jax.ShapeDtypeStruct((rows, LANES), F32), jax.ShapeDtypeStruct((LANES, rows), F32)],
        compiler_params=_cparams(("arbitrary", "arbitrary")),
        name="dn_prep_rope" if rope else "dn_prep",
    )(p, p, p, p, p, p, p, p, p, conv_w3, ab, gpar, cos2, sin2)


def _scan_kernel(n_steps,
                 qf_ref, kf_ref, vf_ref, gf_ref, gtf_ref, qb_ref, kb_ref, vb_ref, gb_ref, gtb_ref, s0_ref,
                 of_ref, ob_ref, sfin_ref, s_scr):
    hg = pl.program_id(1)
    step = pl.program_id(2)
    c = DN_CHUNK
    n_chunks = qf_ref.shape[0] // c
    refs = ((qf_ref, kf_ref, vf_ref, gf_ref, gtf_ref, of_ref),
            (qb_ref, kb_ref, vb_ref, gb_ref, gtb_ref, ob_ref))

    @pl.when(step == 0)
    def _():
        s_scr[...] = s0_ref[...]

    ri = lax.broadcasted_iota(jnp.int32, (c, c), 0)
    ci = lax.broadcasted_iota(jnp.int32, (c, c), 1)
    lane = lax.broadcasted_iota(jnp.int32, (c, LANES), 1)
    eye = (ri == ci).astype(F32)
    incl = (ri >= ci, ri <= ci)
    strict = (ri > ci, ri < ci)

    def pick(tile, idx):
        return jnp.sum(jnp.where(lane == idx, tile, 0.0), axis=1, keepdims=True)

    sub = lax.broadcasted_iota(jnp.int32, (DN_HEADS, qf_ref.shape[0]), 0)
    gc_rows = {(hh, dr): jnp.sum(jnp.where(sub == hg * SCAN_HEADS + hh,
                                           refs[dr][4][dr * DN_HEADS:(dr + 1) * DN_HEADS, :], 0.0),
                                 axis=0, keepdims=True)
               for hh in range(SCAN_HEADS) for dr in range(2)}

    items = [(hh, dr, cc) for hh in range(SCAN_HEADS) for dr in range(2) for cc in range(n_chunks)]
    st = []
    for hh, dr, cc in items:
        q_ref, k_ref, v_ref, g_ref, gt_ref, _ = refs[dr]
        rs = slice(cc * c, (cc + 1) * c)
        ls = slice(hh * LANES, (hh + 1) * LANES)
        head = hg * SCAN_HEADS + hh
        gtile = g_ref[rs, :]
        gc = pick(gtile, dr * DN_HEADS + head)
        beta = pick(gtile, (2 + dr) * DN_HEADS + head)
        gc_row = gc_rows[(hh, dr)][:, rs]
        edge = c - 1 if dr == 0 else 0
        g_last = gc_row[:, edge:edge + 1]
        q = q_ref[rs, ls]
        k = k_ref[rs, ls]
        v = v_ref[rs, ls]
        dec = jnp.where(incl[dr], jnp.exp(jnp.where(incl[dr], gc - gc_row, 0.0)), 0.0)
        egc = jnp.exp(gc)
        kbeta = k * beta
        st.append(dict(dr=dr, rs=rs, ls=ls, q=q, k=k, dec=dec, kbeta=kbeta,
                       rhs=jnp.concatenate([v * beta, kbeta * egc], axis=1).astype(BF16),
                       k_dec=(k * jnp.exp(g_last - gc)).astype(BF16),
                       q_dec=(q * egc).astype(BF16),
                       e_last=jnp.exp(g_last)))
    for s in st:
        s["tm"] = jnp.where(strict[s["dr"]], _dot_nt(s["kbeta"], s["k"]) * s["dec"], 0.0)
    for s in st:
        s["attn"] = (_dot_nt(s["q"], s["k"]) * s["dec"]).astype(BF16)

    m8 = (ri // 8) == (ci // 8)
    pw = [-jnp.where(m8, s["tm"], 0.0) for s in st]
    p2 = [_bdot(p, p) for p in pw]
    p4 = [_bdot(p, p) for p in p2]
    xs = [eye + p for p in pw]
    xs = [x + _bdot(p, x) for x, p in zip(xs, p2)]
    xs = [x + _bdot(p, x) for x, p in zip(xs, p4)]
    blk = 8
    while blk < c:
        off = ((ri // (2 * blk)) == (ci // (2 * blk))) & ((ri // blk) != (ci // blk))
        lx = [_bdot(jnp.where(off, s["tm"], 0.0), x) for s, x in zip(st, xs)]
        xs = [x - _bdot(x, y) for x, y in zip(xs, lx)]
        blk *= 2
    uw = [_bdot(x, s["rhs"]) for s, x in zip(st, xs)]

    by_key = {it: (s, y) for it, s, y in zip(items, st, uw)}
    chains = [(hh, dr) for hh in range(SCAN_HEADS) for dr in range(2)]
    state = {ch: s_scr[ch[0], ch[1]] for ch in chains}
    for i in range(n_chunks):
        cur = {ch: by_key[(ch[0], ch[1], i if ch[1] == 0 else n_chunks - 1 - i)] for ch in chains}
        wsqs = {ch: _bdot(jnp.concatenate([cur[ch][1][:, LANES:].astype(BF16), cur[ch][0]["q_dec"]], axis=0),
                          state[ch]) for ch in chains}
        v_new = {ch: cur[ch][1][:, :LANES] - wsqs[ch][:c] for ch in chains}
        for ch in chains:
            s = cur[ch][0]
            refs[ch[1]][5][s["rs"], s["ls"]] = wsqs[ch][c:] + _bdot(s["attn"], v_new[ch])
        state = {ch: state[ch] * cur[ch][0]["e_last"] + _dot_tn(cur[ch][0]["k_dec"], v_new[ch])
                 for ch in chains}
    for ch in chains:
        s_scr[ch[0], ch[1]] = state[ch]

    @pl.when(step == n_steps - 1)
    def _():
        sfin_ref[...] = s_scr[...]


def _dn_scan(qn, kn, vv, gb, gbt, s0, batch, seq):
    rows, d = qn.shape
    tt = SCAN_TILE
    n_steps = seq // tt
    n_groups = DN_HEADS // SCAN_HEADS
    w = SCAN_HEADS * LANES
    fwd_t = lambda b, s: b * n_steps + s
    bwd_t = lambda b, s: b * n_steps + n_steps - 1 - s

    def specs(tile):
        wide = pl.BlockSpec((tt, w), lambda b, g, s: (tile(b, s), g))
        return wide, [wide, wide, wide,
                      pl.BlockSpec((tt, LANES), lambda b, g, s: (tile(b, s), 0)),
                      pl.BlockSpec((LANES, tt), lambda b, g, s: (0, tile(b, s)))]

    out_f, in_f = specs(fwd_t)
    out_b, in_b = specs(bwd_t)
    st_spec = pl.BlockSpec((None, SCAN_HEADS, 2, LANES, LANES), lambda b, g, s: (b, g, 0, 0, 0))
    return pl.pallas_call(
        functools.partial(_scan_kernel, n_steps),
        grid=(batch, n_groups, n_steps),
        in_specs=in_f + in_b + [st_spec],
        out_specs=[out_f, out_b, st_spec],
        out_shape=[jax.ShapeDtypeStruct((rows, d), F32), jax.ShapeDtypeStruct((rows, d), F32),
                   jax.ShapeDtypeStruct((batch, DN_HEADS, 2, LANES, LANES), F32)],
        scratch_shapes=[pltpu.VMEM((SCAN_HEADS, 2, LANES, LANES), F32)],
        compiler_params=_cparams(("arbitrary", "arbitrary", "arbitrary")),
        name="dn_scan",
    )(qn, kn, vv, gb, gbt, qn, kn, vv, gb, gbt, s0)


def _na_kernel(n_rows, q_ref, k_ref, v_ref, kc_ref, vc_ref, bias_ref, o_ref):
    t = pl.program_id(2)
    kw = NA_KROWS * GRID_W
    ks = jnp.clip(t * NA_QROWS - NA_WIN_R // 2, 0, n_rows - NA_KROWS)
    start = pl.multiple_of(ks * GRID_W, GRID_W)
    q = q_ref[...] * (NA_DH ** -0.5)
    kwin = k_ref[pl.ds(start, kw), :].astype(BF16)
    vwin = v_ref[pl.ds(start, kw), :].astype(BF16)
    kc = kc_ref[...].astype(BF16)
    vc = vc_ref[...].astype(BF16)
    lane = lax.broadcasted_iota(jnp.int32, q.shape, 1)
    outs = []
    for hh in range(2):
        sel = (lane < NA_DH) if hh == 0 else (lane >= NA_DH)
        qh = jnp.where(sel, q, 0.0).astype(BF16)
        s_loc = lax.dot_general(qh, kwin, (((1,), (1,)), ((), ())),
                                preferred_element_type=F32) + bias_ref[hh]
        s_ctx = lax.dot_general(qh, kc, (((1,), (1,)), ((), ())), preferred_element_type=F32)
        m = jnp.maximum(jnp.max(s_loc, axis=1, keepdims=True), jnp.max(s_ctx, axis=1, keepdims=True))
        p_loc = jnp.exp(s_loc - m)
        p_ctx = jnp.exp(s_ctx - m)
        denom = jnp.sum(p_loc, axis=1, keepdims=True) + jnp.sum(p_ctx, axis=1, keepdims=True)
        o = (jnp.dot(p_loc.astype(BF16), vwin, preferred_element_type=F32)
             + jnp.dot(p_ctx.astype(BF16), vc, preferred_element_type=F32)) / denom
        outs.append(o)
    o_ref[...] = jnp.where(lane < NA_DH, outs[0], outs[1])


def _na_attention(p, pc, bias, batch, seq, ctx_len, q_col, k_col, v_col):
    rows = p.shape[0]
    n_rows = seq // GRID_W
    qt = NA_QROWS * GRID_W
    n_tiles = n_rows // NA_QROWS
    kw = NA_KROWS * GRID_W
    n_pairs = NA_HEADS // 2

    def geom(t):
        return jnp.where(t == 0, 0, jnp.where(t == n_tiles - 1, 2, 1))

    return pl.pallas_call(
        functools.partial(_na_kernel, n_rows),
        grid=(batch, n_pairs, n_tiles),
        in_specs=[pl.BlockSpec((qt, LANES), lambda b, pr, t: (b * n_tiles + t, q_col + pr)),
                  pl.BlockSpec((seq, LANES), lambda b, pr, t: (b, k_col + pr)),
                  pl.BlockSpec((seq, LANES), lambda b, pr, t: (b, v_col + pr)),
                  pl.BlockSpec((ctx_len, LANES), lambda b, pr, t: (b, k_col + pr)),
                  pl.BlockSpec((ctx_len, LANES), lambda b, pr, t: (b, v_col + pr)),
                  pl.BlockSpec((None, 2, qt, kw), lambda b, pr, t: (geom(t), pr, 0, 0))],
        out_specs=pl.BlockSpec((qt, LANES), lambda b, pr, t: (b * n_tiles + t, pr)),
        out_shape=jax.ShapeDtypeStruct((rows, n_pairs * LANES), F32),
        compiler_params=_cparams(("arbitrary", "arbitrary", "arbitrary")),
        name="na_attention",
    )(p, p, p, pc, pc, bias)


def _ctx_attn_kernel(q_ref, k_ref, v_ref, o_ref):
    q = q_ref[...] * (NA_DH ** -0.5)
    k = k_ref[...].astype(BF16)
    v = v_ref[...].astype(BF16)
    lane = lax.broadcasted_iota(jnp.int32, q.shape, 1)
    outs = []
    for hh in range(2):
        sel = (lane < NA_DH) if hh == 0 else (lane >= NA_DH)
        qh = jnp.where(sel, q, 0.0).astype(BF16)
        s = lax.dot_general(qh, k, (((1,), (1,)), ((), ())), preferred_element_type=F32)
        pm = jnp.exp(s - jnp.max(s, axis=1, keepdims=True))
        outs.append(jnp.dot(pm.astype(BF16), v, preferred_element_type=F32)
                    / jnp.sum(pm, axis=1, keepdims=True))
    o_ref[...] = jnp.where(lane < NA_DH, outs[0], outs[1])


def _ctx_attention(pc, batch, ctx_len, q_col, k_col, v_col):
    n_pairs = NA_HEADS // 2
    return pl.pallas_call(
        _ctx_attn_kernel,
        grid=(batch, n_pairs),
        in_specs=[pl.BlockSpec((ctx_len, LANES), lambda b, pr: (b, q_col + pr)),
                  pl.BlockSpec((ctx_len, LANES), lambda b, pr: (b, k_col + pr)),
                  pl.BlockSpec((ctx_len, LANES), lambda b, pr: (b, v_col + pr))],
        out_specs=pl.BlockSpec((ctx_len, LANES), lambda b, pr: (b, pr)),
        out_shape=jax.ShapeDtypeStruct((pc.shape[0], n_pairs * LANES), F32),
        compiler_params=_cparams(("arbitrary", "arbitrary")),
        name="ctx_attention",
    )(pc, pc, pc)


def _na_bias_tables(rpb, n_rows):
    n_tiles = n_rows // NA_QROWS
    n_roff = 2 * NA_WIN_R - 1
    n_coff = 2 * NA_WIN_W - 1
    col = np.arange(GRID_W)
    c0 = np.clip(col - NA_WIN_W // 2, 0, GRID_W - NA_WIN_W)
    col_in = (col[None, :] >= c0[:, None]) & (col[None, :] < c0[:, None] + NA_WIN_W)
    coff = np.clip(col[None, :] - col[:, None] + (NA_WIN_W - 1), 0, n_coff - 1)
    col_sel = (coff[:, :, None] == np.arange(n_coff)).astype(np.float32)
    row_sel = np.zeros((3, NA_QROWS, NA_KROWS, n_roff), np.float32)
    mask = np.zeros((3, NA_QROWS, GRID_W, NA_KROWS, GRID_W), np.float32)
    for g, t in enumerate((0, 1, n_tiles - 1)):
        rs = t * NA_QROWS
        ks = min(max(rs - NA_WIN_R // 2, 0), n_rows - NA_KROWS)
        qrow = rs + np.arange(NA_QROWS)
        krow = ks + np.arange(NA_KROWS)
        r0 = np.clip(qrow - NA_WIN_R // 2, 0, n_rows - NA_WIN_R)
        row_in = (krow[None, :] >= r0[:, None]) & (krow[None, :] < r0[:, None] + NA_WIN_R)
        roff = np.clip(krow[None, :] - qrow[:, None] + (NA_WIN_R - 1), 0, n_roff - 1)
        row_sel[g] = (roff[:, :, None] == np.arange(n_roff)) & row_in[:, :, None]
        ok = row_in[:, None, :, None] & col_in[None, :, None, :]
        mask[g] = np.where(ok, 0.0, -np.inf)
    hp = lax.Precision.HIGHEST
    by_col = jnp.einsum("hrc,qkc->hrqk", rpb, jnp.asarray(col_sel), precision=hp)
    tab = jnp.einsum("gair,hrqk->ghaqik", jnp.asarray(row_sel), by_col, precision=hp)
    tab = tab + jnp.asarray(mask)[:, None]
    return tab.reshape(3, rpb.shape[0], NA_QROWS * GRID_W, NA_KROWS * GRID_W)


def _merge_kernel(of_ref, ob_ref, z_ref, na_ref, gd_ref, gn_ref, x_ref, dnw_ref, wpa_ref, wpb_ref,
                  wout_ref, gpost_ref, g1_ref, o_ref, dn_scr):
    o = of_ref[...] + ob_ref[...]
    z = z_ref[...]
    for hh in range(DN_HEADS):
        sl = slice(hh * LANES, (hh + 1) * LANES)
        oh = o[:, sl]
        oh = oh * lax.rsqrt(jnp.mean(oh * oh, axis=-1, keepdims=True) + EPS) * dnw_ref[...]
        zh = z[:, sl]
        dn_scr[:, sl] = (oh * (zh * jax.nn.sigmoid(zh))).astype(BF16)
    y = (jax.nn.sigmoid(gd_ref[...]) * jnp.dot(dn_scr[...], wpa_ref[...], preferred_element_type=F32)
         + jax.nn.sigmoid(gn_ref[...]) * _bdot(na_ref[...], wpb_ref[...]))
    out = _bdot(y, wout_ref[...])
    o_ref[...] = x_ref[...] + g1_ref[...] * _rms(out, gpost_ref[...])


def _merge(o_f, o_b, p, na_o, x2, dn_norm, w_pa, w_pb, w_out, gpost, mod, mod_row, g1_blk, tm):
    rows, d = x2.shape
    nw = na_o.shape[1]
    row_blk = lambda c: pl.BlockSpec((tm, d), lambda i: (i, c))
    const = lambda shape: pl.BlockSpec(shape, lambda i: (0,) * len(shape))
    return pl.pallas_call(
        _merge_kernel,
        grid=(rows // tm,),
        in_specs=[row_blk(0), row_blk(0), row_blk(3), pl.BlockSpec((tm, nw), lambda i: (i, 0)),
                  row_blk(4), row_blk(5), row_blk(0),
                  const((1, LANES)), const((d, d)), const((nw, d)), const((d, d)), const((1, d)),
                  pl.BlockSpec((None, 1, d), lambda i: (mod_row(i), 0, g1_blk))],
        out_specs=row_blk(0),
        out_shape=jax.ShapeDtypeStruct((rows, d), F32),
        scratch_shapes=[pltpu.VMEM((tm, d), BF16)],
        compiler_params=_cparams(("arbitrary",)),
        name="merge",
    )(o_f, o_b, p, na_o, p, p, x2, dn_norm, w_pa, w_pb, w_out, gpost, mod)


def _ffn_kernel(n_f, x_ref, gpre_ref, sc_ref, sh_ref, w1_ref, w3_ref, w2_ref, gpost_ref, g2_ref,
                o_ref, h_scr, acc_scr):
    j = pl.program_id(1)

    @pl.when(j == 0)
    def _():
        h = _rms(x_ref[...], gpre_ref[...]) * (1.0 + sc_ref[...]) + sh_ref[...]
        h_scr[...] = h.astype(BF16)
        acc_scr[...] = jnp.zeros_like(acc_scr)

    h = h_scr[...]
    a = jnp.dot(h, w1_ref[...], preferred_element_type=F32)
    b = jnp.dot(h, w3_ref[...], preferred_element_type=F32)
    acc_scr[...] += _bdot(a * jax.nn.sigmoid(a) * b, w2_ref[...])

    @pl.when(j == n_f - 1)
    def _():
        o_ref[...] = x_ref[...] + g2_ref[...] * _rms(acc_scr[...], gpost_ref[...])


def _dense_ffn(x2, gpre, gpost, mod, mod_row, w1, w3, w2, tm, tf):
    rows, d = x2.shape
    f = w1.shape[1]
    n_f = f // tf
    modspec = lambda blk: pl.BlockSpec((None, 1, d), lambda i, j: (mod_row(i), 0, blk))
    return pl.pallas_call(
        functools.partial(_ffn_kernel, n_f),
        grid=(rows // tm, n_f),
        in_specs=[pl.BlockSpec((tm, d), lambda i, j: (i, 0)),
                  pl.BlockSpec((1, d), lambda i, j: (0, 0)),
                  modspec(4), modspec(3),
                  pl.BlockSpec((d, tf), lambda i, j: (0, j)),
                  pl.BlockSpec((d, tf), lambda i, j: (0, j)),
                  pl.BlockSpec((tf, d), lambda i, j: (j, 0)),
                  pl.BlockSpec((1, d), lambda i, j: (0, 0)),
                  modspec(5)],
        out_specs=pl.BlockSpec((tm, d), lambda i, j: (i, 0)),
        out_shape=jax.ShapeDtypeStruct((rows, d), F32),
        scratch_shapes=[pltpu.VMEM((tm, d), BF16), pltpu.VMEM((tm, d), F32)],
        compiler_params=_cparams(("arbitrary", "arbitrary")),
        name="dense_ffn",
    )(x2, gpre, mod, mod, w1, w3, w2, gpost, mod)


def _router_kernel(x_ref, gpre_ref, sc_ref, sh_ref, r_ref, h_ref, gate_ref):
    h = _rms(x_ref[...], gpre_ref[...]) * (1.0 + sc_ref[...]) + sh_ref[...]
    h_ref[...] = h.astype(BF16)
    logits = jnp.dot(h, r_ref[...], precision=lax.Precision.HIGHEST, preferred_element_type=F32)
    lane = lax.broadcasted_iota(jnp.int32, logits.shape, 1)
    neg = -jnp.inf
    l1 = jnp.where(lane < N_EXPERTS, logits, neg)
    m1 = jnp.max(l1, axis=1, keepdims=True)
    i1 = jnp.min(jnp.where(l1 == m1, lane, LANES), axis=1, keepdims=True)
    l2 = jnp.where(lane == i1, neg, l1)
    m2 = jnp.max(l2, axis=1, keepdims=True)
    i2 = jnp.min(jnp.where(l2 == m2, lane, LANES), axis=1, keepdims=True)
    e = jnp.exp(m2 - m1)
    w1 = 1.0 / (1.0 + e)
    w2 = e / (1.0 + e)
    out = jnp.where(lane == 0, i1.astype(F32), 0.0)
    out = jnp.where(lane == 1, i2.astype(F32), out)
    out = jnp.where(lane == 2, w1, out)
    out = jnp.where(lane == 3, w2, out)
    gate_ref[...] = out


def _router(x2, gpre, mod, mod_row, router_pad, tm):
    rows, d = x2.shape
    modspec = lambda blk: pl.BlockSpec((None, 1, d), lambda i: (mod_row(i), 0, blk))
    return pl.pallas_call(
        _router_kernel,
        grid=(rows // tm,),
        in_specs=[pl.BlockSpec((tm, d), lambda i: (i, 0)),
                  pl.BlockSpec((1, d), lambda i: (0, 0)),
                  modspec(4), modspec(3),
                  pl.BlockSpec((d, LANES), lambda i: (0, 0))],
        out_specs=[pl.BlockSpec((tm, d), lambda i: (i, 0)),
                   pl.BlockSpec((tm, LANES), lambda i: (i, 0))],
        out_shape=[jax.ShapeDtypeStruct((rows, d), BF16),
                   jax.ShapeDtypeStruct((rows, LANES), F32)],
        compiler_params=_cparams(("arbitrary",)),
        name="moe_router",
    )(x2, gpre, mod, mod, router_pad)


def _expert_kernel(n_f, te_ref, nv_ref, x_ref, w1_ref, w3_ref, w2_ref, o_ref, acc_scr):
    i = pl.program_id(0)
    j = pl.program_id(1)
    valid = i < nv_ref[0]

    @pl.when(j == 0)
    def _():
        acc_scr[...] = jnp.zeros_like(acc_scr)

    @pl.when(valid)
    def _():
        x = x_ref[...]
        a = jnp.dot(x, w1_ref[...], preferred_element_type=F32)
        b = jnp.dot(x, w3_ref[...], preferred_element_type=F32)
        acc_scr[...] += _bdot(a * jax.nn.sigmoid(a) * b, w2_ref[...])

    @pl.when(j == n_f - 1)
    def _():
        o_ref[...] = acc_scr[...]


def _expert_ffn(xs, tile_expert, n_valid, w1, w3, w2, tf):
    prow, d = xs.shape
    f = w1.shape[2]
    n_f = f // tf
    n_tiles = prow // MOE_TM
    grid_spec = pltpu.PrefetchScalarGridSpec(
        num_scalar_prefetch=2,
        grid=(n_tiles, n_f),
        in_specs=[pl.BlockSpec((MOE_TM, d), lambda i, j, te, nv: (i, 0)),
                  pl.BlockSpec((None, d, tf), lambda i, j, te, nv: (te[i], 0, j)),
                  pl.BlockSpec((None, d, tf), lambda i, j, te, nv: (te[i], 0, j)),
                  pl.BlockSpec((None, tf, d), lambda i, j, te, nv: (te[i], j, 0))],
        out_specs=pl.BlockSpec((MOE_TM, d), lambda i, j, te, nv: (i, 0)),
        scratch_shapes=[pltpu.VMEM((MOE_TM, d), F32)],
    )
    return pl.pallas_call(
        functools.partial(_expert_kernel, n_f),
        grid_spec=grid_spec,
        out_shape=jax.ShapeDtypeStruct((prow, d), F32),
        compiler_params=_cparams(("arbitrary", "arbitrary")),
        name="moe_experts",
    )(tile_expert, n_valid, xs, w1, w3, w2)


def _combine_kernel(y1_ref, y2_ref, gate_ref, x_ref, gpost_ref, g2_ref, o_ref):
    gt = gate_ref[...]
    lane = lax.broadcasted_iota(jnp.int32, gt.shape, 1)
    w1 = jnp.sum(jnp.where(lane == 2, gt, 0.0), axis=1, keepdims=True)
    w2 = jnp.sum(jnp.where(lane == 3, gt, 0.0), axis=1, keepdims=True)
    y = w1 * y1_ref[...] + w2 * y2_ref[...]
    o_ref[...] = x_ref[...] + g2_ref[...] * _rms(y, gpost_ref[...])


def _moe_combine(y1, y2, gates, x2, gpost, mod, mod_row, tm):
    rows, d = x2.shape
    rb = pl.BlockSpec((tm, d), lambda i: (i, 0))
    return pl.pallas_call(
        _combine_kernel,
        grid=(rows // tm,),
        in_specs=[rb, rb, pl.BlockSpec((tm, LANES), lambda i: (i, 0)), rb,
                  pl.BlockSpec((1, d), lambda i: (0, 0)),
                  pl.BlockSpec((None, 1, d), lambda i: (mod_row(i), 0, 5))],
        out_specs=rb,
        out_shape=jax.ShapeDtypeStruct((rows, d), F32),
        compiler_params=_cparams(("arbitrary",)),
        name="moe_combine",
    )(y1, y2, gates, x2, gpost, mod)


def _moe_schedule(gates, n_tokens):
    idx = gates[:, 0:2].astype(jnp.int32)
    flat_e = idx.reshape(-1)
    onehot = (flat_e[:, None] == jnp.arange(N_EXPERTS)[None, :]).astype(jnp.int32)
    csum = jnp.cumsum(onehot, axis=0)
    counts = csum[-1]
    rank = jnp.sum(csum * onehot, axis=1) - 1
    padded = ((counts + MOE_TM - 1) // MOE_TM) * MOE_TM
    ends = jnp.cumsum(padded)
    starts = ends - padded
    dest = jnp.sum(starts[None, :] * onehot, axis=1) + rank
    n_rows = 2 * n_tokens + N_EXPERTS * MOE_TM
    n_tiles = n_rows // MOE_TM
    row_token = jnp.zeros((n_rows,), jnp.int32).at[dest].set(jnp.arange(2 * n_tokens, dtype=jnp.int32) // 2)
    tile_start = jnp.arange(n_tiles, dtype=jnp.int32) * MOE_TM
    tile_expert = jnp.minimum(jnp.sum((ends[None, :] <= tile_start[:, None]).astype(jnp.int32), axis=1),
                              N_EXPERTS - 1)
    n_valid = (ends[-1] // MOE_TM).astype(jnp.int32).reshape(1)
    return row_token, tile_expert, n_valid, dest.reshape(n_tokens, 2)


def _rope_tables(seq):
    t = jnp.arange(seq)
    row = (t // GRID_W).astype(F32)
    col = (t % GRID_W).astype(F32)
    n_freq = DN_DK // 4
    inv = ROPE_BASE ** (-jnp.arange(n_freq, dtype=F32) / n_freq)
    ang = jnp.concatenate([row[:, None] * inv, col[:, None] * inv], axis=-1)
    cos, sin = jnp.cos(ang), jnp.sin(ang)
    return jnp.concatenate([cos, cos], axis=-1), jnp.concatenate([-sin, sin], axis=-1)


def kernel(x, c, ctx, c_ctx, ada_w, ada_b, norm_mix_pre, norm_mix_post, norm_ffn_pre, norm_ffn_post,
           w_in, dn_conv, dn_a_log, dn_dt_bias, dn_norm, na_rpb, w_branch_dn, w_branch_na, w_out,
           ffn_w1, ffn_w3, ffn_w2, moe_router, moe_w1, moe_w3, moe_w2):
    batch, seq, d = x.shape
    ctx_len = ctx.shape[1]
    depth = w_in.shape[0]
    nh = DN_HEADS
    dn_w = nh * DN_DK
    na_w = NA_HEADS * NA_DH
    n_rows = seq // GRID_W
    assert d == dn_w and seq % SCAN_TILE == 0 and ctx_len % SCAN_TILE == 0 and n_rows % NA_QROWS == 0
    assert depth <= 2, "context tokens only take the dense FFN path"

    c_rows = jnp.zeros((SUBLANES, d), F32).at[:batch].set(c).at[batch].set(c_ctx)
    mod_all = _mod_vectors(c_rows, ada_w, ada_b)
    cos2, sin2 = _rope_tables(seq)
    ones_t = jnp.ones((SCAN_TILE, LANES), F32)

    x2 = x.reshape(batch * seq, d)
    xc2 = ctx.reshape(batch * ctx_len, d)
    lat_tm = 1024
    lat_row = lambda tm: (lambda i: i // (seq // tm))
    ctx_row = lambda i: batch

    q_col, k_col, v_col = (4 * dn_w + 2 * d) // LANES, (4 * dn_w + 2 * d + na_w) // LANES, \
        (4 * dn_w + 2 * d + 2 * na_w) // LANES

    for l in range(depth):
        last = l == depth - 1
        mod = mod_all[l].reshape(SUBLANES, 1, 6 * d)
        wl = w_in[l]
        o_ab = 4 * dn_w
        o_na = o_ab + 4 * nh
        o_gate = o_na + 3 * na_w
        w_main = jnp.concatenate([wl[:, :o_ab], wl[:, o_gate:], wl[:, o_na:o_gate]], axis=1).astype(BF16)
        wab = jnp.pad(wl[:, o_ab:o_na], ((0, 0), (0, LANES - 4 * nh)))
        wab_hi = wab.astype(BF16)
        wab_lo = (wab - wab_hi.astype(F32)).astype(BF16)
        gpre = norm_mix_pre[l].reshape(1, d)
        gpost = norm_mix_post[l].reshape(1, d)

        p, ab = _in_proj(x2, gpre, mod, lat_row(lat_tm), 1, 0, w_main, wab_hi, wab_lo, lat_tm, 512)
        pc, abc = _in_proj(xc2, gpre, mod, ctx_row, 1, 0, w_main, wab_hi, wab_lo, batch * ctx_len, 512)

        conv_w3 = jnp.pad(dn_conv[l].T.reshape(DN_CONV, 3, dn_w).transpose(1, 0, 2),
                          ((0, 0), (0, SUBLANES - DN_CONV), (0, 0)))
        gpar = jnp.zeros((SUBLANES, LANES), F32)
        gpar = gpar.at[0, :2 * nh].set(-jnp.exp(dn_a_log[l].reshape(-1)))
        gpar = gpar.at[1, :2 * nh].set(dn_dt_bias[l].reshape(-1))

        qc_, kc_, vc_, gbc, gbtc = _dn_prep(pc, abc, conv_w3, gpar, ones_t, ones_t, batch, ctx_len, False)
        ql_, kl_, vl_, gbl, gbtl = _dn_prep(p, ab, conv_w3, gpar, cos2, sin2, batch, seq, True)
        s0 = jnp.zeros((batch, nh, 2, LANES, LANES), F32)
        oc_f, oc_b, s_ctx = _dn_scan(qc_, kc_, vc_, gbc, gbtc, s0, batch, ctx_len)
        ol_f, ol_b, _ = _dn_scan(ql_, kl_, vl_, gbl, gbtl, s_ctx, batch, seq)

        bias = _na_bias_tables(na_rpb[l], n_rows)
        na_lat = _na_attention(p, pc, bias, batch, seq, ctx_len, q_col, k_col, v_col)

        dnw = dn_norm[l].reshape(1, LANES)
        w_pa = w_branch_dn[l].astype(BF16)
        w_pb = w_branch_na[l].astype(BF16)
        w_o = w_out[l].astype(BF16)
        x2 = _merge(ol_f, ol_b, p, na_lat, x2, dnw, w_pa, w_pb, w_o, gpost, mod, lat_row(256), 2, 256)

        gfpre = norm_ffn_pre[l].reshape(1, d)
        gfpost = norm_ffn_post[l].reshape(1, d)
        if l % 2 == 0:
            w1 = ffn_w1[l // 2].astype(BF16)
            w3 = ffn_w3[l // 2].astype(BF16)
            w2 = ffn_w2[l // 2].astype(BF16)
            tf = w1.shape[1] // 2
            x2 = _dense_ffn(x2, gfpre, gfpost, mod, lat_row(512), w1, w3, w2, 512, tf)
        else:
            rpad = jnp.pad(moe_router[l // 2], ((0, 0), (0, LANES - N_EXPERTS)))
            hb, gates = _router(x2, gfpre, mod, lat_row(512), rpad, 512)
            n_tok = batch * seq
            row_token, tile_expert, n_valid, dest = _moe_schedule(gates, n_tok)
            xs = jnp.take(hb, row_token, axis=0)
            ys = _expert_ffn(xs, tile_expert, n_valid, moe_w1[l // 2].astype(BF16),
                             moe_w3[l // 2].astype(BF16), moe_w2[l // 2].astype(BF16), 512)
            y1 = jnp.take(ys, dest[:, 0], axis=0)
            y2 = jnp.take(ys, dest[:, 1], axis=0)
            x2 = _moe_combine(y1, y2, gates, x2, gfpost, mod, lat_row(512), 512)

        if not last:
            na_ctx = _ctx_attention(pc, batch, ctx_len, q_col, k_col, v_col)
            xc2 = _merge(oc_f, oc_b, pc, na_ctx, xc2, dnw, w_pa, w_pb, w_o, gpost, mod, ctx_row, 2, 256)
            xc2 = _dense_ffn(xc2, gfpre, gfpost, mod, ctx_row, w1, w3, w2, 512, tf)
    return x2.reshape(batch, seq, d)
```

```python
import functools

import numpy as np
import jax
import jax.numpy as jnp
from jax import lax
from jax.experimental import pallas as pl
from jax.experimental.pallas import tpu as pltpu

F32 = jnp.float32
BF16 = jnp.bfloat16

GRID_W = 64
DN_HEADS = 8
DN_DK = 128
DN_CONV = 5
DN_CHUNK = 64
NA_HEADS = 8
NA_DH = 64
NA_WIN_R = 8
NA_WIN_W = 16
ROPE_BASE = 10000.0
N_EXPERTS = 8
EPS = 1e-6

LANES = 128
SUBLANES = 8
BF16_SUBLANES = 16
VMEM_LIMIT = 56 * 1024 * 1024

SCAN_TILE = 256
SCAN_HEADS = 4
NA_QROWS = 4
NA_KROWS = NA_QROWS + 8
MOE_TM = 512


def _cparams(sem):
    return pltpu.CompilerParams(dimension_semantics=sem, vmem_limit_bytes=VMEM_LIMIT)


def _bdot(a, b):
    return jnp.dot(a.astype(BF16), b.astype(BF16), preferred_element_type=F32)


def _dot_nt(a, b):
    return lax.dot_general(a.astype(BF16), b.astype(BF16), (((1,), (1,)), ((), ())),
                           preferred_element_type=F32)


def _dot_tn(a, b):
    return lax.dot_general(a.astype(BF16), b.astype(BF16), (((0,), (0,)), ((), ())),
                           preferred_element_type=F32)


def _split3(x):
    hi = x.astype(BF16)
    r = x - hi.astype(F32)
    mid = r.astype(BF16)
    lo = (r - mid.astype(F32)).astype(BF16)
    return hi, mid, lo


def _rms(x, gain):
    return x * lax.rsqrt(jnp.mean(x * x, axis=-1, keepdims=True) + EPS) * gain


def _mod_kernel(c_ref, w_ref, b_ref, o_ref):
    c = c_ref[...]
    s = c * jax.nn.sigmoid(c)
    o_ref[0] = jnp.dot(s, w_ref[0], precision=lax.Precision.HIGHEST,
                       preferred_element_type=F32) + b_ref[0]


def _mod_vectors(c_rows, ada_w, ada_b):
    depth, d, n = ada_w.shape
    tn = 1536
    return pl.pallas_call(
        _mod_kernel,
        grid=(depth, n // tn),
        in_specs=[pl.BlockSpec((SUBLANES, d), lambda l, j: (0, 0)),
                  pl.BlockSpec((1, d, tn), lambda l, j: (l, 0, j)),
                  pl.BlockSpec((1, 1, tn), lambda l, j: (l, 0, j))],
        out_specs=pl.BlockSpec((1, SUBLANES, tn), lambda l, j: (l, 0, j)),
        out_shape=jax.ShapeDtypeStruct((depth, SUBLANES, n), F32),
        compiler_params=_cparams(("arbitrary", "arbitrary")),
        name="mod_vectors",
    )(c_rows, ada_w, ada_b.reshape(depth, 1, n))


def _inproj_kernel(x_ref, g_ref, sc_ref, sh_ref, w_ref, wab_hi_ref, wab_lo_ref,
                   o_ref, ab_ref, h_scr, hlo_scr):
    j = pl.program_id(1)

    @pl.when(j == 0)
    def _():
        h = _rms(x_ref[...], g_ref[...]) * (1.0 + sc_ref[...]) + sh_ref[...]
        hi = h.astype(BF16)
        lo = (h - hi.astype(F32)).astype(BF16)
        h_scr[...] = hi
        hlo_scr[...] = lo
        ab_ref[...] = (jnp.dot(hi, wab_hi_ref[...], preferred_element_type=F32)
                       + jnp.dot(lo, wab_hi_ref[...], preferred_element_type=F32)
                       + jnp.dot(hi, wab_lo_ref[...], preferred_element_type=F32))

    o_ref[...] = jnp.dot(h_scr[...], w_ref[...], preferred_element_type=F32).astype(BF16)


def _in_proj(x2, gain, mod, mod_row, sc_blk, sh_blk, w_main, wab_hi, wab_lo, tm, tn):
    rows, d = x2.shape
    n = w_main.shape[1]
    return pl.pallas_call(
        _inproj_kernel,
        grid=(rows // tm, n // tn),
        in_specs=[pl.BlockSpec((tm, d), lambda i, j: (i, 0)),
                  pl.BlockSpec((1, d), lambda i, j: (0, 0)),
                  pl.BlockSpec((None, 1, d), lambda i, j: (mod_row(i), 0, sc_blk)),
                  pl.BlockSpec((None, 1, d), lambda i, j: (mod_row(i), 0, sh_blk)),
                  pl.BlockSpec((d, tn), lambda i, j: (0, j)),
                  pl.BlockSpec((d, LANES), lambda i, j: (0, 0)),
                  pl.BlockSpec((d, LANES), lambda i, j: (0, 0))],
        out_specs=[pl.BlockSpec((tm, tn), lambda i, j: (i, j)),
                   pl.BlockSpec((tm, LANES), lambda i, j: (i, 0))],
        out_shape=[jax.ShapeDtypeStruct((rows, n), BF16),
                   jax.ShapeDtypeStruct((rows, LANES), F32)],
        scratch_shapes=[pltpu.VMEM((tm, d), BF16), pltpu.VMEM((tm, d), BF16)],
        compiler_params=_cparams(("arbitrary", "arbitrary")),
        name="in_proj",
    )(x2, gain, mod, mod, w_main, wab_hi, wab_lo)


def _prep_kernel(rope, n_tiles,
                 q_ref, qp_ref, qn_ref, k_ref, kp_ref, kn_ref, v_ref, vp_ref, vn_ref,
                 cw_ref, ab_ref, gpar_ref, cos_ref, sin_ref,
                 qo_ref, ko_ref, vo_ref, gb_ref, gbt_ref, xq_scr, xk_scr, xv_scr):
    t = pl.program_id(1)
    tt = q_ref.shape[0]
    first = t == 0
    last = t == n_tiles - 1
    pad = DN_CONV // 2
    halo = qp_ref.shape[0]

    for scr, m_ref, p_ref, n_ref in ((xq_scr, q_ref, qp_ref, qn_ref), (xk_scr, k_ref, kp_ref, kn_ref),
                                     (xv_scr, v_ref, vp_ref, vn_ref)):
        scr[0:halo, :] = jnp.where(first, 0.0, p_ref[...].astype(F32))
        scr[halo:halo + tt, :] = m_ref[...].astype(F32)
        scr[halo + tt:, :] = jnp.where(last, 0.0, n_ref[...].astype(F32))

    def conv_silu(scr, w, sl):
        acc = scr[halo - pad:halo - pad + tt, sl] * w[0:1]
        for i in range(1, DN_CONV):
            o = halo - pad + i
            acc = acc + scr[o:o + tt, sl] * w[i:i + 1]
        return acc * jax.nn.sigmoid(acc)

    def l2n(x):
        return x * lax.rsqrt(jnp.sum(x * x, axis=-1, keepdims=True) + EPS)

    def rot(x):
        if not rope:
            return x
        return x * cos_ref[...] + pltpu.roll(x, LANES // 2, 1) * sin_ref[...]

    for hh in range(DN_HEADS):
        sl = slice(hh * LANES, (hh + 1) * LANES)
        qo_ref[:, sl] = (rot(l2n(conv_silu(xq_scr, cw_ref[0, :, sl], sl))) * (DN_DK ** -0.5)).astype(BF16)
        ko_ref[:, sl] = rot(l2n(conv_silu(xk_scr, cw_ref[1, :, sl], sl))).astype(BF16)
        vo_ref[:, sl] = conv_silu(xv_scr, cw_ref[2, :, sl], sl).astype(BF16)

    ab = ab_ref[...]
    lane = lax.broadcasted_iota(jnp.int32, ab.shape, 1)
    row = lax.broadcasted_iota(jnp.int32, ab.shape, 0) % DN_CHUNK
    xg = ab + gpar_ref[1:2]
    sp = jnp.maximum(xg, 0.0) + jnp.log1p(jnp.exp(-jnp.abs(xg)))
    g = gpar_ref[0:1] * sp
    beta = jax.nn.sigmoid(ab)
    pre = g
    suf = g
    s = 1
    while s < DN_CHUNK:
        pre = pre + jnp.where(row >= s, pltpu.roll(pre, s, 0), 0.0)
        suf = suf + jnp.where(row < DN_CHUNK - s, pltpu.roll(suf, tt - s, 0), 0.0)
        s *= 2
    nh = DN_HEADS
    gb = jnp.where(lane < nh, pre, jnp.where(lane < 2 * nh, suf, beta))
    gb_ref[...] = gb
    er = lax.broadcasted_iota(jnp.int32, (LANES, 3 * LANES), 0)
    ec = lax.broadcasted_iota(jnp.int32, (LANES, 3 * LANES), 1)
    eye3 = ((ec % LANES) == er).astype(BF16)
    gbt_ref[...] = lax.dot_general(eye3, jnp.concatenate(_split3(gb), axis=1),
                                   (((1,), (1,)), ((), ())), preferred_element_type=F32)


def _dn_prep(p, ab, conv_w3, gpar, cos2, sin2, batch, seq, rope):
    rows = p.shape[0]
    tt = SCAN_TILE
    n_tiles = seq // tt
    halo = BF16_SUBLANES
    hb = tt // halo
    n_hblk = rows // halo
    d = DN_HEADS * LANES

    def main(cb):
        return pl.BlockSpec((tt, d), lambda b, t: (b * n_tiles + t, cb))

    def prev(cb):
        return pl.BlockSpec((halo, d), lambda b, t: (jnp.maximum((b * n_tiles + t) * hb - 1, 0), cb))

    def nxt(cb):
        return pl.BlockSpec((halo, d),
                            lambda b, t: (jnp.minimum((b * n_tiles + t + 1) * hb, n_hblk - 1), cb))

    in_specs = []
    for cb in range(3):
        in_specs += [main(cb), prev(cb), nxt(cb)]
    in_specs += [
        pl.BlockSpec((3, SUBLANES, d), lambda b, t: (0, 0, 0)),
        pl.BlockSpec((tt, LANES), lambda b, t: (b * n_tiles + t, 0)),
        pl.BlockSpec((SUBLANES, LANES), lambda b, t: (0, 0)),
        pl.BlockSpec((tt, LANES), lambda b, t: (t, 0)),
        pl.BlockSpec((tt, LANES), lambda b, t: (t, 0)),
    ]
    out_full = pl.BlockSpec((tt, d), lambda b, t: (b * n_tiles + t, 0))
    return pl.pallas_call(
        functools.partial(_prep_kernel, rope, n_tiles),
        grid=(batch, n_tiles),
        in_specs=in_specs,
        out_specs=[out_full, out_full, out_full,
                   pl.BlockSpec((tt, LANES), lambda b, t: (b * n_tiles + t, 0)),
                   pl.BlockSpec((LANES, tt), lambda b, t: (0, b * n_tiles + t))],
        out_shape=[jax.ShapeDtypeStruct((rows, d), BF16)] * 3
        + [jax.ShapeDtypeStruct((rows, LANES), F32), jax.ShapeDtypeStruct((LANES, rows), F32)],
        scratch_shapes=[pltpu.VMEM((tt + 2 * halo, d), F32)] * 3,
        compiler_params=_cparams(("arbitrary", "arbitrary")),
        name="dn_prep_rope" if rope else "dn_prep",
    )(p, p, p, p, p, p, p, p, p, conv_w3, ab, gpar, cos2, sin2)


def _scan_kernel(n_steps,
                 qf_ref, kf_ref, vf_ref, gf_ref, gtf_ref, qb_ref, kb_ref, vb_ref, gb_ref, gtb_ref, s0_ref,
                 of_ref, ob_ref, sfin_ref, s_scr):
    hg = pl.program_id(1)
    step = pl.program_id(2)
    c = DN_CHUNK
    n_chunks = qf_ref.shape[0] // c
    refs = ((qf_ref, kf_ref, vf_ref, gf_ref, gtf_ref, of_ref),
            (qb_ref, kb_ref, vb_ref, gb_ref, gtb_ref, ob_ref))

    @pl.when(step == 0)
    def _():
        s_scr[...] = s0_ref[...]

    ri = lax.broadcasted_iota(jnp.int32, (c, c), 0)
    ci = lax.broadcasted_iota(jnp.int32, (c, c), 1)
    lane = lax.broadcasted_iota(jnp.int32, (c, LANES), 1)
    eye = (ri == ci).astype(F32)
    incl = (ri >= ci, ri <= ci)
    strict = (ri > ci, ri < ci)

    def pick(tile, idx):
        return jnp.sum(jnp.where(lane == idx, tile, 0.0), axis=1, keepdims=True)

    sub = lax.broadcasted_iota(jnp.int32, (DN_HEADS, qf_ref.shape[0]), 0)
    gc_rows = {(hh, dr): jnp.sum(jnp.where(sub == hg * SCAN_HEADS + hh,
                                           refs[dr][4][dr * DN_HEADS:(dr + 1) * DN_HEADS, :], 0.0),
                                 axis=0, keepdims=True)
               for hh in range(SCAN_HEADS) for dr in range(2)}

    items = [(hh, dr, cc) for hh in range(SCAN_HEADS) for dr in range(2) for cc in range(n_chunks)]
    st = []
    for hh, dr, cc in items:
        q_ref, k_ref, v_ref, g_ref, gt_ref, _ = refs[dr]
        rs = slice(cc * c, (cc + 1) * c)
        ls = slice(hh * LANES, (hh + 1) * LANES)
        head = hg * SCAN_HEADS + hh
        gtile = g_ref[rs, :]
        gc = pick(gtile, dr * DN_HEADS + head)
        beta = pick(gtile, (2 + dr) * DN_HEADS + head)
        gc_row = gc_rows[(hh, dr)][:, rs]
        edge = c - 1 if dr == 0 else 0
        g_last = gc_row[:, edge:edge + 1]
        q = q_ref[rs, ls].astype(F32)
        k = k_ref[rs, ls].astype(F32)
        v = v_ref[rs, ls].astype(F32)
        dec = jnp.where(incl[dr], jnp.exp(jnp.where(incl[dr], gc - gc_row, 0.0)), 0.0)
        egc = jnp.exp(gc)
        kbeta = k * beta
        st.append(dict(dr=dr, rs=rs, ls=ls, q=q, k=k, dec=dec, kbeta=kbeta,
                       rhs=jnp.concatenate([v * beta, kbeta * egc], axis=1).astype(BF16),
                       k_dec=(k * jnp.exp(g_last - gc)).astype(BF16),
                       q_dec=(q * egc).astype(BF16),
                       e_last=jnp.exp(g_last)))
    for s in st:
        s["tm"] = jnp.where(strict[s["dr"]], _dot_nt(s["kbeta"], s["k"]) * s["dec"], 0.0)
    for s in st:
        s["attn"] = (_dot_nt(s["q"], s["k"]) * s["dec"]).astype(BF16)

    m8 = (ri // 8) == (ci // 8)
    pw = [-jnp.where(m8, s["tm"], 0.0) for s in st]
    p2 = [_bdot(p, p) for p in pw]
    p4 = [_bdot(p, p) for p in p2]
    xs = [eye + p for p in pw]
    xs = [x + _bdot(p, x) for x, p in zip(xs, p2)]
    xs = [x + _bdot(p, x) for x, p in zip(xs, p4)]
    blk = 8
    while blk < c:
        off = ((ri // (2 * blk)) == (ci // (2 * blk))) & ((ri // blk) != (ci // blk))
        lx = [_bdot(jnp.where(off, s["tm"], 0.0), x) for s, x in zip(st, xs)]
        xs = [x - _bdot(x, y) for x, y in zip(xs, lx)]
        blk *= 2
    uw = [_bdot(x, s["rhs"]) for s, x in zip(st, xs)]

    by_key = {it: (s, y) for it, s, y in zip(items, st, uw)}
    chains = [(hh, dr) for hh in range(SCAN_HEADS) for dr in range(2)]
    state = {ch: s_scr[ch[0], ch[1]] for ch in chains}
    for i in range(n_chunks):
        cur = {ch: by_key[(ch[0], ch[1], i if ch[1] == 0 else n_chunks - 1 - i)] for ch in chains}
        wsqs = {ch: _bdot(jnp.concatenate([cur[ch][1][:, LANES:].astype(BF16), cur[ch][0]["q_dec"]], axis=0),
                          state[ch]) for ch in chains}
        v_new = {ch: cur[ch][1][:, :LANES] - wsqs[ch][:c] for ch in chains}
        for ch in chains:
            s = cur[ch][0]
            refs[ch[1]][5][s["rs"], s["ls"]] = wsqs[ch][c:] + _bdot(s["attn"], v_new[ch])
        state = {ch: state[ch] * cur[ch][0]["e_last"] + _dot_tn(cur[ch][0]["k_dec"], v_new[ch])
                 for ch in chains}
    for ch in chains:
        s_scr[ch[0], ch[1]] = state[ch]

    @pl.when(step == n_steps - 1)
    def _():
        sfin_ref[...] = s_scr[...]


def _dn_scan(qn, kn, vv, gb, gbt, s0, batch, seq):
    rows, d = qn.shape
    tt = SCAN_TILE
    n_steps = seq // tt
    n_groups = DN_HEADS // SCAN_HEADS
    w = SCAN_HEADS * LANES
    fwd_t = lambda b, s: b * n_steps + s
    bwd_t = lambda b, s: b * n_steps + n_steps - 1 - s

    def specs(tile):
        wide = pl.BlockSpec((tt, w), lambda b, g, s: (tile(b, s), g))
        return wide, [wide, wide, wide,
                      pl.BlockSpec((tt, LANES), lambda b, g, s: (tile(b, s), 0)),
                      pl.BlockSpec((LANES, tt), lambda b, g, s: (0, tile(b, s)))]

    out_f, in_f = specs(fwd_t)
    out_b, in_b = specs(bwd_t)
    st_spec = pl.BlockSpec((None, SCAN_HEADS, 2, LANES, LANES), lambda b, g, s: (b, g, 0, 0, 0))
    return pl.pallas_call(
        functools.partial(_scan_kernel, n_steps),
        grid=(batch, n_groups, n_steps),
        in_specs=in_f + in_b + [st_spec],
        out_specs=[out_f, out_b, st_spec],
        out_shape=[jax.ShapeDtypeStruct((rows, d), F32), jax.ShapeDtypeStruct((rows, d), F32),
                   jax.ShapeDtypeStruct((batch, DN_HEADS, 2, LANES, LANES), F32)],
        scratch_shapes=[pltpu.VMEM((SCAN_HEADS, 2, LANES, LANES), F32)],
        compiler_params=_cparams(("arbitrary", "arbitrary", "arbitrary")),
        name="dn_scan",
    )(qn, kn, vv, gb, gbt, qn, kn, vv, gb, gbt, s0)


def _na_kernel(n_rows, q_ref, k_ref, v_ref, kc_ref, vc_ref, bias_ref, o_ref):
    t = pl.program_id(2)
    kw = NA_KROWS * GRID_W
    ks = jnp.clip(t * NA_QROWS - NA_WIN_R // 2, 0, n_rows - NA_KROWS)
    start = pl.multiple_of(ks * GRID_W, GRID_W)
    q = q_ref[...] * (NA_DH ** -0.5)
    kwin = k_ref[pl.ds(start, kw), :]
    vwin = v_ref[pl.ds(start, kw), :]
    kc = kc_ref[...]
    vc = vc_ref[...]
    lane = lax.broadcasted_iota(jnp.int32, q.shape, 1)
    outs = []
    for hh in range(2):
        sel = (lane < NA_DH) if hh == 0 else (lane >= NA_DH)
        qh = jnp.where(sel, q, jnp.zeros_like(q))
        s_loc = lax.dot_general(qh, kwin, (((1,), (1,)), ((), ())),
                                preferred_element_type=F32) + bias_ref[hh]
        s_ctx = lax.dot_general(qh, kc, (((1,), (1,)), ((), ())), preferred_element_type=F32)
        m = jnp.maximum(jnp.max(s_loc, axis=1, keepdims=True), jnp.max(s_ctx, axis=1, keepdims=True))
        p_loc = jnp.exp(s_loc - m)
        p_ctx = jnp.exp(s_ctx - m)
        denom = jnp.sum(p_loc, axis=1, keepdims=True) + jnp.sum(p_ctx, axis=1, keepdims=True)
        o = (jnp.dot(p_loc.astype(BF16), vwin, preferred_element_type=F32)
             + jnp.dot(p_ctx.astype(BF16), vc, preferred_element_type=F32)) / denom
        outs.append(o)
    o_ref[...] = jnp.where(lane < NA_DH, outs[0], outs[1]).astype(BF16)


def _na_attention(p, pc, bias, batch, seq, ctx_len, q_col, k_col, v_col):
    rows = p.shape[0]
    n_rows = seq // GRID_W
    qt = NA_QROWS * GRID_W
    n_tiles = n_rows // NA_QROWS
    kw = NA_KROWS * GRID_W
    n_pairs = NA_HEADS // 2

    def geom(t):
        return jnp.where(t == 0, 0, jnp.where(t == n_tiles - 1, 2, 1))

    return pl.pallas_call(
        functools.partial(_na_kernel, n_rows),
        grid=(batch, n_pairs, n_tiles),
        in_specs=[pl.BlockSpec((qt, LANES), lambda b, pr, t: (b * n_tiles + t, q_col + pr)),
                  pl.BlockSpec((seq, LANES), lambda b, pr, t: (b, k_col + pr)),
                  pl.BlockSpec((seq, LANES), lambda b, pr, t: (b, v_col + pr)),
                  pl.BlockSpec((ctx_len, LANES), lambda b, pr, t: (b, k_col + pr)),
                  pl.BlockSpec((ctx_len, LANES), lambda b, pr, t: (b, v_col + pr)),
                  pl.BlockSpec((None, 2, qt, kw), lambda b, pr, t: (geom(t), pr, 0, 0))],
        out_specs=pl.BlockSpec((qt, LANES), lambda b, pr, t: (b * n_tiles + t, pr)),
        out_shape=jax.ShapeDtypeStruct((rows, n_pairs * LANES), BF16),
        compiler_params=_cparams(("arbitrary", "arbitrary", "arbitrary")),
        name="na_attention",
    )(p, p, p, pc, pc, bias)


def _ctx_attn_kernel(q_ref, k_ref, v_ref, o_ref):
    q = q_ref[...] * (NA_DH ** -0.5)
    k = k_ref[...]
    v = v_ref[...]
    lane = lax.broadcasted_iota(jnp.int32, q.shape, 1)
    outs = []
    for hh in range(2):
        sel = (lane < NA_DH) if hh == 0 else (lane >= NA_DH)
        qh = jnp.where(sel, q, jnp.zeros_like(q))
        s = lax.dot_general(qh, k, (((1,), (1,)), ((), ())), preferred_element_type=F32)
        pm = jnp.exp(s - jnp.max(s, axis=1, keepdims=True))
        outs.append(jnp.dot(pm.astype(BF16), v, preferred_element_type=F32)
                    / jnp.sum(pm, axis=1, keepdims=True))
    o_ref[...] = jnp.where(lane < NA_DH, outs[0], outs[1]).astype(BF16)


def _ctx_attention(pc, batch, ctx_len, q_col, k_col, v_col):
    n_pairs = NA_HEADS // 2
    return pl.pallas_call(
        _ctx_attn_kernel,
        grid=(batch, n_pairs),
        in_specs=[pl.BlockSpec((ctx_len, LANES), lambda b, pr: (b, q_col + pr)),
                  pl.BlockSpec((ctx_len, LANES), lambda b, pr: (b, k_col + pr)),
                  pl.BlockSpec((ctx_len, LANES), lambda b, pr: (b, v_col + pr))],
        out_specs=pl.BlockSpec((ctx_len, LANES), lambda b, pr: (b, pr)),
        out_shape=jax.ShapeDtypeStruct((pc.shape[0], n_pairs * LANES), BF16),
        compiler_params=_cparams(("arbitrary", "arbitrary")),
        name="ctx_attention",
    )(pc, pc, pc)


def _na_bias_tables(rpb, n_rows):
    n_tiles = n_rows // NA_QROWS
    n_roff = 2 * NA_WIN_R - 1
    n_coff = 2 * NA_WIN_W - 1
    col = np.arange(GRID_W)
    c0 = np.clip(col - NA_WIN_W // 2, 0, GRID_W - NA_WIN_W)
    col_in = (col[None, :] >= c0[:, None]) & (col[None, :] < c0[:, None] + NA_WIN_W)
    coff = np.clip(col[None, :] - col[:, None] + (NA_WIN_W - 1), 0, n_coff - 1)
    col_sel = (coff[:, :, None] == np.arange(n_coff)).astype(np.float32)
    row_sel = np.zeros((3, NA_QROWS, NA_KROWS, n_roff), np.float32)
    mask = np.zeros((3, NA_QROWS, GRID_W, NA_KROWS, GRID_W), np.float32)
    for g, t in enumerate((0, 1, n_tiles - 1)):
        rs = t * NA_QROWS
        ks = min(max(rs - NA_WIN_R // 2, 0), n_rows - NA_KROWS)
        qrow = rs + np.arange(NA_QROWS)
        krow = ks + np.arange(NA_KROWS)
        r0 = np.clip(qrow - NA_WIN_R // 2, 0, n_rows - NA_WIN_R)
        row_in = (krow[None, :] >= r0[:, None]) & (krow[None, :] < r0[:, None] + NA_WIN_R)
        roff = np.clip(krow[None, :] - qrow[:, None] + (NA_WIN_R - 1), 0, n_roff - 1)
        row_sel[g] = (roff[:, :, None] == np.arange(n_roff)) & row_in[:, :, None]
        ok = row_in[:, None, :, None] & col_in[None, :, None, :]
        mask[g] = np.where(ok, 0.0, -np.inf)
    hp = lax.Precision.HIGHEST
    by_col = jnp.einsum("hrc,qkc->hrqk", rpb, jnp.asarray(col_sel), precision=hp)
    tab = jnp.einsum("gair,hrqk->ghaqik", jnp.asarray(row_sel), by_col, precision=hp)
    tab = tab + jnp.asarray(mask)[:, None]
    return tab.reshape(3, rpb.shape[0], NA_QROWS * GRID_W, NA_KROWS * GRID_W)


def _merge_kernel(of_ref, ob_ref, z_ref, na_ref, gd_ref, gn_ref, x_ref, dnw_ref, wpa_ref, wpb_ref,
                  wout_ref, gpost_ref, g1_ref, o_ref, dn_scr):
    o = of_ref[...] + ob_ref[...]
    z = z_ref[...].astype(F32)
    for hh in range(DN_HEADS):
        sl = slice(hh * LANES, (hh + 1) * LANES)
        oh = o[:, sl]
        oh = oh * lax.rsqrt(jnp.mean(oh * oh, axis=-1, keepdims=True) + EPS) * dnw_ref[...]
        zh = z[:, sl]
        dn_scr[:, sl] = (oh * (zh * jax.nn.sigmoid(zh))).astype(BF16)
    y = (jax.nn.sigmoid(gd_ref[...].astype(F32)) * jnp.dot(dn_scr[...], wpa_ref[...], preferred_element_type=F32)
         + jax.nn.sigmoid(gn_ref[...].astype(F32)) * _bdot(na_ref[...], wpb_ref[...]))
    out = _bdot(y, wout_ref[...])
    o_ref[...] = x_ref[...] + g1_ref[...] * _rms(out, gpost_ref[...])


def _merge(o_f, o_b, p, na_o, x2, dn_norm, w_pa, w_pb, w_out, gpost, mod, mod_row, g1_blk, tm):
    rows, d = x2.shape
    nw = na_o.shape[1]
    row_blk = lambda c: pl.BlockSpec((tm, d), lambda i: (i, c))
    const = lambda shape: pl.BlockSpec(shape, lambda i: (0,) * len(shape))
    return pl.pallas_call(
        _merge_kernel,
        grid=(rows // tm,),
        in_specs=[row_blk(0), row_blk(0), row_blk(3), pl.BlockSpec((tm, nw), lambda i: (i, 0)),
                  row_blk(4), row_blk(5), row_blk(0),
                  const((1, LANES)), const((d, d)), const((nw, d)), const((d, d)), const((1, d)),
                  pl.BlockSpec((None, 1, d), lambda i: (mod_row(i), 0, g1_blk))],
        out_specs=row_blk(0),
        out_shape=jax.ShapeDtypeStruct((rows, d), F32),
        scratch_shapes=[pltpu.VMEM((tm, d), BF16)],
        compiler_params=_cparams(("arbitrary",)),
        name="merge",
    )(o_f, o_b, p, na_o, p, p, x2, dn_norm, w_pa, w_pb, w_out, gpost, mod)


def _ffn_kernel(n_f, x_ref, gpre_ref, sc_ref, sh_ref, w1_ref, w3_ref, w2_ref, gpost_ref, g2_ref,
                o_ref, h_scr, acc_scr):
    j = pl.program_id(1)

    @pl.when(j == 0)
    def _():
        h = _rms(x_ref[...], gpre_ref[...]) * (1.0 + sc_ref[...]) + sh_ref[...]
        h_scr[...] = h.astype(BF16)
        acc_scr[...] = jnp.zeros_like(acc_scr)

    h = h_scr[...]
    a = jnp.dot(h, w1_ref[...], preferred_element_type=F32)
    b = jnp.dot(h, w3_ref[...], preferred_element_type=F32)
    acc_scr[...] += _bdot(a * jax.nn.sigmoid(a) * b, w2_ref[...])

    @pl.when(j == n_f - 1)
    def _():
        o_ref[...] = x_ref[...] + g2_ref[...] * _rms(acc_scr[...], gpost_ref[...])


def _dense_ffn(x2, gpre, gpost, mod, mod_row, w1, w3, w2, tm, tf):
    rows, d = x2.shape
    f = w1.shape[1]
    n_f = f // tf
    modspec = lambda blk: pl.BlockSpec((None, 1, d), lambda i, j: (mod_row(i), 0, blk))
    return pl.pallas_call(
        functools.partial(_ffn_kernel, n_f),
        grid=(rows // tm, n_f),
        in_specs=[pl.BlockSpec((tm, d), lambda i, j: (i, 0)),
                  pl.BlockSpec((1, d), lambda i, j: (0, 0)),
                  modspec(4), modspec(3),
                  pl.BlockSpec((d, tf), lambda i, j: (0, j)),
                  pl.BlockSpec((d, tf), lambda i, j: (0, j)),
                  pl.BlockSpec((tf, d), lambda i, j: (j, 0)),
                  pl.BlockSpec((1, d), lambda i, j: (0, 0)),
                  modspec(5)],
        out_specs=pl.BlockSpec((tm, d), lambda i, j: (i, 0)),
        out_shape=jax.ShapeDtypeStruct((rows, d), F32),
        scratch_shapes=[pltpu.VMEM((tm, d), BF16), pltpu.VMEM((tm, d), F32)],
        compiler_params=_cparams(("arbitrary", "arbitrary")),
        name="dense_ffn",
    )(x2, gpre, mod, mod, w1, w3, w2, gpost, mod)


def _router_kernel(x_ref, gpre_ref, sc_ref, sh_ref, r_ref, h_ref, gate_ref):
    h = _rms(x_ref[...], gpre_ref[...]) * (1.0 + sc_ref[...]) + sh_ref[...]
    h_ref[...] = h
    logits = jnp.dot(h, r_ref[...], precision=lax.Precision.HIGHEST, preferred_element_type=F32)
    lane = lax.broadcasted_iota(jnp.int32, logits.shape, 1)
    neg = -jnp.inf
    l1 = jnp.where(lane < N_EXPERTS, logits, neg)
    m1 = jnp.max(l1, axis=1, keepdims=True)
    i1 = jnp.min(jnp.where(l1 == m1, lane, LANES), axis=1, keepdims=True)
    l2 = jnp.where(lane == i1, neg, l1)
    m2 = jnp.max(l2, axis=1, keepdims=True)
    i2 = jnp.min(jnp.where(l2 == m2, lane, LANES), axis=1, keepdims=True)
    e = jnp.exp(m2 - m1)
    w1 = 1.0 / (1.0 + e)
    w2 = e / (1.0 + e)
    out = jnp.where(lane == 0, i1.astype(F32), 0.0)
    out = jnp.where(lane == 1, i2.astype(F32), out)
    out = jnp.where(lane == 2, w1, out)
    out = jnp.where(lane == 3, w2, out)
    gate_ref[...] = out


def _router(x2, gpre, mod, mod_row, router_pad, tm):
    rows, d = x2.shape
    modspec = lambda blk: pl.BlockSpec((None, 1, d), lambda i: (mod_row(i), 0, blk))
    return pl.pallas_call(
        _router_kernel,
        grid=(rows // tm,),
        in_specs=[pl.BlockSpec((tm, d), lambda i: (i, 0)),
                  pl.BlockSpec((1, d), lambda i: (0, 0)),
                  modspec(4), modspec(3),
                  pl.BlockSpec((d, LANES), lambda i: (0, 0))],
        out_specs=[pl.BlockSpec((tm, d), lambda i: (i, 0)),
                   pl.BlockSpec((tm, LANES), lambda i: (i, 0))],
        out_shape=[jax.ShapeDtypeStruct((rows, d), F32),
                   jax.ShapeDtypeStruct((rows, LANES), F32)],
        compiler_params=_cparams(("arbitrary",)),
        name="moe_router",
    )(x2, gpre, mod, mod, router_pad)


def _gather_row(h_hbm, xbuf, sem, slot, r, tok):
    return pltpu.make_async_copy(h_hbm.at[pl.ds(tok, 1), :], xbuf.at[slot, pl.ds(r, 1), :], sem.at[slot])


def _scatter_row(stage, out_hbm, sem, slot, r, dst):
    return pltpu.make_async_copy(stage.at[slot, pl.ds(r, 1), :], out_hbm.at[pl.ds(dst, 1), :], sem.at[slot])


def _expert_kernel(n_f, n_tiles, te_ref, nv_ref, tok_ref, tokn_ref, dst_ref, h_hbm, w1_ref, w3_ref, w2_ref,
                   out_hbm, xbuf, xb16, acc_scr, stage, gsem, ssem):
    i = pl.program_id(0)
    j = pl.program_id(1)
    slot = i % 2

    def gathers(idx_ref, sl, start):
        def body(r, carry):
            cp = _gather_row(h_hbm, xbuf, gsem, sl, r, idx_ref[0, r] if start else 0)
            cp.start() if start else cp.wait()
            return carry
        lax.fori_loop(0, MOE_TM, body, 0, unroll=8)

    def scatters(sl, start):
        def body(r, carry):
            cp = _scatter_row(stage, out_hbm, ssem, sl, r, dst_ref[0, r] if start else 0)
            cp.start() if start else cp.wait()
            return carry
        lax.fori_loop(0, MOE_TM, body, 0, unroll=8)

    @pl.when(j == 0)
    def _():
        @pl.when(i == 0)
        def _():
            gathers(tok_ref, slot, True)

        gathers(tok_ref, slot, False)

        @pl.when(i + 1 < n_tiles)
        def _():
            gathers(tokn_ref, 1 - slot, True)

        xb16[...] = xbuf[slot].astype(BF16)
        acc_scr[...] = jnp.zeros_like(acc_scr)

    @pl.when(i < nv_ref[0])
    def _():
        x = xb16[...]
        a = jnp.dot(x, w1_ref[...], preferred_element_type=F32)
        b = jnp.dot(x, w3_ref[...], preferred_element_type=F32)
        acc_scr[...] += _bdot(a * jax.nn.sigmoid(a) * b, w2_ref[...])

    @pl.when(j == n_f - 1)
    def _():
        @pl.when(i >= 2)
        def _():
            scatters(slot, False)

        stage[slot] = acc_scr[...]
        scatters(slot, True)

        @pl.when(i == n_tiles - 1)
        def _():
            scatters(slot, False)
            if n_tiles >= 2:
                scatters(1 - slot, False)


def _expert_ffn(h, row_token, out_row, tile_expert, n_valid, w1, w3, w2, tf):
    d = h.shape[1]
    prow = row_token.shape[0]
    f = w1.shape[2]
    n_f = f // tf
    n_tiles = prow // MOE_TM
    idx_spec = lambda fn: pl.BlockSpec((None, 1, MOE_TM), lambda i, j, te, nv: (fn(i), 0, 0),
                                       memory_space=pltpu.SMEM)
    tok3 = row_token.reshape(n_tiles, 1, MOE_TM)
    grid_spec = pltpu.PrefetchScalarGridSpec(
        num_scalar_prefetch=2,
        grid=(n_tiles, n_f),
        in_specs=[idx_spec(lambda i: i),
                  idx_spec(lambda i: jnp.minimum(i + 1, n_tiles - 1)),
                  idx_spec(lambda i: i),
                  pl.BlockSpec(memory_space=pl.ANY),
                  pl.BlockSpec((None, d, tf), lambda i, j, te, nv: (te[i], 0, j)),
                  pl.BlockSpec((None, d, tf), lambda i, j, te, nv: (te[i], 0, j)),
                  pl.BlockSpec((None, tf, d), lambda i, j, te, nv: (te[i], j, 0))],
        out_specs=pl.BlockSpec(memory_space=pl.ANY),
        scratch_shapes=[pltpu.VMEM((2, MOE_TM, d), F32), pltpu.VMEM((MOE_TM, d), BF16),
                        pltpu.VMEM((MOE_TM, d), F32), pltpu.VMEM((2, MOE_TM, d), F32),
                        pltpu.SemaphoreType.DMA((2,)), pltpu.SemaphoreType.DMA((2,))],
    )
    return pl.pallas_call(
        functools.partial(_expert_kernel, n_f, n_tiles),
        grid_spec=grid_spec,
        out_shape=jax.ShapeDtypeStruct((prow, d), F32),
        compiler_params=_cparams(("arbitrary", "arbitrary")),
        name="moe_experts",
    )(tile_expert, n_valid, tok3, tok3, out_row.reshape(n_tiles, 1, MOE_TM), h, w1, w3, w2)


def _combine_kernel(y_ref, gate_ref, x_ref, gpost_ref, g2_ref, o_ref):
    d = x_ref.shape[1]
    gt = gate_ref[...]
    lane = lax.broadcasted_iota(jnp.int32, gt.shape, 1)
    w1 = jnp.sum(jnp.where(lane == 2, gt, 0.0), axis=1, keepdims=True)
    w2 = jnp.sum(jnp.where(lane == 3, gt, 0.0), axis=1, keepdims=True)
    y = w1 * y_ref[:, :d] + w2 * y_ref[:, d:]
    o_ref[...] = x_ref[...] + g2_ref[...] * _rms(y, gpost_ref[...])


def _moe_combine(y_pairs, gates, x2, gpost, mod, mod_row, tm):
    rows, d = x2.shape
    rb = pl.BlockSpec((tm, d), lambda i: (i, 0))
    return pl.pallas_call(
        _combine_kernel,
        grid=(rows // tm,),
        in_specs=[pl.BlockSpec((tm, 2 * d), lambda i: (i, 0)),
                  pl.BlockSpec((tm, LANES), lambda i: (i, 0)), rb,
                  pl.BlockSpec((1, d), lambda i: (0, 0)),
                  pl.BlockSpec((None, 1, d), lambda i: (mod_row(i), 0, 5))],
        out_specs=rb,
        out_shape=jax.ShapeDtypeStruct((rows, d), F32),
        compiler_params=_cparams(("arbitrary",)),
        name="moe_combine",
    )(y_pairs, gates, x2, gpost, mod)


def _moe_schedule(gates, n_tokens):
    idx = gates[:, 0:2].astype(jnp.int32)
    flat_e = idx.reshape(-1)
    onehot = (flat_e[:, None] == jnp.arange(N_EXPERTS)[None, :]).astype(jnp.int32)
    csum = jnp.cumsum(onehot, axis=0)
    counts = csum[-1]
    rank = jnp.sum(csum * onehot, axis=1) - 1
    padded = ((counts + MOE_TM - 1) // MOE_TM) * MOE_TM
    ends = jnp.cumsum(padded)
    starts = ends - padded
    dest = jnp.sum(starts[None, :] * onehot, axis=1) + rank
    n_pairs = 2 * n_tokens
    n_rows = n_pairs + N_EXPERTS * MOE_TM
    n_tiles = n_rows // MOE_TM
    pair_of_row = jnp.full((n_rows,), -1, jnp.int32).at[dest].set(jnp.arange(n_pairs, dtype=jnp.int32))
    is_pad = pair_of_row < 0
    row_token = jnp.where(is_pad, 0, pair_of_row // 2)
    out_row = jnp.where(is_pad, n_pairs - 1 + jnp.cumsum(is_pad.astype(jnp.int32)), pair_of_row)
    tile_start = jnp.arange(n_tiles, dtype=jnp.int32) * MOE_TM
    tile_expert = jnp.minimum(jnp.sum((ends[None, :] <= tile_start[:, None]).astype(jnp.int32), axis=1),
                              N_EXPERTS - 1)
    n_valid = (ends[-1] // MOE_TM).astype(jnp.int32).reshape(1)
    return row_token, out_row, tile_expert, n_valid


def _rope_tables(seq):
    t = jnp.arange(seq)
    row = (t // GRID_W).astype(F32)
    col = (t % GRID_W).astype(F32)
    n_freq = DN_DK // 4
    inv = ROPE_BASE ** (-jnp.arange(n_freq, dtype=F32) / n_freq)
    ang = jnp.concatenate([row[:, None] * inv, col[:, None] * inv], axis=-1)
    cos, sin = jnp.cos(ang), jnp.sin(ang)
    return jnp.concatenate([cos, cos], axis=-1), jnp.concatenate([-sin, sin], axis=-1)


def kernel(x, c, ctx, c_ctx, ada_w, ada_b, norm_mix_pre, norm_mix_post, norm_ffn_pre, norm_ffn_post,
           w_in, dn_conv, dn_a_log, dn_dt_bias, dn_norm, na_rpb, w_branch_dn, w_branch_na, w_out,
           ffn_w1, ffn_w3, ffn_w2, moe_router, moe_w1, moe_w3, moe_w2):
    batch, seq, d = x.shape
    ctx_len = ctx.shape[1]
    depth = w_in.shape[0]
    nh = DN_HEADS
    dn_w = nh * DN_DK
    na_w = NA_HEADS * NA_DH
    n_rows = seq // GRID_W
    assert d == dn_w and seq % SCAN_TILE == 0 and ctx_len % SCAN_TILE == 0 and n_rows % NA_QROWS == 0
    assert depth <= 2, "context tokens only take the dense FFN path"

    c_rows = jnp.zeros((SUBLANES, d), F32).at[:batch].set(c).at[batch].set(c_ctx)
    mod_all = _mod_vectors(c_rows, ada_w, ada_b)
    cos2, sin2 = _rope_tables(seq)
    ones_t = jnp.ones((SCAN_TILE, LANES), F32)

    x2 = x.reshape(batch * seq, d)
    xc2 = ctx.reshape(batch * ctx_len, d)
    lat_tm = 1024
    lat_row = lambda tm: (lambda i: i // (seq // tm))
    ctx_row = lambda i: batch

    q_col, k_col, v_col = (4 * dn_w + 2 * d) // LANES, (4 * dn_w + 2 * d + na_w) // LANES, \
        (4 * dn_w + 2 * d + 2 * na_w) // LANES

    for l in range(depth):
        last = l == depth - 1
        mod = mod_all[l].reshape(SUBLANES, 1, 6 * d)
        wl = w_in[l]
        o_ab = 4 * dn_w
        o_na = o_ab + 4 * nh
        o_gate = o_na + 3 * na_w
        w_main = jnp.concatenate([wl[:, :o_ab], wl[:, o_gate:], wl[:, o_na:o_gate]], axis=1).astype(BF16)
        wab = jnp.pad(wl[:, o_ab:o_na], ((0, 0), (0, LANES - 4 * nh)))
        wab_hi = wab.astype(BF16)
        wab_lo = (wab - wab_hi.astype(F32)).astype(BF16)
        gpre = norm_mix_pre[l].reshape(1, d)
        gpost = norm_mix_post[l].reshape(1, d)

        p, ab = _in_proj(x2, gpre, mod, lat_row(lat_tm), 1, 0, w_main, wab_hi, wab_lo, lat_tm, 512)
        pc, abc = _in_proj(xc2, gpre, mod, ctx_row, 1, 0, w_main, wab_hi, wab_lo, batch * ctx_len, 512)

        conv_w3 = jnp.pad(dn_conv[l].T.reshape(DN_CONV, 3, dn_w).transpose(1, 0, 2),
                          ((0, 0), (0, SUBLANES - DN_CONV), (0, 0)))
        gpar = jnp.zeros((SUBLANES, LANES), F32)
        gpar = gpar.at[0, :2 * nh].set(-jnp.exp(dn_a_log[l].reshape(-1)))
        gpar = gpar.at[1, :2 * nh].set(dn_dt_bias[l].reshape(-1))

        qc_, kc_, vc_, gbc, gbtc = _dn_prep(pc, abc, conv_w3, gpar, ones_t, ones_t, batch, ctx_len, False)
        ql_, kl_, vl_, gbl, gbtl = _dn_prep(p, ab, conv_w3, gpar, cos2, sin2, batch, seq, True)
        s0 = jnp.zeros((batch, nh, 2, LANES, LANES), F32)
        oc_f, oc_b, s_ctx = _dn_scan(qc_, kc_, vc_, gbc, gbtc, s0, batch, ctx_len)
        ol_f, ol_b, _ = _dn_scan(ql_, kl_, vl_, gbl, gbtl, s_ctx, batch, seq)

        bias = _na_bias_tables(na_rpb[l], n_rows)
        na_lat = _na_attention(p, pc, bias, batch, seq, ctx_len, q_col, k_col, v_col)

        dnw = dn_norm[l].reshape(1, LANES)
        w_pa = w_branch_dn[l].astype(BF16)
        w_pb = w_branch_na[l].astype(BF16)
        w_o = w_out[l].astype(BF16)
        x2 = _merge(ol_f, ol_b, p, na_lat, x2, dnw, w_pa, w_pb, w_o, gpost, mod, lat_row(256), 2, 256)

        gfpre = norm_ffn_pre[l].reshape(1, d)
        gfpost = norm_ffn_post[l].reshape(1, d)
        if l % 2 == 0:
            w1 = ffn_w1[l // 2].astype(BF16)
            w3 = ffn_w3[l // 2].astype(BF16)
            w2 = ffn_w2[l // 2].astype(BF16)
            tf = w1.shape[1] // 2
            x2 = _dense_ffn(x2, gfpre, gfpost, mod, lat_row(512), w1, w3, w2, 512, tf)
        else:
            rpad = jnp.pad(moe_router[l // 2], ((0, 0), (0, LANES - N_EXPERTS)))
            hb, gates = _router(x2, gfpre, mod, lat_row(512), rpad, 512)
            n_tok = batch * seq
            row_token, out_row, tile_expert, n_valid = _moe_schedule(gates, n_tok)
            ys = _expert_ffn(hb, row_token, out_row, tile_expert, n_valid, moe_w1[l // 2].astype(BF16),
                             moe_w3[l // 2].astype(BF16), moe_w2[l // 2].astype(BF16), 512)
            x2 = _moe_combine(ys.reshape(-1, 2 * d), gates, x2, gfpost, mod, lat_row(512), 512)

        if not last:
            na_ctx = _ctx_attention(pc, batch, ctx_len, q_col, k_col, v_col)
            xc2 = _merge(oc_f, oc_b, pc, na_ctx, xc2, dnw, w_pa, w_pb, w_o, gpost, mod, ctx_row, 2, 256)
            xc2 = _dense_ffn(xc2, gfpre, gfpost, mod, ctx_row, w1, w3, w2, 512, tf)
    return x2.reshape(batch, seq, d)
```

```python
import functools

import numpy as np
import jax
import jax.numpy as jnp
from jax import lax
from jax.experimental import pallas as pl
from jax.experimental.pallas import tpu as pltpu

F32 = jnp.float32
BF16 = jnp.bfloat16

GRID_W = 64
DN_HEADS = 8
DN_DK = 128
DN_CONV = 5
DN_CHUNK = 64
NA_HEADS = 8
NA_DH = 64
NA_WIN_R = 8
NA_WIN_W = 16
ROPE_BASE = 10000.0
N_EXPERTS = 8
EPS = 1e-6

LANES = 128
SUBLANES = 8
BF16_SUBLANES = 16
VMEM_LIMIT = 56 * 1024 * 1024

SCAN_TILE = 256
SCAN_HEADS = 4
NA_QROWS = 4
NA_KROWS = NA_QROWS + 8
MOE_TM = 512
INPROJ_TN = 1920


def _cparams(sem):
    return pltpu.CompilerParams(dimension_semantics=sem, vmem_limit_bytes=VMEM_LIMIT)


def _bdot(a, b):
    return jnp.dot(a.astype(BF16), b.astype(BF16), preferred_element_type=F32)


def _dot_nt(a, b):
    return lax.dot_general(a.astype(BF16), b.astype(BF16), (((1,), (1,)), ((), ())),
                           preferred_element_type=F32)


def _dot_tn(a, b):
    return lax.dot_general(a.astype(BF16), b.astype(BF16), (((0,), (0,)), ((), ())),
                           preferred_element_type=F32)


def _split3(x):
    hi = x.astype(BF16)
    r = x - hi.astype(F32)
    mid = r.astype(BF16)
    lo = (r - mid.astype(F32)).astype(BF16)
    return hi, mid, lo


def _rms(x, gain):
    return x * lax.rsqrt(jnp.mean(x * x, axis=-1, keepdims=True) + EPS) * gain


def _mod_kernel(c_ref, w_ref, b_ref, o_ref):
    c = c_ref[...]
    s = c * jax.nn.sigmoid(c)
    o_ref[0] = jnp.dot(s, w_ref[0], precision=lax.Precision.HIGHEST,
                       preferred_element_type=F32) + b_ref[0]


def _mod_vectors(c_rows, ada_w, ada_b):
    depth, d, n = ada_w.shape
    tn = 1536
    return pl.pallas_call(
        _mod_kernel,
        grid=(depth, n // tn),
        in_specs=[pl.BlockSpec((SUBLANES, d), lambda l, j: (0, 0)),
                  pl.BlockSpec((1, d, tn), lambda l, j: (l, 0, j)),
                  pl.BlockSpec((1, 1, tn), lambda l, j: (l, 0, j))],
        out_specs=pl.BlockSpec((1, SUBLANES, tn), lambda l, j: (l, 0, j)),
        out_shape=jax.ShapeDtypeStruct((depth, SUBLANES, n), F32),
        compiler_params=_cparams(("arbitrary", "arbitrary")),
        name="mod_vectors",
    )(c_rows, ada_w, ada_b.reshape(depth, 1, n))


def _inproj_kernel(x_ref, g_ref, sc_ref, sh_ref, w_ref, wab_hi_ref, wab_lo_ref,
                   o_ref, ab_ref, h_scr, hlo_scr):
    j = pl.program_id(1)

    @pl.when(j == 0)
    def _():
        h = _rms(x_ref[...], g_ref[...]) * (1.0 + sc_ref[...]) + sh_ref[...]
        hi = h.astype(BF16)
        lo = (h - hi.astype(F32)).astype(BF16)
        h_scr[...] = hi
        hlo_scr[...] = lo
        ab_ref[...] = (jnp.dot(hi, wab_hi_ref[...], preferred_element_type=F32)
                       + jnp.dot(lo, wab_hi_ref[...], preferred_element_type=F32)
                       + jnp.dot(hi, wab_lo_ref[...], preferred_element_type=F32))

    o_ref[...] = jnp.dot(h_scr[...], w_ref[...], preferred_element_type=F32).astype(BF16)


def _in_proj(x2, gain, mod, mod_row, sc_blk, sh_blk, w_main, wab_hi, wab_lo, tm, tn):
    rows, d = x2.shape
    n = w_main.shape[1]
    return pl.pallas_call(
        _inproj_kernel,
        grid=(rows // tm, n // tn),
        in_specs=[pl.BlockSpec((tm, d), lambda i, j: (i, 0)),
                  pl.BlockSpec((1, d), lambda i, j: (0, 0)),
                  pl.BlockSpec((None, 1, d), lambda i, j: (mod_row(i), 0, sc_blk)),
                  pl.BlockSpec((None, 1, d), lambda i, j: (mod_row(i), 0, sh_blk)),
                  pl.BlockSpec((d, tn), lambda i, j: (0, j)),
                  pl.BlockSpec((d, LANES), lambda i, j: (0, 0)),
                  pl.BlockSpec((d, LANES), lambda i, j: (0, 0))],
        out_specs=[pl.BlockSpec((tm, tn), lambda i, j: (i, j)),
                   pl.BlockSpec((tm, LANES), lambda i, j: (i, 0))],
        out_shape=[jax.ShapeDtypeStruct((rows, n), BF16),
                   jax.ShapeDtypeStruct((rows, LANES), F32)],
        scratch_shapes=[pltpu.VMEM((tm, d), BF16), pltpu.VMEM((tm, d), BF16)],
        compiler_params=_cparams(("arbitrary", "arbitrary")),
        name="in_proj",
    )(x2, gain, mod, mod, w_main, wab_hi, wab_lo)


def _prep_kernel(rope, n_tiles,
                 q_ref, qp_ref, qn_ref, k_ref, kp_ref, kn_ref, v_ref, vp_ref, vn_ref,
                 cw_ref, ab_ref, gpar_ref, cos_ref, sin_ref,
                 qo_ref, ko_ref, vo_ref, gb_ref, gbt_ref, xq_scr, xk_scr, xv_scr):
    t = pl.program_id(1)
    tt = q_ref.shape[0]
    first = t == 0
    last = t == n_tiles - 1
    pad = DN_CONV // 2
    halo = qp_ref.shape[0]

    for scr, m_ref, p_ref, n_ref in ((xq_scr, q_ref, qp_ref, qn_ref), (xk_scr, k_ref, kp_ref, kn_ref),
                                     (xv_scr, v_ref, vp_ref, vn_ref)):
        scr[0:halo, :] = jnp.where(first, 0.0, p_ref[...].astype(F32))
        scr[halo:halo + tt, :] = m_ref[...].astype(F32)
        scr[halo + tt:, :] = jnp.where(last, 0.0, n_ref[...].astype(F32))

    def conv_silu(scr, w, sl):
        acc = scr[halo - pad:halo - pad + tt, sl] * w[0:1]
        for i in range(1, DN_CONV):
            o = halo - pad + i
            acc = acc + scr[o:o + tt, sl] * w[i:i + 1]
        return acc * jax.nn.sigmoid(acc)

    def l2n(x):
        return x * lax.rsqrt(jnp.sum(x * x, axis=-1, keepdims=True) + EPS)

    def rot(x):
        if not rope:
            return x
        return x * cos_ref[...] + pltpu.roll(x, LANES // 2, 1) * sin_ref[...]

    for hh in range(DN_HEADS):
        sl = slice(hh * LANES, (hh + 1) * LANES)
        qo_ref[:, sl] = (rot(l2n(conv_silu(xq_scr, cw_ref[0, :, sl], sl))) * (DN_DK ** -0.5)).astype(BF16)
        ko_ref[:, sl] = rot(l2n(conv_silu(xk_scr, cw_ref[1, :, sl], sl))).astype(BF16)
        vo_ref[:, sl] = conv_silu(xv_scr, cw_ref[2, :, sl], sl).astype(BF16)

    ab = ab_ref[...]
    lane = lax.broadcasted_iota(jnp.int32, ab.shape, 1)
    row = lax.broadcasted_iota(jnp.int32, ab.shape, 0) % DN_CHUNK
    xg = ab + gpar_ref[1:2]
    sp = jnp.maximum(xg, 0.0) + jnp.log1p(jnp.exp(-jnp.abs(xg)))
    g = gpar_ref[0:1] * sp
    beta = jax.nn.sigmoid(ab)
    pre = g
    suf = g
    s = 1
    while s < DN_CHUNK:
        pre = pre + jnp.where(row >= s, pltpu.roll(pre, s, 0), 0.0)
        suf = suf + jnp.where(row < DN_CHUNK - s, pltpu.roll(suf, tt - s, 0), 0.0)
        s *= 2
    nh = DN_HEADS
    gb = jnp.where(lane < nh, pre, jnp.where(lane < 2 * nh, suf, beta))
    gb_ref[...] = gb
    er = lax.broadcasted_iota(jnp.int32, (LANES, 3 * LANES), 0)
    ec = lax.broadcasted_iota(jnp.int32, (LANES, 3 * LANES), 1)
    eye3 = ((ec % LANES) == er).astype(BF16)
    gbt_ref[...] = lax.dot_general(eye3, jnp.concatenate(_split3(gb), axis=1),
                                   (((1,), (1,)), ((), ())), preferred_element_type=F32)


def _dn_prep(p, ab, conv_w3, gpar, cos2, sin2, batch, seq, rope):
    rows = p.shape[0]
    tt = SCAN_TILE
    n_tiles = seq // tt
    halo = BF16_SUBLANES
    hb = tt // halo
    n_hblk = rows // halo
    d = DN_HEADS * LANES

    def main(cb):
        return pl.BlockSpec((tt, d), lambda b, t: (b * n_tiles + t, cb))

    def prev(cb):
        return pl.BlockSpec((halo, d), lambda b, t: (jnp.maximum((b * n_tiles + t) * hb - 1, 0), cb))

    def nxt(cb):
        return pl.BlockSpec((halo, d),
                            lambda b, t: (jnp.minimum((b * n_tiles + t + 1) * hb, n_hblk - 1), cb))

    in_specs = []
    for cb in range(3):
        in_specs += [main(cb), prev(cb), nxt(cb)]
    in_specs += [
        pl.BlockSpec((3, SUBLANES, d), lambda b, t: (0, 0, 0)),
        pl.BlockSpec((tt, LANES), lambda b, t: (b * n_tiles + t, 0)),
        pl.BlockSpec((SUBLANES, LANES), lambda b, t: (0, 0)),
        pl.BlockSpec((tt, LANES), lambda b, t: (t, 0)),
        pl.BlockSpec((tt, LANES), lambda b, t: (t, 0)),
    ]
    out_full = pl.BlockSpec((tt, d), lambda b, t: (b * n_tiles + t, 0))
    return pl.pallas_call(
        functools.partial(_prep_kernel, rope, n_tiles),
        grid=(batch, n_tiles),
        in_specs=in_specs,
        out_specs=[out_full, out_full, out_full,
                   pl.BlockSpec((tt, LANES), lambda b, t: (b * n_tiles + t, 0)),
                   pl.BlockSpec((LANES, tt), lambda b, t: (0, b * n_tiles + t))],
        out_shape=[jax.ShapeDtypeStruct((rows, d), BF16)] * 3
        + [jax.ShapeDtypeStruct((rows, LANES), F32), jax.ShapeDtypeStruct((LANES, rows), F32)],
        scratch_shapes=[pltpu.VMEM((tt + 2 * halo, d), F32)] * 3,
        compiler_params=_cparams(("arbitrary", "arbitrary")),
        name="dn_prep_rope" if rope else "dn_prep",
    )(p, p, p, p, p, p, p, p, p, conv_w3, ab, gpar, cos2, sin2)


def _scan_kernel(n_steps,
                 qf_ref, kf_ref, vf_ref, gf_ref, gtf_ref, qb_ref, kb_ref, vb_ref, gb_ref, gtb_ref, s0_ref,
                 of_ref, ob_ref, sfin_ref, s_scr):
    hg = pl.program_id(1)
    step = pl.program_id(2)
    c = DN_CHUNK
    n_chunks = qf_ref.shape[0] // c
    refs = ((qf_ref, kf_ref, vf_ref, gf_ref, gtf_ref, of_ref),
            (qb_ref, kb_ref, vb_ref, gb_ref, gtb_ref, ob_ref))

    @pl.when(step == 0)
    def _():
        s_scr[...] = s0_ref[...]

    ri = lax.broadcasted_iota(jnp.int32, (c, c), 0)
    ci = lax.broadcasted_iota(jnp.int32, (c, c), 1)
    lane = lax.broadcasted_iota(jnp.int32, (c, LANES), 1)
    eye = (ri == ci).astype(F32)
    incl = (ri >= ci, ri <= ci)
    strict = (ri > ci, ri < ci)

    def pick(tile, idx):
        return jnp.sum(jnp.where(lane == idx, tile, 0.0), axis=1, keepdims=True)

    sub = lax.broadcasted_iota(jnp.int32, (DN_HEADS, qf_ref.shape[0]), 0)
    gc_rows = {(hh, dr): jnp.sum(jnp.where(sub == hg * SCAN_HEADS + hh,
                                           refs[dr][4][dr * DN_HEADS:(dr + 1) * DN_HEADS, :], 0.0),
                                 axis=0, keepdims=True)
               for hh in range(SCAN_HEADS) for dr in range(2)}

    items = [(hh, dr, cc) for hh in range(SCAN_HEADS) for dr in range(2) for cc in range(n_chunks)]
    st = []
    for hh, dr, cc in items:
        q_ref, k_ref, v_ref, g_ref, gt_ref, _ = refs[dr]
        rs = slice(cc * c, (cc + 1) * c)
        ls = slice(hh * LANES, (hh + 1) * LANES)
        head = hg * SCAN_HEADS + hh
        gtile = g_ref[rs, :]
        gc = pick(gtile, dr * DN_HEADS + head)
        beta = pick(gtile, (2 + dr) * DN_HEADS + head)
        gc_row = gc_rows[(hh, dr)][:, rs]
        edge = c - 1 if dr == 0 else 0
        g_last = gc_row[:, edge:edge + 1]
        q = q_ref[rs, ls].astype(F32)
        k = k_ref[rs, ls].astype(F32)
        v = v_ref[rs, ls].astype(F32)
        dec = jnp.where(incl[dr], jnp.exp(jnp.where(incl[dr], gc - gc_row, 0.0)), 0.0)
        egc = jnp.exp(gc)
        kbeta = k * beta
        st.append(dict(dr=dr, rs=rs, ls=ls, q=q, k=k, dec=dec, kbeta=kbeta,
                       rhs=jnp.concatenate([v * beta, kbeta * egc], axis=1).astype(BF16),
                       k_dec=(k * jnp.exp(g_last - gc)).astype(BF16),
                       q_dec=(q * egc).astype(BF16),
                       e_last=jnp.exp(g_last)))
    for s in st:
        s["tm"] = jnp.where(strict[s["dr"]], _dot_nt(s["kbeta"], s["k"]) * s["dec"], 0.0)
    for s in st:
        s["attn"] = (_dot_nt(s["q"], s["k"]) * s["dec"]).astype(BF16)

    m8 = (ri // 8) == (ci // 8)
    pw = [-jnp.where(m8, s["tm"], 0.0) for s in st]
    p2 = [_bdot(p, p) for p in pw]
    p4 = [_bdot(p, p) for p in p2]
    xs = [eye + p for p in pw]
    xs = [x + _bdot(p, x) for x, p in zip(xs, p2)]
    xs = [x + _bdot(p, x) for x, p in zip(xs, p4)]
    blk = 8
    while blk < c:
        off = ((ri // (2 * blk)) == (ci // (2 * blk))) & ((ri // blk) != (ci // blk))
        lx = [_bdot(jnp.where(off, s["tm"], 0.0), x) for s, x in zip(st, xs)]
        xs = [x - _bdot(x, y) for x, y in zip(xs, lx)]
        blk *= 2
    uw = [_bdot(x, s["rhs"]) for s, x in zip(st, xs)]

    by_key = {it: (s, y) for it, s, y in zip(items, st, uw)}
    chains = [(hh, dr) for hh in range(SCAN_HEADS) for dr in range(2)]
    state = {ch: s_scr[ch[0], ch[1]] for ch in chains}
    for i in range(n_chunks):
        cur = {ch: by_key[(ch[0], ch[1], i if ch[1] == 0 else n_chunks - 1 - i)] for ch in chains}
        wsqs = {ch: _bdot(jnp.concatenate([cur[ch][1][:, LANES:].astype(BF16), cur[ch][0]["q_dec"]], axis=0),
                          state[ch]) for ch in chains}
        v_new = {ch: cur[ch][1][:, :LANES] - wsqs[ch][:c] for ch in chains}
        for ch in chains:
            s = cur[ch][0]
            refs[ch[1]][5][s["rs"], s["ls"]] = wsqs[ch][c:] + _bdot(s["attn"], v_new[ch])
        state = {ch: state[ch] * cur[ch][0]["e_last"] + _dot_tn(cur[ch][0]["k_dec"], v_new[ch])
                 for ch in chains}
    for ch in chains:
        s_scr[ch[0], ch[1]] = state[ch]

    @pl.when(step == n_steps - 1)
    def _():
        sfin_ref[...] = s_scr[...]


def _dn_scan(qn, kn, vv, gb, gbt, s0, batch, seq):
    rows, d = qn.shape
    tt = SCAN_TILE
    n_steps = seq // tt
    n_groups = DN_HEADS // SCAN_HEADS
    w = SCAN_HEADS * LANES
    fwd_t = lambda b, s: b * n_steps + s
    bwd_t = lambda b, s: b * n_steps + n_steps - 1 - s

    def specs(tile):
        wide = pl.BlockSpec((tt, w), lambda b, g, s: (tile(b, s), g))
        return wide, [wide, wide, wide,
                      pl.BlockSpec((tt, LANES), lambda b, g, s: (tile(b, s), 0)),
                      pl.BlockSpec((LANES, tt), lambda b, g, s: (0, tile(b, s)))]

    out_f, in_f = specs(fwd_t)
    out_b, in_b = specs(bwd_t)
    st_spec = pl.BlockSpec((None, SCAN_HEADS, 2, LANES, LANES), lambda b, g, s: (b, g, 0, 0, 0))
    return pl.pallas_call(
        functools.partial(_scan_kernel, n_steps),
        grid=(batch, n_groups, n_steps),
        in_specs=in_f + in_b + [st_spec],
        out_specs=[out_f, out_b, st_spec],
        out_shape=[jax.ShapeDtypeStruct((rows, d), F32), jax.ShapeDtypeStruct((rows, d), F32),
                   jax.ShapeDtypeStruct((batch, DN_HEADS, 2, LANES, LANES), F32)],
        scratch_shapes=[pltpu.VMEM((SCAN_HEADS, 2, LANES, LANES), F32)],
        compiler_params=_cparams(("arbitrary", "arbitrary", "arbitrary")),
        name="dn_scan",
    )(qn, kn, vv, gb, gbt, qn, kn, vv, gb, gbt, s0)


def _na_kernel(n_rows, q_ref, k_ref, v_ref, kc_ref, vc_ref, bias_ref, o_ref):
    t = pl.program_id(2)
    kw = NA_KROWS * GRID_W
    ks = jnp.clip(t * NA_QROWS - NA_WIN_R // 2, 0, n_rows - NA_KROWS)
    start = pl.multiple_of(ks * GRID_W, GRID_W)
    q = q_ref[...] * (NA_DH ** -0.5)
    kwin = k_ref[pl.ds(start, kw), :]
    vwin = v_ref[pl.ds(start, kw), :]
    kc = kc_ref[...]
    vc = vc_ref[...]
    lane = lax.broadcasted_iota(jnp.int32, q.shape, 1)
    outs = []
    for hh in range(2):
        sel = (lane < NA_DH) if hh == 0 else (lane >= NA_DH)
        qh = jnp.where(sel, q, jnp.zeros_like(q))
        s_loc = lax.dot_general(qh, kwin, (((1,), (1,)), ((), ())),
                                preferred_element_type=F32) + bias_ref[hh]
        s_ctx = lax.dot_general(qh, kc, (((1,), (1,)), ((), ())), preferred_element_type=F32)
        m = jnp.maximum(jnp.max(s_loc, axis=1, keepdims=True), jnp.max(s_ctx, axis=1, keepdims=True))
        p_loc = jnp.exp(s_loc - m)
        p_ctx = jnp.exp(s_ctx - m)
        denom = jnp.sum(p_loc, axis=1, keepdims=True) + jnp.sum(p_ctx, axis=1, keepdims=True)
        o = (jnp.dot(p_loc.astype(BF16), vwin, preferred_element_type=F32)
             + jnp.dot(p_ctx.astype(BF16), vc, preferred_element_type=F32)) / denom
        outs.append(o)
    o_ref[...] = jnp.where(lane < NA_DH, outs[0], outs[1]).astype(BF16)


def _na_attention(p, pc, bias, batch, seq, ctx_len, q_col, k_col, v_col):
    rows = p.shape[0]
    n_rows = seq // GRID_W
    qt = NA_QROWS * GRID_W
    n_tiles = n_rows // NA_QROWS
    kw = NA_KROWS * GRID_W
    n_pairs = NA_HEADS // 2

    def geom(t):
        return jnp.where(t == 0, 0, jnp.where(t == n_tiles - 1, 2, 1))

    return pl.pallas_call(
        functools.partial(_na_kernel, n_rows),
        grid=(batch, n_pairs, n_tiles),
        in_specs=[pl.BlockSpec((qt, LANES), lambda b, pr, t: (b * n_tiles + t, q_col + pr)),
                  pl.BlockSpec((seq, LANES), lambda b, pr, t: (b, k_col + pr)),
                  pl.BlockSpec((seq, LANES), lambda b, pr, t: (b, v_col + pr)),
                  pl.BlockSpec((ctx_len, LANES), lambda b, pr, t: (b, k_col + pr)),
                  pl.BlockSpec((ctx_len, LANES), lambda b, pr, t: (b, v_col + pr)),
                  pl.BlockSpec((None, 2, qt, kw), lambda b, pr, t: (geom(t), pr, 0, 0))],
        out_specs=pl.BlockSpec((qt, LANES), lambda b, pr, t: (b * n_tiles + t, pr)),
        out_shape=jax.ShapeDtypeStruct((rows, n_pairs * LANES), BF16),
        compiler_params=_cparams(("arbitrary", "arbitrary", "arbitrary")),
        name="na_attention",
    )(p, p, p, pc, pc, bias)


def _ctx_attn_kernel(q_ref, k_ref, v_ref, o_ref):
    q = q_ref[...] * (NA_DH ** -0.5)
    k = k_ref[...]
    v = v_ref[...]
    lane = lax.broadcasted_iota(jnp.int32, q.shape, 1)
    outs = []
    for hh in range(2):
        sel = (lane < NA_DH) if hh == 0 else (lane >= NA_DH)
        qh = jnp.where(sel, q, jnp.zeros_like(q))
        s = lax.dot_general(qh, k, (((1,), (1,)), ((), ())), preferred_element_type=F32)
        pm = jnp.exp(s - jnp.max(s, axis=1, keepdims=True))
        outs.append(jnp.dot(pm.astype(BF16), v, preferred_element_type=F32)
                    / jnp.sum(pm, axis=1, keepdims=True))
    o_ref[...] = jnp.where(lane < NA_DH, outs[0], outs[1]).astype(BF16)


def _ctx_attention(pc, batch, ctx_len, q_col, k_col, v_col):
    n_pairs = NA_HEADS // 2
    return pl.pallas_call(
        _ctx_attn_kernel,
        grid=(batch, n_pairs),
        in_specs=[pl.BlockSpec((ctx_len, LANES), lambda b, pr: (b, q_col + pr)),
                  pl.BlockSpec((ctx_len, LANES), lambda b, pr: (b, k_col + pr)),
                  pl.BlockSpec((ctx_len, LANES), lambda b, pr: (b, v_col + pr))],
        out_specs=pl.BlockSpec((ctx_len, LANES), lambda b, pr: (b, pr)),
        out_shape=jax.ShapeDtypeStruct((pc.shape[0], n_pairs * LANES), BF16),
        compiler_params=_cparams(("arbitrary", "arbitrary")),
        name="ctx_attention",
    )(pc, pc, pc)


def _na_bias_tables(rpb, n_rows):
    n_tiles = n_rows // NA_QROWS
    n_roff = 2 * NA_WIN_R - 1
    n_coff = 2 * NA_WIN_W - 1
    col = np.arange(GRID_W)
    c0 = np.clip(col - NA_WIN_W // 2, 0, GRID_W - NA_WIN_W)
    col_in = (col[None, :] >= c0[:, None]) & (col[None, :] < c0[:, None] + NA_WIN_W)
    coff = np.clip(col[None, :] - col[:, None] + (NA_WIN_W - 1), 0, n_coff - 1)
    col_sel = (coff[:, :, None] == np.arange(n_coff)).astype(np.float32)
    row_sel = np.zeros((3, NA_QROWS, NA_KROWS, n_roff), np.float32)
    mask = np.zeros((3, NA_QROWS, GRID_W, NA_KROWS, GRID_W), np.float32)
    for g, t in enumerate((0, 1, n_tiles - 1)):
        rs = t * NA_QROWS
        ks = min(max(rs - NA_WIN_R // 2, 0), n_rows - NA_KROWS)
        qrow = rs + np.arange(NA_QROWS)
        krow = ks + np.arange(NA_KROWS)
        r0 = np.clip(qrow - NA_WIN_R // 2, 0, n_rows - NA_WIN_R)
        row_in = (krow[None, :] >= r0[:, None]) & (krow[None, :] < r0[:, None] + NA_WIN_R)
        roff = np.clip(krow[None, :] - qrow[:, None] + (NA_WIN_R - 1), 0, n_roff - 1)
        row_sel[g] = (roff[:, :, None] == np.arange(n_roff)) & row_in[:, :, None]
        ok = row_in[:, None, :, None] & col_in[None, :, None, :]
        mask[g] = np.where(ok, 0.0, -np.inf)
    hp = lax.Precision.HIGHEST
    by_col = jnp.einsum("hrc,qkc->hrqk", rpb, jnp.asarray(col_sel), precision=hp)
    tab = jnp.einsum("gair,hrqk->ghaqik", jnp.asarray(row_sel), by_col, precision=hp)
    tab = tab + jnp.asarray(mask)[:, None]
    return tab.reshape(3, rpb.shape[0], NA_QROWS * GRID_W, NA_KROWS * GRID_W)


def _merge_kernel(of_ref, ob_ref, z_ref, na_ref, gd_ref, gn_ref, x_ref, dnw_ref, wpa_ref, wpb_ref,
                  wout_ref, gpost_ref, g1_ref, o_ref, dn_scr):
    o = of_ref[...] + ob_ref[...]
    z = z_ref[...].astype(F32)
    for hh in range(DN_HEADS):
        sl = slice(hh * LANES, (hh + 1) * LANES)
        oh = o[:, sl]
        oh = oh * lax.rsqrt(jnp.mean(oh * oh, axis=-1, keepdims=True) + EPS) * dnw_ref[...]
        zh = z[:, sl]
        dn_scr[:, sl] = (oh * (zh * jax.nn.sigmoid(zh))).astype(BF16)
    y = (jax.nn.sigmoid(gd_ref[...].astype(F32)) * jnp.dot(dn_scr[...], wpa_ref[...], preferred_element_type=F32)
         + jax.nn.sigmoid(gn_ref[...].astype(F32)) * _bdot(na_ref[...], wpb_ref[...]))
    out = _bdot(y, wout_ref[...])
    o_ref[...] = x_ref[...] + g1_ref[...] * _rms(out, gpost_ref[...])


def _merge(o_f, o_b, p, na_o, x2, dn_norm, w_pa, w_pb, w_out, gpost, mod, mod_row, g1_blk, tm):
    rows, d = x2.shape
    nw = na_o.shape[1]
    row_blk = lambda c: pl.BlockSpec((tm, d), lambda i: (i, c))
    const = lambda shape: pl.BlockSpec(shape, lambda i: (0,) * len(shape))
    return pl.pallas_call(
        _merge_kernel,
        grid=(rows // tm,),
        in_specs=[row_blk(0), row_blk(0), row_blk(3), pl.BlockSpec((tm, nw), lambda i: (i, 0)),
                  row_blk(4), row_blk(5), row_blk(0),
                  const((1, LANES)), const((d, d)), const((nw, d)), const((d, d)), const((1, d)),
                  pl.BlockSpec((None, 1, d), lambda i: (mod_row(i), 0, g1_blk))],
        out_specs=row_blk(0),
        out_shape=jax.ShapeDtypeStruct((rows, d), F32),
        scratch_shapes=[pltpu.VMEM((tm, d), BF16)],
        compiler_params=_cparams(("arbitrary",)),
        name="merge",
    )(o_f, o_b, p, na_o, p, p, x2, dn_norm, w_pa, w_pb, w_out, gpost, mod)


def _ffn_kernel(n_f, x_ref, gpre_ref, sc_ref, sh_ref, w1_ref, w3_ref, w2_ref, gpost_ref, g2_ref,
                o_ref, h_scr, acc_scr):
    j = pl.program_id(1)

    @pl.when(j == 0)
    def _():
        h = _rms(x_ref[...], gpre_ref[...]) * (1.0 + sc_ref[...]) + sh_ref[...]
        h_scr[...] = h.astype(BF16)
        acc_scr[...] = jnp.zeros_like(acc_scr)

    h = h_scr[...]
    a = jnp.dot(h, w1_ref[...], preferred_element_type=F32)
    b = jnp.dot(h, w3_ref[...], preferred_element_type=F32)
    acc_scr[...] += _bdot(a * jax.nn.sigmoid(a) * b, w2_ref[...])

    @pl.when(j == n_f - 1)
    def _():
        o_ref[...] = x_ref[...] + g2_ref[...] * _rms(acc_scr[...], gpost_ref[...])


def _dense_ffn(x2, gpre, gpost, mod, mod_row, w1, w3, w2, tm, tf):
    rows, d = x2.shape
    f = w1.shape[1]
    n_f = f // tf
    modspec = lambda blk: pl.BlockSpec((None, 1, d), lambda i, j: (mod_row(i), 0, blk))
    return pl.pallas_call(
        functools.partial(_ffn_kernel, n_f),
        grid=(rows // tm, n_f),
        in_specs=[pl.BlockSpec((tm, d), lambda i, j: (i, 0)),
                  pl.BlockSpec((1, d), lambda i, j: (0, 0)),
                  modspec(4), modspec(3),
                  pl.BlockSpec((d, tf), lambda i, j: (0, j)),
                  pl.BlockSpec((d, tf), lambda i, j: (0, j)),
                  pl.BlockSpec((tf, d), lambda i, j: (j, 0)),
                  pl.BlockSpec((1, d), lambda i, j: (0, 0)),
                  modspec(5)],
        out_specs=pl.BlockSpec((tm, d), lambda i, j: (i, 0)),
        out_shape=jax.ShapeDtypeStruct((rows, d), F32),
        scratch_shapes=[pltpu.VMEM((tm, d), BF16), pltpu.VMEM((tm, d), F32)],
        compiler_params=_cparams(("arbitrary", "arbitrary")),
        name="dense_ffn",
    )(x2, gpre, mod, mod, w1, w3, w2, gpost, mod)


def _router_kernel(x_ref, gpre_ref, sc_ref, sh_ref, r_ref, h_ref, gate_ref):
    h = _rms(x_ref[...], gpre_ref[...]) * (1.0 + sc_ref[...]) + sh_ref[...]
    h_ref[...] = h
    logits = jnp.dot(h, r_ref[...], precision=lax.Precision.HIGHEST, preferred_element_type=F32)
    lane = lax.broadcasted_iota(jnp.int32, logits.shape, 1)
    neg = -jnp.inf
    l1 = jnp.where(lane < N_EXPERTS, logits, neg)
    m1 = jnp.max(l1, axis=1, keepdims=True)
    i1 = jnp.min(jnp.where(l1 == m1, lane, LANES), axis=1, keepdims=True)
    l2 = jnp.where(lane == i1, neg, l1)
    m2 = jnp.max(l2, axis=1, keepdims=True)
    i2 = jnp.min(jnp.where(l2 == m2, lane, LANES), axis=1, keepdims=True)
    e = jnp.exp(m2 - m1)
    w1 = 1.0 / (1.0 + e)
    w2 = e / (1.0 + e)
    out = jnp.where(lane == 0, i1.astype(F32), 0.0)
    out = jnp.where(lane == 1, i2.astype(F32), out)
    out = jnp.where(lane == 2, w1, out)
    out = jnp.where(lane == 3, w2, out)
    gate_ref[...] = out


def _router(x2, gpre, mod, mod_row, router_pad, tm):
    rows, d = x2.shape
    modspec = lambda blk: pl.BlockSpec((None, 1, d), lambda i: (mod_row(i), 0, blk))
    return pl.pallas_call(
        _router_kernel,
        grid=(rows // tm,),
        in_specs=[pl.BlockSpec((tm, d), lambda i: (i, 0)),
                  pl.BlockSpec((1, d), lambda i: (0, 0)),
                  modspec(4), modspec(3),
                  pl.BlockSpec((d, LANES), lambda i: (0, 0))],
        out_specs=[pl.BlockSpec((tm, d), lambda i: (i, 0)),
                   pl.BlockSpec((tm, LANES), lambda i: (i, 0))],
        out_shape=[jax.ShapeDtypeStruct((rows, d), F32),
                   jax.ShapeDtypeStruct((rows, LANES), F32)],
        compiler_params=_cparams(("arbitrary",)),
        name="moe_router",
    )(x2, gpre, mod, mod, router_pad)


def _gather_row(h_hbm, xbuf, sem, slot, r, tok):
    return pltpu.make_async_copy(h_hbm.at[pl.ds(tok, 1), :], xbuf.at[slot, pl.ds(r, 1), :], sem.at[slot])


def _scatter_row(stage, out_hbm, sem, slot, r, dst):
    return pltpu.make_async_copy(stage.at[slot, pl.ds(r, 1), :], out_hbm.at[pl.ds(dst, 1), :], sem.at[slot])


def _expert_kernel(n_f, n_tiles, te_ref, nv_ref, tok_ref, tokn_ref, dstp_ref, dst_ref, h_hbm,
                   w1_ref, w3_ref, w2_ref, out_hbm, xbuf, xb16, acc_scr, stage, gsem, ssem):
    i = pl.program_id(0)
    j = pl.program_id(1)
    slot = i % 2
    rows_per_step = MOE_TM // n_f

    def wait_gathers(sl):
        pltpu.make_async_copy(h_hbm.at[pl.ds(0, MOE_TM), :], xbuf.at[sl], gsem.at[sl]).wait()

    def wait_scatters(sl):
        pltpu.make_async_copy(stage.at[sl], out_hbm.at[pl.ds(0, MOE_TM), :], ssem.at[sl]).wait()

    @pl.when((i == 0) & (j == 0))
    def _():
        stage[...] = jnp.zeros_like(stage)

        def body(r, carry):
            _gather_row(h_hbm, xbuf, gsem, slot, r, tok_ref[0, r]).start()
            return carry
        lax.fori_loop(0, MOE_TM, body, 0, unroll=8)

    @pl.when(j == 0)
    def _():
        wait_gathers(slot)
        xb16[...] = xbuf[slot].astype(BF16)
        acc_scr[...] = jnp.zeros_like(acc_scr)

    def row_dmas():
        for k in range(rows_per_step):
            r = j * rows_per_step + k
            _gather_row(h_hbm, xbuf, gsem, 1 - slot, r, tokn_ref[0, r]).start()
            _scatter_row(stage, out_hbm, ssem, 1 - slot, r, dstp_ref[0, r]).start()

    valid = i < nv_ref[0]

    @pl.when(valid)
    def _():
        row_dmas()
        x = xb16[...]
        a = jnp.dot(x, w1_ref[...], preferred_element_type=F32)
        b = jnp.dot(x, w3_ref[...], preferred_element_type=F32)
        acc_scr[...] += _bdot(a * jax.nn.sigmoid(a) * b, w2_ref[...])

    @pl.when(jnp.logical_not(valid))
    def _():
        row_dmas()

    @pl.when(j == n_f - 1)
    def _():
        @pl.when(i >= 1)
        def _():
            wait_scatters(slot)

        stage[slot] = acc_scr[...]

        @pl.when(i == n_tiles - 1)
        def _():
            wait_scatters(1 - slot)
            wait_gathers(1 - slot)

            def body(r, carry):
                _scatter_row(stage, out_hbm, ssem, slot, r, dst_ref[0, r]).start()
                return carry
            lax.fori_loop(0, MOE_TM, body, 0, unroll=8)
            wait_scatters(slot)


def _expert_ffn(h, row_token, out_row, tile_expert, n_valid, w1, w3, w2, tf):
    d = h.shape[1]
    prow = row_token.shape[0]
    f = w1.shape[2]
    n_f = f // tf
    n_tiles = prow // MOE_TM
    idx_spec = lambda fn: pl.BlockSpec((None, 1, MOE_TM), lambda i, j, te, nv: (fn(i), 0, 0),
                                       memory_space=pltpu.SMEM)
    tok3 = row_token.reshape(n_tiles, 1, MOE_TM)
    dst3 = jnp.concatenate([prow + jnp.arange(MOE_TM, dtype=jnp.int32), out_row]).reshape(n_tiles + 1, 1, MOE_TM)
    grid_spec = pltpu.PrefetchScalarGridSpec(
        num_scalar_prefetch=2,
        grid=(n_tiles, n_f),
        in_specs=[idx_spec(lambda i: i),
                  idx_spec(lambda i: jnp.minimum(i + 1, n_tiles - 1)),
                  idx_spec(lambda i: i),
                  idx_spec(lambda i: i + 1),
                  pl.BlockSpec(memory_space=pl.ANY),
                  pl.BlockSpec((None, d, tf), lambda i, j, te, nv: (te[i], 0, j)),
                  pl.BlockSpec((None, d, tf), lambda i, j, te, nv: (te[i], 0, j)),
                  pl.BlockSpec((None, tf, d), lambda i, j, te, nv: (te[i], j, 0))],
        out_specs=pl.BlockSpec(memory_space=pl.ANY),
        scratch_shapes=[pltpu.VMEM((2, MOE_TM, d), F32), pltpu.VMEM((MOE_TM, d), BF16),
                        pltpu.VMEM((MOE_TM, d), F32), pltpu.VMEM((2, MOE_TM, d), F32),
                        pltpu.SemaphoreType.DMA((2,)), pltpu.SemaphoreType.DMA((2,))],
    )
    return pl.pallas_call(
        functools.partial(_expert_kernel, n_f, n_tiles),
        grid_spec=grid_spec,
        out_shape=jax.ShapeDtypeStruct((prow + MOE_TM, d), F32),
        compiler_params=_cparams(("arbitrary", "arbitrary")),
        name="moe_experts",
    )(tile_expert, n_valid, tok3, tok3, dst3, dst3, h, w1, w3, w2)


def _combine_kernel(y1_ref, y2_ref, gate_ref, x_ref, gpost_ref, g2_ref, o_ref):
    gt = gate_ref[...]
    lane = lax.broadcasted_iota(jnp.int32, gt.shape, 1)
    w1 = jnp.sum(jnp.where(lane == 2, gt, 0.0), axis=1, keepdims=True)
    w2 = jnp.sum(jnp.where(lane == 3, gt, 0.0), axis=1, keepdims=True)
    y = w1 * y1_ref[...] + w2 * y2_ref[...]
    o_ref[...] = x_ref[...] + g2_ref[...] * _rms(y, gpost_ref[...])


def _moe_combine(ys, gates, x2, gpost, mod, mod_row, tm):
    rows, d = x2.shape
    rb = pl.BlockSpec((tm, d), lambda i: (i, 0))
    return pl.pallas_call(
        _combine_kernel,
        grid=(rows // tm,),
        in_specs=[rb, pl.BlockSpec((tm, d), lambda i: (rows // tm + i, 0)),
                  pl.BlockSpec((tm, LANES), lambda i: (i, 0)), rb,
                  pl.BlockSpec((1, d), lambda i: (0, 0)),
                  pl.BlockSpec((None, 1, d), lambda i: (mod_row(i), 0, 5))],
        out_specs=rb,
        out_shape=jax.ShapeDtypeStruct((rows, d), F32),
        compiler_params=_cparams(("arbitrary",)),
        name="moe_combine",
    )(ys, ys, gates, x2, gpost, mod)


def _moe_schedule(gates, n_tokens):
    idx = gates[:, 0:2].astype(jnp.int32)
    flat_e = idx.reshape(-1)
    onehot = (flat_e[:, None] == jnp.arange(N_EXPERTS)[None, :]).astype(jnp.int32)
    csum = jnp.cumsum(onehot, axis=0)
    counts = csum[-1]
    rank = jnp.sum(csum * onehot, axis=1) - 1
    padded = ((counts + MOE_TM - 1) // MOE_TM) * MOE_TM
    ends = jnp.cumsum(padded)
    starts = ends - padded
    dest = jnp.sum(starts[None, :] * onehot, axis=1) + rank
    n_pairs = 2 * n_tokens
    n_rows = n_pairs + N_EXPERTS * MOE_TM
    n_tiles = n_rows // MOE_TM
    pair_of_row = jnp.full((n_rows,), -1, jnp.int32).at[dest].set(jnp.arange(n_pairs, dtype=jnp.int32))
    is_pad = pair_of_row < 0
    row_token = jnp.where(is_pad, 0, pair_of_row // 2)
    out_row = jnp.where(is_pad, n_pairs - 1 + jnp.cumsum(is_pad.astype(jnp.int32)),
                        (pair_of_row % 2) * n_tokens + pair_of_row // 2)
    tile_start = jnp.arange(n_tiles, dtype=jnp.int32) * MOE_TM
    tile_expert = jnp.minimum(jnp.sum((ends[None, :] <= tile_start[:, None]).astype(jnp.int32), axis=1),
                              N_EXPERTS - 1)
    n_valid = (ends[-1] // MOE_TM).astype(jnp.int32).reshape(1)
    return row_token, out_row, tile_expert, n_valid


def _rope_tables(seq):
    t = jnp.arange(seq)
    row = (t // GRID_W).astype(F32)
    col = (t % GRID_W).astype(F32)
    n_freq = DN_DK // 4
    inv = ROPE_BASE ** (-jnp.arange(n_freq, dtype=F32) / n_freq)
    ang = jnp.concatenate([row[:, None] * inv, col[:, None] * inv], axis=-1)
    cos, sin = jnp.cos(ang), jnp.sin(ang)
    return jnp.concatenate([cos, cos], axis=-1), jnp.concatenate([-sin, sin], axis=-1)


def kernel(x, c, ctx, c_ctx, ada_w, ada_b, norm_mix_pre, norm_mix_post, norm_ffn_pre, norm_ffn_post,
           w_in, dn_conv, dn_a_log, dn_dt_bias, dn_norm, na_rpb, w_branch_dn, w_branch_na, w_out,
           ffn_w1, ffn_w3, ffn_w2, moe_router, moe_w1, moe_w3, moe_w2):
    batch, seq, d = x.shape
    ctx_len = ctx.shape[1]
    depth = w_in.shape[0]
    nh = DN_HEADS
    dn_w = nh * DN_DK
    na_w = NA_HEADS * NA_DH
    n_rows = seq // GRID_W
    assert d == dn_w and seq % SCAN_TILE == 0 and ctx_len % SCAN_TILE == 0 and n_rows % NA_QROWS == 0
    assert depth <= 2, "context tokens only take the dense FFN path"

    c_rows = jnp.zeros((SUBLANES, d), F32).at[:batch].set(c).at[batch].set(c_ctx)
    mod_all = _mod_vectors(c_rows, ada_w, ada_b)
    cos2, sin2 = _rope_tables(seq)
    ones_t = jnp.ones((SCAN_TILE, LANES), F32)

    x2 = x.reshape(batch * seq, d)
    xc2 = ctx.reshape(batch * ctx_len, d)
    lat_tm = 1024
    lat_row = lambda tm: (lambda i: i // (seq // tm))
    ctx_row = lambda i: batch

    q_col, k_col, v_col = (4 * dn_w + 2 * d) // LANES, (4 * dn_w + 2 * d + na_w) // LANES, \
        (4 * dn_w + 2 * d + 2 * na_w) // LANES

    for l in range(depth):
        last = l == depth - 1
        mod = mod_all[l].reshape(SUBLANES, 1, 6 * d)
        wl = w_in[l]
        o_ab = 4 * dn_w
        o_na = o_ab + 4 * nh
        o_gate = o_na + 3 * na_w
        w_main = jnp.concatenate([wl[:, :o_ab], wl[:, o_gate:], wl[:, o_na:o_gate]], axis=1).astype(BF16)
        wab = jnp.pad(wl[:, o_ab:o_na], ((0, 0), (0, LANES - 4 * nh)))
        wab_hi = wab.astype(BF16)
        wab_lo = (wab - wab_hi.astype(F32)).astype(BF16)
        gpre = norm_mix_pre[l].reshape(1, d)
        gpost = norm_mix_post[l].reshape(1, d)

        p, ab = _in_proj(x2, gpre, mod, lat_row(lat_tm), 1, 0, w_main, wab_hi, wab_lo, lat_tm, INPROJ_TN)
        pc, abc = _in_proj(xc2, gpre, mod, ctx_row, 1, 0, w_main, wab_hi, wab_lo, batch * ctx_len, INPROJ_TN)

        conv_w3 = jnp.pad(dn_conv[l].T.reshape(DN_CONV, 3, dn_w).transpose(1, 0, 2),
                          ((0, 0), (0, SUBLANES - DN_CONV), (0, 0)))
        gpar = jnp.zeros((SUBLANES, LANES), F32)
        gpar = gpar.at[0, :2 * nh].set(-jnp.exp(dn_a_log[l].reshape(-1)))
        gpar = gpar.at[1, :2 * nh].set(dn_dt_bias[l].reshape(-1))

        qc_, kc_, vc_, gbc, gbtc = _dn_prep(pc, abc, conv_w3, gpar, ones_t, ones_t, batch, ctx_len, False)
        ql_, kl_, vl_, gbl, gbtl = _dn_prep(p, ab, conv_w3, gpar, cos2, sin2, batch, seq, True)
        s0 = jnp.zeros((batch, nh, 2, LANES, LANES), F32)
        oc_f, oc_b, s_ctx = _dn_scan(qc_, kc_, vc_, gbc, gbtc, s0, batch, ctx_len)
        ol_f, ol_b, _ = _dn_scan(ql_, kl_, vl_, gbl, gbtl, s_ctx, batch, seq)

        bias = _na_bias_tables(na_rpb[l], n_rows)
        na_lat = _na_attention(p, pc, bias, batch, seq, ctx_len, q_col, k_col, v_col)

        dnw = dn_norm[l].reshape(1, LANES)
        w_pa = w_branch_dn[l].astype(BF16)
        w_pb = w_branch_na[l].astype(BF16)
        w_o = w_out[l].astype(BF16)
        x2 = _merge(ol_f, ol_b, p, na_lat, x2, dnw, w_pa, w_pb, w_o, gpost, mod, lat_row(256), 2, 256)

        gfpre = norm_ffn_pre[l].reshape(1, d)
        gfpost = norm_ffn_post[l].reshape(1, d)
        if l % 2 == 0:
            w1 = ffn_w1[l // 2].astype(BF16)
            w3 = ffn_w3[l // 2].astype(BF16)
            w2 = ffn_w2[l // 2].astype(BF16)
            tf = w1.shape[1] // 2
            x2 = _dense_ffn(x2, gfpre, gfpost, mod, lat_row(512), w1, w3, w2, 512, tf)
        else:
            rpad = jnp.pad(moe_router[l // 2], ((0, 0), (0, LANES - N_EXPERTS)))
            hb, gates = _router(x2, gfpre, mod, lat_row(512), rpad, 512)
            n_tok = batch * seq
            row_token, out_row, tile_expert, n_valid = _moe_schedule(gates, n_tok)
            ys = _expert_ffn(hb, row_token, out_row, tile_expert, n_valid, moe_w1[l // 2].astype(BF16),
                             moe_w3[l // 2].astype(BF16), moe_w2[l // 2].astype(BF16), 896)
            x2 = _moe_combine(ys, gates, x2, gfpost, mod, lat_row(512), 512)

        if not last:
            na_ctx = _ctx_attention(pc, batch, ctx_len, q_col, k_col, v_col)
            xc2 = _merge(oc_f, oc_b, pc, na_ctx, xc2, dnw, w_pa, w_pb, w_o, gpost, mod, ctx_row, 2, 256)
            xc2 = _dense_ffn(xc2, gfpre, gfpost, mod, ctx_row, w1, w3, w2, 512, tf)
    return x2.reshape(batch, seq, d)
```

```python
import functools

import numpy as np
import jax
import jax.numpy as jnp
from jax import lax
from jax.experimental import pallas as pl
from jax.experimental.pallas import tpu as pltpu

F32 = jnp.float32
BF16 = jnp.bfloat16

GRID_W = 64
DN_HEADS = 8
DN_DK = 128
DN_CONV = 5
DN_CHUNK = 64
NA_HEADS = 8
NA_DH = 64
NA_WIN_R = 8
NA_WIN_W = 16
ROPE_BASE = 10000.0
N_EXPERTS = 8
EPS = 1e-6

LANES = 128
SUBLANES = 8
BF16_SUBLANES = 16
VMEM_LIMIT = 56 * 1024 * 1024

SCAN_TILE = 256
SCAN_HEADS = 8
NA_QROWS = 4
NA_KROWS = NA_QROWS + 8
MOE_TM = 512
INPROJ_TN = 1920


def _cparams(sem):
    return pltpu.CompilerParams(dimension_semantics=sem, vmem_limit_bytes=VMEM_LIMIT)


def _bdot(a, b):
    return jnp.dot(a.astype(BF16), b.astype(BF16), preferred_element_type=F32)


def _dot_nt(a, b):
    return lax.dot_general(a.astype(BF16), b.astype(BF16), (((1,), (1,)), ((), ())),
                           preferred_element_type=F32)


def _dot_tn(a, b):
    return lax.dot_general(a.astype(BF16), b.astype(BF16), (((0,), (0,)), ((), ())),
                           preferred_element_type=F32)


def _split3(x):
    hi = x.astype(BF16)
    r = x - hi.astype(F32)
    mid = r.astype(BF16)
    lo = (r - mid.astype(F32)).astype(BF16)
    return hi, mid, lo


def _rms(x, gain):
    return x * lax.rsqrt(jnp.mean(x * x, axis=-1, keepdims=True) + EPS) * gain


def _mod_kernel(c_ref, w_ref, b_ref, o_ref):
    c = c_ref[...]
    s = c * jax.nn.sigmoid(c)
    o_ref[0] = jnp.dot(s, w_ref[0], precision=lax.Precision.HIGHEST,
                       preferred_element_type=F32) + b_ref[0]


def _mod_vectors(c_rows, ada_w, ada_b):
    depth, d, n = ada_w.shape
    tn = 1536
    return pl.pallas_call(
        _mod_kernel,
        grid=(depth, n // tn),
        in_specs=[pl.BlockSpec((SUBLANES, d), lambda l, j: (0, 0)),
                  pl.BlockSpec((1, d, tn), lambda l, j: (l, 0, j)),
                  pl.BlockSpec((1, 1, tn), lambda l, j: (l, 0, j))],
        out_specs=pl.BlockSpec((1, SUBLANES, tn), lambda l, j: (l, 0, j)),
        out_shape=jax.ShapeDtypeStruct((depth, SUBLANES, n), F32),
        compiler_params=_cparams(("arbitrary", "arbitrary")),
        name="mod_vectors",
    )(c_rows, ada_w, ada_b.reshape(depth, 1, n))


def _inproj_kernel(x_ref, g_ref, sc_ref, sh_ref, w_ref, wab_hi_ref, wab_lo_ref,
                   o_ref, ab_ref, h_scr, hlo_scr):
    j = pl.program_id(1)

    @pl.when(j == 0)
    def _():
        h = _rms(x_ref[...], g_ref[...]) * (1.0 + sc_ref[...]) + sh_ref[...]
        hi = h.astype(BF16)
        lo = (h - hi.astype(F32)).astype(BF16)
        h_scr[...] = hi
        hlo_scr[...] = lo
        ab_ref[...] = (jnp.dot(hi, wab_hi_ref[...], preferred_element_type=F32)
                       + jnp.dot(lo, wab_hi_ref[...], preferred_element_type=F32)
                       + jnp.dot(hi, wab_lo_ref[...], preferred_element_type=F32))

    o_ref[...] = jnp.dot(h_scr[...], w_ref[...], preferred_element_type=F32).astype(BF16)


def _in_proj(x2, gain, mod, mod_row, sc_blk, sh_blk, w_main, wab_hi, wab_lo, tm, tn):
    rows, d = x2.shape
    n = w_main.shape[1]
    return pl.pallas_call(
        _inproj_kernel,
        grid=(rows // tm, n // tn),
        in_specs=[pl.BlockSpec((tm, d), lambda i, j: (i, 0)),
                  pl.BlockSpec((1, d), lambda i, j: (0, 0)),
                  pl.BlockSpec((None, 1, d), lambda i, j: (mod_row(i), 0, sc_blk)),
                  pl.BlockSpec((None, 1, d), lambda i, j: (mod_row(i), 0, sh_blk)),
                  pl.BlockSpec((d, tn), lambda i, j: (0, j)),
                  pl.BlockSpec((d, LANES), lambda i, j: (0, 0)),
                  pl.BlockSpec((d, LANES), lambda i, j: (0, 0))],
        out_specs=[pl.BlockSpec((tm, tn), lambda i, j: (i, j)),
                   pl.BlockSpec((tm, LANES), lambda i, j: (i, 0))],
        out_shape=[jax.ShapeDtypeStruct((rows, n), BF16),
                   jax.ShapeDtypeStruct((rows, LANES), F32)],
        scratch_shapes=[pltpu.VMEM((tm, d), BF16), pltpu.VMEM((tm, d), BF16)],
        compiler_params=_cparams(("arbitrary", "arbitrary")),
        name="in_proj",
    )(x2, gain, mod, mod, w_main, wab_hi, wab_lo)


def _prep_kernel(rope, n_tiles,
                 q_ref, qp_ref, qn_ref, k_ref, kp_ref, kn_ref, v_ref, vp_ref, vn_ref,
                 cw_ref, ab_ref, gpar_ref, cos_ref, sin_ref,
                 qo_ref, ko_ref, vo_ref, gb_ref, gbt_ref, xq_scr, xk_scr, xv_scr):
    t = pl.program_id(1)
    tt = q_ref.shape[0]
    first = t == 0
    last = t == n_tiles - 1
    pad = DN_CONV // 2
    halo = qp_ref.shape[0]

    for scr, m_ref, p_ref, n_ref in ((xq_scr, q_ref, qp_ref, qn_ref), (xk_scr, k_ref, kp_ref, kn_ref),
                                     (xv_scr, v_ref, vp_ref, vn_ref)):
        scr[0:halo, :] = jnp.where(first, 0.0, p_ref[...].astype(F32))
        scr[halo:halo + tt, :] = m_ref[...].astype(F32)
        scr[halo + tt:, :] = jnp.where(last, 0.0, n_ref[...].astype(F32))

    def conv_silu(scr, w, sl):
        acc = scr[halo - pad:halo - pad + tt, sl] * w[0:1]
        for i in range(1, DN_CONV):
            o = halo - pad + i
            acc = acc + scr[o:o + tt, sl] * w[i:i + 1]
        return acc * jax.nn.sigmoid(acc)

    def l2n(x):
        return x * lax.rsqrt(jnp.sum(x * x, axis=-1, keepdims=True) + EPS)

    def rot(x):
        if not rope:
            return x
        return x * cos_ref[...] + pltpu.roll(x, LANES // 2, 1) * sin_ref[...]

    for hh in range(DN_HEADS):
        sl = slice(hh * LANES, (hh + 1) * LANES)
        qo_ref[:, sl] = (rot(l2n(conv_silu(xq_scr, cw_ref[0, :, sl], sl))) * (DN_DK ** -0.5)).astype(BF16)
        ko_ref[:, sl] = rot(l2n(conv_silu(xk_scr, cw_ref[1, :, sl], sl))).astype(BF16)
        vo_ref[:, sl] = conv_silu(xv_scr, cw_ref[2, :, sl], sl).astype(BF16)

    ab = ab_ref[...]
    lane = lax.broadcasted_iota(jnp.int32, ab.shape, 1)
    row = lax.broadcasted_iota(jnp.int32, ab.shape, 0) % DN_CHUNK
    xg = ab + gpar_ref[1:2]
    sp = jnp.maximum(xg, 0.0) + jnp.log1p(jnp.exp(-jnp.abs(xg)))
    g = gpar_ref[0:1] * sp
    beta = jax.nn.sigmoid(ab)
    pre = g
    suf = g
    s = 1
    while s < DN_CHUNK:
        pre = pre + jnp.where(row >= s, pltpu.roll(pre, s, 0), 0.0)
        suf = suf + jnp.where(row < DN_CHUNK - s, pltpu.roll(suf, tt - s, 0), 0.0)
        s *= 2
    nh = DN_HEADS
    gb = jnp.where(lane < nh, pre, jnp.where(lane < 2 * nh, suf, beta))
    gb_ref[...] = gb
    er = lax.broadcasted_iota(jnp.int32, (LANES, 3 * LANES), 0)
    ec = lax.broadcasted_iota(jnp.int32, (LANES, 3 * LANES), 1)
    eye3 = ((ec % LANES) == er).astype(BF16)
    gbt_ref[...] = lax.dot_general(eye3, jnp.concatenate(_split3(gb), axis=1),
                                   (((1,), (1,)), ((), ())), preferred_element_type=F32)


def _dn_prep(p, ab, conv_w3, gpar, cos2, sin2, batch, seq, rope):
    rows = p.shape[0]
    tt = SCAN_TILE
    n_tiles = seq // tt
    halo = BF16_SUBLANES
    hb = tt // halo
    n_hblk = rows // halo
    d = DN_HEADS * LANES

    def main(cb):
        return pl.BlockSpec((tt, d), lambda b, t: (b * n_tiles + t, cb))

    def prev(cb):
        return pl.BlockSpec((halo, d), lambda b, t: (jnp.maximum((b * n_tiles + t) * hb - 1, 0), cb))

    def nxt(cb):
        return pl.BlockSpec((halo, d),
                            lambda b, t: (jnp.minimum((b * n_tiles + t + 1) * hb, n_hblk - 1), cb))

    in_specs = []
    for cb in range(3):
        in_specs += [main(cb), prev(cb), nxt(cb)]
    in_specs += [
        pl.BlockSpec((3, SUBLANES, d), lambda b, t: (0, 0, 0)),
        pl.BlockSpec((tt, LANES), lambda b, t: (b * n_tiles + t, 0)),
        pl.BlockSpec((SUBLANES, LANES), lambda b, t: (0, 0)),
        pl.BlockSpec((tt, LANES), lambda b, t: (t, 0)),
        pl.BlockSpec((tt, LANES), lambda b, t: (t, 0)),
    ]
    out_full = pl.BlockSpec((tt, d), lambda b, t: (b * n_tiles + t, 0))
    return pl.pallas_call(
        functools.partial(_prep_kernel, rope, n_tiles),
        grid=(batch, n_tiles),
        in_specs=in_specs,
        out_specs=[out_full, out_full, out_full,
                   pl.BlockSpec((tt, LANES), lambda b, t: (b * n_tiles + t, 0)),
                   pl.BlockSpec((LANES, tt), lambda b, t: (0, b * n_tiles + t))],
        out_shape=[jax.ShapeDtypeStruct((rows, d), BF16)] * 3
        + [jax.ShapeDtypeStruct((rows, LANES), F32), jax.ShapeDtypeStruct((LANES, rows), F32)],
        scratch_shapes=[pltpu.VMEM((tt + 2 * halo, d), F32)] * 3,
        compiler_params=_cparams(("arbitrary", "arbitrary")),
        name="dn_prep_rope" if rope else "dn_prep",
    )(p, p, p, p, p, p, p, p, p, conv_w3, ab, gpar, cos2, sin2)


def _scan_kernel(n_steps,
                 qf_ref, kf_ref, vf_ref, gf_ref, gtf_ref, qb_ref, kb_ref, vb_ref, gb_ref, gtb_ref, s0_ref,
                 of_ref, ob_ref, sfin_ref, s_scr):
    hg = pl.program_id(1)
    step = pl.program_id(2)
    c = DN_CHUNK
    n_chunks = qf_ref.shape[0] // c
    refs = ((qf_ref, kf_ref, vf_ref, gf_ref, gtf_ref, of_ref),
            (qb_ref, kb_ref, vb_ref, gb_ref, gtb_ref, ob_ref))

    @pl.when(step == 0)
    def _():
        s_scr[...] = s0_ref[...]

    ri = lax.broadcasted_iota(jnp.int32, (c, c), 0)
    ci = lax.broadcasted_iota(jnp.int32, (c, c), 1)
    lane = lax.broadcasted_iota(jnp.int32, (c, LANES), 1)
    eye = (ri == ci).astype(F32)
    incl = (ri >= ci, ri <= ci)
    strict = (ri > ci, ri < ci)

    def pick(tile, idx):
        return jnp.sum(jnp.where(lane == idx, tile, 0.0), axis=1, keepdims=True)

    sub = lax.broadcasted_iota(jnp.int32, (DN_HEADS, qf_ref.shape[0]), 0)
    gc_rows = {(hh, dr): jnp.sum(jnp.where(sub == hg * SCAN_HEADS + hh,
                                           refs[dr][4][dr * DN_HEADS:(dr + 1) * DN_HEADS, :], 0.0),
                                 axis=0, keepdims=True)
               for hh in range(SCAN_HEADS) for dr in range(2)}

    items = [(hh, dr, cc) for hh in range(SCAN_HEADS) for dr in range(2) for cc in range(n_chunks)]
    st = []
    for hh, dr, cc in items:
        q_ref, k_ref, v_ref, g_ref, gt_ref, _ = refs[dr]
        rs = slice(cc * c, (cc + 1) * c)
        ls = slice(hh * LANES, (hh + 1) * LANES)
        head = hg * SCAN_HEADS + hh
        gtile = g_ref[rs, :]
        gc = pick(gtile, dr * DN_HEADS + head)
        beta = pick(gtile, (2 + dr) * DN_HEADS + head)
        gc_row = gc_rows[(hh, dr)][:, rs]
        edge = c - 1 if dr == 0 else 0
        g_last = gc_row[:, edge:edge + 1]
        q = q_ref[rs, ls].astype(F32)
        k = k_ref[rs, ls].astype(F32)
        v = v_ref[rs, ls].astype(F32)
        dec = jnp.where(incl[dr], jnp.exp(jnp.where(incl[dr], gc - gc_row, 0.0)), 0.0)
        egc = jnp.exp(gc)
        kbeta = k * beta
        st.append(dict(dr=dr, rs=rs, ls=ls, q=q, k=k, dec=dec, kbeta=kbeta,
                       rhs=jnp.concatenate([v * beta, kbeta * egc], axis=1).astype(BF16),
                       k_dec=(k * jnp.exp(g_last - gc)).astype(BF16),
                       q_dec=(q * egc).astype(BF16),
                       e_last=jnp.exp(g_last)))
    for s in st:
        s["tm"] = jnp.where(strict[s["dr"]], _dot_nt(s["kbeta"], s["k"]) * s["dec"], 0.0)
    for s in st:
        s["attn"] = (_dot_nt(s["q"], s["k"]) * s["dec"]).astype(BF16)

    m8 = (ri // 8) == (ci // 8)
    pw = [-jnp.where(m8, s["tm"], 0.0) for s in st]
    p2 = [_bdot(p, p) for p in pw]
    p4 = [_bdot(p, p) for p in p2]
    xs = [eye + p for p in pw]
    xs = [x + _bdot(p, x) for x, p in zip(xs, p2)]
    xs = [x + _bdot(p, x) for x, p in zip(xs, p4)]
    blk = 8
    while blk < c:
        off = ((ri // (2 * blk)) == (ci // (2 * blk))) & ((ri // blk) != (ci // blk))
        lx = [_bdot(jnp.where(off, s["tm"], 0.0), x) for s, x in zip(st, xs)]
        xs = [x - _bdot(x, y) for x, y in zip(xs, lx)]
        blk *= 2
    uw = [_bdot(x, s["rhs"]) for s, x in zip(st, xs)]

    by_key = {it: (s, y) for it, s, y in zip(items, st, uw)}
    chains = [(hh, dr) for hh in range(SCAN_HEADS) for dr in range(2)]
    state = {ch: s_scr[ch[0], ch[1]] for ch in chains}
    for i in range(n_chunks):
        cur = {ch: by_key[(ch[0], ch[1], i if ch[1] == 0 else n_chunks - 1 - i)] for ch in chains}
        wsqs = {ch: _bdot(jnp.concatenate([cur[ch][1][:, LANES:].astype(BF16), cur[ch][0]["q_dec"]], axis=0),
                          state[ch]) for ch in chains}
        v_new = {ch: cur[ch][1][:, :LANES] - wsqs[ch][:c] for ch in chains}
        for ch in chains:
            s = cur[ch][0]
            refs[ch[1]][5][s["rs"], s["ls"]] = wsqs[ch][c:] + _bdot(s["attn"], v_new[ch])
        state = {ch: state[ch] * cur[ch][0]["e_last"] + _dot_tn(cur[ch][0]["k_dec"], v_new[ch])
                 for ch in chains}
    for ch in chains:
        s_scr[ch[0], ch[1]] = state[ch]

    @pl.when(step == n_steps - 1)
    def _():
        sfin_ref[...] = s_scr[...]


def _dn_scan(qn, kn, vv, gb, gbt, s0, batch, seq):
    rows, d = qn.shape
    tt = SCAN_TILE
    n_steps = seq // tt
    n_groups = DN_HEADS // SCAN_HEADS
    w = SCAN_HEADS * LANES
    fwd_t = lambda b, s: b * n_steps + s
    bwd_t = lambda b, s: b * n_steps + n_steps - 1 - s

    def specs(tile):
        wide = pl.BlockSpec((tt, w), lambda b, g, s: (tile(b, s), g))
        return wide, [wide, wide, wide,
                      pl.BlockSpec((tt, LANES), lambda b, g, s: (tile(b, s), 0)),
                      pl.BlockSpec((LANES, tt), lambda b, g, s: (0, tile(b, s)))]

    out_f, in_f = specs(fwd_t)
    out_b, in_b = specs(bwd_t)
    st_spec = pl.BlockSpec((None, SCAN_HEADS, 2, LANES, LANES), lambda b, g, s: (b, g, 0, 0, 0))
    return pl.pallas_call(
        functools.partial(_scan_kernel, n_steps),
        grid=(batch, n_groups, n_steps),
        in_specs=in_f + in_b + [st_spec],
        out_specs=[out_f, out_b, st_spec],
        out_shape=[jax.ShapeDtypeStruct((rows, d), F32), jax.ShapeDtypeStruct((rows, d), F32),
                   jax.ShapeDtypeStruct((batch, DN_HEADS, 2, LANES, LANES), F32)],
        scratch_shapes=[pltpu.VMEM((SCAN_HEADS, 2, LANES, LANES), F32)],
        compiler_params=_cparams(("arbitrary", "arbitrary", "arbitrary")),
        name="dn_scan",
    )(qn, kn, vv, gb, gbt, qn, kn, vv, gb, gbt, s0)


def _na_kernel(n_rows, q_ref, k_ref, v_ref, kc_ref, vc_ref, bias_ref, o_ref):
    t = pl.program_id(2)
    kw = NA_KROWS * GRID_W
    ks = jnp.clip(t * NA_QROWS - NA_WIN_R // 2, 0, n_rows - NA_KROWS)
    start = pl.multiple_of(ks * GRID_W, GRID_W)
    q = q_ref[...] * (NA_DH ** -0.5)
    kwin = k_ref[pl.ds(start, kw), :]
    vwin = v_ref[pl.ds(start, kw), :]
    kc = kc_ref[...]
    vc = vc_ref[...]
    lane = lax.broadcasted_iota(jnp.int32, q.shape, 1)
    outs = []
    for hh in range(2):
        sel = (lane < NA_DH) if hh == 0 else (lane >= NA_DH)
        qh = jnp.where(sel, q, jnp.zeros_like(q))
        s_loc = lax.dot_general(qh, kwin, (((1,), (1,)), ((), ())),
                                preferred_element_type=F32) + bias_ref[hh]
        s_ctx = lax.dot_general(qh, kc, (((1,), (1,)), ((), ())), preferred_element_type=F32)
        m = jnp.maximum(jnp.max(s_loc, axis=1, keepdims=True), jnp.max(s_ctx, axis=1, keepdims=True))
        p_loc = jnp.exp(s_loc - m)
        p_ctx = jnp.exp(s_ctx - m)
        denom = jnp.sum(p_loc, axis=1, keepdims=True) + jnp.sum(p_ctx, axis=1, keepdims=True)
        o = (jnp.dot(p_loc.astype(BF16), vwin, preferred_element_type=F32)
             + jnp.dot(p_ctx.astype(BF16), vc, preferred_element_type=F32)) / denom
        outs.append(o)
    o_ref[...] = jnp.where(lane < NA_DH, outs[0], outs[1]).astype(BF16)


def _na_attention(p, pc, bias, batch, seq, ctx_len, q_col, k_col, v_col):
    rows = p.shape[0]
    n_rows = seq // GRID_W
    qt = NA_QROWS * GRID_W
    n_tiles = n_rows // NA_QROWS
    kw = NA_KROWS * GRID_W
    n_pairs = NA_HEADS // 2

    def geom(t):
        return jnp.where(t == 0, 0, jnp.where(t == n_tiles - 1, 2, 1))

    return pl.pallas_call(
        functools.partial(_na_kernel, n_rows),
        grid=(batch, n_pairs, n_tiles),
        in_specs=[pl.BlockSpec((qt, LANES), lambda b, pr, t: (b * n_tiles + t, q_col + pr)),
                  pl.BlockSpec((seq, LANES), lambda b, pr, t: (b, k_col + pr)),
                  pl.BlockSpec((seq, LANES), lambda b, pr, t: (b, v_col + pr)),
                  pl.BlockSpec((ctx_len, LANES), lambda b, pr, t: (b, k_col + pr)),
                  pl.BlockSpec((ctx_len, LANES), lambda b, pr, t: (b, v_col + pr)),
                  pl.BlockSpec((None, 2, qt, kw), lambda b, pr, t: (geom(t), pr, 0, 0))],
        out_specs=pl.BlockSpec((qt, LANES), lambda b, pr, t: (b * n_tiles + t, pr)),
        out_shape=jax.ShapeDtypeStruct((rows, n_pairs * LANES), BF16),
        compiler_params=_cparams(("arbitrary", "arbitrary", "arbitrary")),
        name="na_attention",
    )(p, p, p, pc, pc, bias)


def _ctx_attn_kernel(q_ref, k_ref, v_ref, o_ref):
    q = q_ref[...] * (NA_DH ** -0.5)
    k = k_ref[...]
    v = v_ref[...]
    lane = lax.broadcasted_iota(jnp.int32, q.shape, 1)
    outs = []
    for hh in range(2):
        sel = (lane < NA_DH) if hh == 0 else (lane >= NA_DH)
        qh = jnp.where(sel, q, jnp.zeros_like(q))
        s = lax.dot_general(qh, k, (((1,), (1,)), ((), ())), preferred_element_type=F32)
        pm = jnp.exp(s - jnp.max(s, axis=1, keepdims=True))
        outs.append(jnp.dot(pm.astype(BF16), v, preferred_element_type=F32)
                    / jnp.sum(pm, axis=1, keepdims=True))
    o_ref[...] = jnp.where(lane < NA_DH, outs[0], outs[1]).astype(BF16)


def _ctx_attention(pc, batch, ctx_len, q_col, k_col, v_col):
    n_pairs = NA_HEADS // 2
    return pl.pallas_call(
        _ctx_attn_kernel,
        grid=(batch, n_pairs),
        in_specs=[pl.BlockSpec((ctx_len, LANES), lambda b, pr: (b, q_col + pr)),
                  pl.BlockSpec((ctx_len, LANES), lambda b, pr: (b, k_col + pr)),
                  pl.BlockSpec((ctx_len, LANES), lambda b, pr: (b, v_col + pr))],
        out_specs=pl.BlockSpec((ctx_len, LANES), lambda b, pr: (b, pr)),
        out_shape=jax.ShapeDtypeStruct((pc.shape[0], n_pairs * LANES), BF16),
        compiler_params=_cparams(("arbitrary", "arbitrary")),
        name="ctx_attention",
    )(pc, pc, pc)


def _na_bias_tables(rpb, n_rows):
    n_tiles = n_rows // NA_QROWS
    n_roff = 2 * NA_WIN_R - 1
    n_coff = 2 * NA_WIN_W - 1
    col = np.arange(GRID_W)
    c0 = np.clip(col - NA_WIN_W // 2, 0, GRID_W - NA_WIN_W)
    col_in = (col[None, :] >= c0[:, None]) & (col[None, :] < c0[:, None] + NA_WIN_W)
    coff = np.clip(col[None, :] - col[:, None] + (NA_WIN_W - 1), 0, n_coff - 1)
    col_sel = (coff[:, :, None] == np.arange(n_coff)).astype(np.float32)
    row_sel = np.zeros((3, NA_QROWS, NA_KROWS, n_roff), np.float32)
    mask = np.zeros((3, NA_QROWS, GRID_W, NA_KROWS, GRID_W), np.float32)
    for g, t in enumerate((0, 1, n_tiles - 1)):
        rs = t * NA_QROWS
        ks = min(max(rs - NA_WIN_R // 2, 0), n_rows - NA_KROWS)
        qrow = rs + np.arange(NA_QROWS)
        krow = ks + np.arange(NA_KROWS)
        r0 = np.clip(qrow - NA_WIN_R // 2, 0, n_rows - NA_WIN_R)
        row_in = (krow[None, :] >= r0[:, None]) & (krow[None, :] < r0[:, None] + NA_WIN_R)
        roff = np.clip(krow[None, :] - qrow[:, None] + (NA_WIN_R - 1), 0, n_roff - 1)
        row_sel[g] = (roff[:, :, None] == np.arange(n_roff)) & row_in[:, :, None]
        ok = row_in[:, None, :, None] & col_in[None, :, None, :]
        mask[g] = np.where(ok, 0.0, -np.inf)
    hp = lax.Precision.HIGHEST
    by_col = jnp.einsum("hrc,qkc->hrqk", rpb, jnp.asarray(col_sel), precision=hp)
    tab = jnp.einsum("gair,hrqk->ghaqik", jnp.asarray(row_sel), by_col, precision=hp)
    tab = tab + jnp.asarray(mask)[:, None]
    return tab.reshape(3, rpb.shape[0], NA_QROWS * GRID_W, NA_KROWS * GRID_W)


def _merge_kernel(of_ref, ob_ref, z_ref, na_ref, gd_ref, gn_ref, x_ref, dnw_ref, wpa_ref, wpb_ref,
                  wout_ref, gpost_ref, g1_ref, o_ref, dn_scr):
    o = of_ref[...] + ob_ref[...]
    z = z_ref[...].astype(F32)
    for hh in range(DN_HEADS):
        sl = slice(hh * LANES, (hh + 1) * LANES)
        oh = o[:, sl]
        oh = oh * lax.rsqrt(jnp.mean(oh * oh, axis=-1, keepdims=True) + EPS) * dnw_ref[...]
        zh = z[:, sl]
        dn_scr[:, sl] = (oh * (zh * jax.nn.sigmoid(zh))).astype(BF16)
    y = (jax.nn.sigmoid(gd_ref[...].astype(F32)) * jnp.dot(dn_scr[...], wpa_ref[...], preferred_element_type=F32)
         + jax.nn.sigmoid(gn_ref[...].astype(F32)) * _bdot(na_ref[...], wpb_ref[...]))
    out = _bdot(y, wout_ref[...])
    o_ref[...] = x_ref[...] + g1_ref[...] * _rms(out, gpost_ref[...])


def _merge(o_f, o_b, p, na_o, x2, dn_norm, w_pa, w_pb, w_out, gpost, mod, mod_row, g1_blk, tm):
    rows, d = x2.shape
    nw = na_o.shape[1]
    row_blk = lambda c: pl.BlockSpec((tm, d), lambda i: (i, c))
    const = lambda shape: pl.BlockSpec(shape, lambda i: (0,) * len(shape))
    return pl.pallas_call(
        _merge_kernel,
        grid=(rows // tm,),
        in_specs=[row_blk(0), row_blk(0), row_blk(3), pl.BlockSpec((tm, nw), lambda i: (i, 0)),
                  row_blk(4), row_blk(5), row_blk(0),
                  const((1, LANES)), const((d, d)), const((nw, d)), const((d, d)), const((1, d)),
                  pl.BlockSpec((None, 1, d), lambda i: (mod_row(i), 0, g1_blk))],
        out_specs=row_blk(0),
        out_shape=jax.ShapeDtypeStruct((rows, d), F32),
        scratch_shapes=[pltpu.VMEM((tm, d), BF16)],
        compiler_params=_cparams(("arbitrary",)),
        name="merge",
    )(o_f, o_b, p, na_o, p, p, x2, dn_norm, w_pa, w_pb, w_out, gpost, mod)


def _ffn_kernel(n_f, x_ref, gpre_ref, sc_ref, sh_ref, w1_ref, w3_ref, w2_ref, gpost_ref, g2_ref,
                o_ref, h_scr, acc_scr):
    j = pl.program_id(1)

    @pl.when(j == 0)
    def _():
        h = _rms(x_ref[...], gpre_ref[...]) * (1.0 + sc_ref[...]) + sh_ref[...]
        h_scr[...] = h.astype(BF16)
        acc_scr[...] = jnp.zeros_like(acc_scr)

    h = h_scr[...]
    a = jnp.dot(h, w1_ref[...], preferred_element_type=F32)
    b = jnp.dot(h, w3_ref[...], preferred_element_type=F32)
    acc_scr[...] += _bdot(a * jax.nn.sigmoid(a) * b, w2_ref[...])

    @pl.when(j == n_f - 1)
    def _():
        o_ref[...] = x_ref[...] + g2_ref[...] * _rms(acc_scr[...], gpost_ref[...])


def _dense_ffn(x2, gpre, gpost, mod, mod_row, w1, w3, w2, tm, tf):
    rows, d = x2.shape
    f = w1.shape[1]
    n_f = f // tf
    modspec = lambda blk: pl.BlockSpec((None, 1, d), lambda i, j: (mod_row(i), 0, blk))
    return pl.pallas_call(
        functools.partial(_ffn_kernel, n_f),
        grid=(rows // tm, n_f),
        in_specs=[pl.BlockSpec((tm, d), lambda i, j: (i, 0)),
                  pl.BlockSpec((1, d), lambda i, j: (0, 0)),
                  modspec(4), modspec(3),
                  pl.BlockSpec((d, tf), lambda i, j: (0, j)),
                  pl.BlockSpec((d, tf), lambda i, j: (0, j)),
                  pl.BlockSpec((tf, d), lambda i, j: (j, 0)),
                  pl.BlockSpec((1, d), lambda i, j: (0, 0)),
                  modspec(5)],
        out_specs=pl.BlockSpec((tm, d), lambda i, j: (i, 0)),
        out_shape=jax.ShapeDtypeStruct((rows, d), F32),
        scratch_shapes=[pltpu.VMEM((tm, d), BF16), pltpu.VMEM((tm, d), F32)],
        compiler_params=_cparams(("arbitrary", "arbitrary")),
        name="dense_ffn",
    )(x2, gpre, mod, mod, w1, w3, w2, gpost, mod)


def _router_kernel(x_ref, gpre_ref, sc_ref, sh_ref, r_ref, h_ref, gate_ref):
    h = _rms(x_ref[...], gpre_ref[...]) * (1.0 + sc_ref[...]) + sh_ref[...]
    h_ref[...] = h
    logits = jnp.dot(h, r_ref[...], precision=lax.Precision.HIGHEST, preferred_element_type=F32)
    lane = lax.broadcasted_iota(jnp.int32, logits.shape, 1)
    neg = -jnp.inf
    l1 = jnp.where(lane < N_EXPERTS, logits, neg)
    m1 = jnp.max(l1, axis=1, keepdims=True)
    i1 = jnp.min(jnp.where(l1 == m1, lane, LANES), axis=1, keepdims=True)
    l2 = jnp.where(lane == i1, neg, l1)
    m2 = jnp.max(l2, axis=1, keepdims=True)
    i2 = jnp.min(jnp.where(l2 == m2, lane, LANES), axis=1, keepdims=True)
    e = jnp.exp(m2 - m1)
    w1 = 1.0 / (1.0 + e)
    w2 = e / (1.0 + e)
    out = jnp.where(lane == 0, i1.astype(F32), 0.0)
    out = jnp.where(lane == 1, i2.astype(F32), out)
    out = jnp.where(lane == 2, w1, out)
    out = jnp.where(lane == 3, w2, out)
    gate_ref[...] = out


def _router(x2, gpre, mod, mod_row, router_pad, tm):
    rows, d = x2.shape
    modspec = lambda blk: pl.BlockSpec((None, 1, d), lambda i: (mod_row(i), 0, blk))
    return pl.pallas_call(
        _router_kernel,
        grid=(rows // tm,),
        in_specs=[pl.BlockSpec((tm, d), lambda i: (i, 0)),
                  pl.BlockSpec((1, d), lambda i: (0, 0)),
                  modspec(4), modspec(3),
                  pl.BlockSpec((d, LANES), lambda i: (0, 0))],
        out_specs=[pl.BlockSpec((tm, d), lambda i: (i, 0)),
                   pl.BlockSpec((tm, LANES), lambda i: (i, 0))],
        out_shape=[jax.ShapeDtypeStruct((rows, d), F32),
                   jax.ShapeDtypeStruct((rows, LANES), F32)],
        compiler_params=_cparams(("arbitrary",)),
        name="moe_router",
    )(x2, gpre, mod, mod, router_pad)


def _gather_row(h_hbm, xbuf, sem, slot, r, tok):
    return pltpu.make_async_copy(h_hbm.at[pl.ds(tok, 1), :], xbuf.at[slot, pl.ds(r, 1), :], sem.at[slot])


def _scatter_row(stage, out_hbm, sem, slot, r, dst):
    return pltpu.make_async_copy(stage.at[slot, pl.ds(r, 1), :], out_hbm.at[pl.ds(dst, 1), :], sem.at[slot])


def _expert_kernel(n_f, n_tiles, te_ref, nv_ref, tok_ref, tokn_ref, dstp_ref, dst_ref, h_hbm,
                   w1_ref, w3_ref, w2_ref, out_hbm, xbuf, xb16, acc_scr, stage, gsem, ssem):
    i = pl.program_id(0)
    j = pl.program_id(1)
    slot = i % 2
    rows_per_step = MOE_TM // n_f

    def wait_gathers(sl):
        pltpu.make_async_copy(h_hbm.at[pl.ds(0, MOE_TM), :], xbuf.at[sl], gsem.at[sl]).wait()

    def wait_scatters(sl):
        pltpu.make_async_copy(stage.at[sl], out_hbm.at[pl.ds(0, MOE_TM), :], ssem.at[sl]).wait()

    @pl.when((i == 0) & (j == 0))
    def _():
        stage[...] = jnp.zeros_like(stage)

        def body(r, carry):
            _gather_row(h_hbm, xbuf, gsem, slot, r, tok_ref[0, r]).start()
            return carry
        lax.fori_loop(0, MOE_TM, body, 0, unroll=8)

    @pl.when(j == 0)
    def _():
        wait_gathers(slot)
        xb16[...] = xbuf[slot].astype(BF16)
        acc_scr[...] = jnp.zeros_like(acc_scr)

    def row_dmas():
        for k in range(rows_per_step):
            r = pl.multiple_of(j * rows_per_step, SUBLANES) + k
            _gather_row(h_hbm, xbuf, gsem, 1 - slot, r, tokn_ref[0, r]).start()
            _scatter_row(stage, out_hbm, ssem, 1 - slot, r, dstp_ref[0, r]).start()

    valid = i < nv_ref[0]

    @pl.when(valid)
    def _():
        row_dmas()
        x = xb16[...]
        a = jnp.dot(x, w1_ref[...], preferred_element_type=F32)
        b = jnp.dot(x, w3_ref[...], preferred_element_type=F32)
        acc_scr[...] += _bdot(a * jax.nn.sigmoid(a) * b, w2_ref[...])

    @pl.when(jnp.logical_not(valid))
    def _():
        row_dmas()

    @pl.when(j == n_f - 1)
    def _():
        @pl.when(i >= 1)
        def _():
            wait_scatters(slot)

        stage[slot] = acc_scr[...]

        @pl.when(i == n_tiles - 1)
        def _():
            wait_scatters(1 - slot)
            wait_gathers(1 - slot)

            def body(r, carry):
                _scatter_row(stage, out_hbm, ssem, slot, r, dst_ref[0, r]).start()
                return carry
            lax.fori_loop(0, MOE_TM, body, 0, unroll=8)
            wait_scatters(slot)


def _expert_ffn(h, row_token, out_row, tile_expert, n_valid, w1, w3, w2, tf):
    d = h.shape[1]
    prow = row_token.shape[0]
    f = w1.shape[2]
    n_f = f // tf
    n_tiles = prow // MOE_TM
    assert f % tf == 0 and MOE_TM % (n_f * SUBLANES) == 0
    idx_spec =lambda fn: pl.BlockSpec((None, 1, MOE_TM), lambda i, j, te, nv: (fn(i), 0, 0),
                                       memory_space=pltpu.SMEM)
    tok3 = row_token.reshape(n_tiles, 1, MOE_TM)
    dst3 = jnp.concatenate([prow + jnp.arange(MOE_TM, dtype=jnp.int32), out_row]).reshape(n_tiles + 1, 1, MOE_TM)
    grid_spec = pltpu.PrefetchScalarGridSpec(
        num_scalar_prefetch=2,
        grid=(n_tiles, n_f),
        in_specs=[idx_spec(lambda i: i),
                  idx_spec(lambda i: jnp.minimum(i + 1, n_tiles - 1)),
                  idx_spec(lambda i: i),
                  idx_spec(lambda i: i + 1),
                  pl.BlockSpec(memory_space=pl.ANY),
                  pl.BlockSpec((None, d, tf), lambda i, j, te, nv: (te[i], 0, j)),
                  pl.BlockSpec((None, d, tf), lambda i, j, te, nv: (te[i], 0, j)),
                  pl.BlockSpec((None, tf, d), lambda i, j, te, nv: (te[i], j, 0))],
        out_specs=pl.BlockSpec(memory_space=pl.ANY),
        scratch_shapes=[pltpu.VMEM((2, MOE_TM, d), F32), pltpu.VMEM((MOE_TM, d), BF16),
                        pltpu.VMEM((MOE_TM, d), F32), pltpu.VMEM((2, MOE_TM, d), F32),
                        pltpu.SemaphoreType.DMA((2,)), pltpu.SemaphoreType.DMA((2,))],
    )
    return pl.pallas_call(
        functools.partial(_expert_kernel, n_f, n_tiles),
        grid_spec=grid_spec,
        out_shape=jax.ShapeDtypeStruct((prow + MOE_TM, d), F32),
        compiler_params=_cparams(("arbitrary", "arbitrary")),
        name="moe_experts",
    )(tile_expert, n_valid, tok3, tok3, dst3, dst3, h, w1, w3, w2)


def _combine_kernel(y1_ref, y2_ref, gate_ref, x_ref, gpost_ref, g2_ref, o_ref):
    gt = gate_ref[...]
    lane = lax.broadcasted_iota(jnp.int32, gt.shape, 1)
    w1 = jnp.sum(jnp.where(lane == 2, gt, 0.0), axis=1, keepdims=True)
    w2 = jnp.sum(jnp.where(lane == 3, gt, 0.0), axis=1, keepdims=True)
    y = w1 * y1_ref[...] + w2 * y2_ref[...]
    o_ref[...] = x_ref[...] + g2_ref[...] * _rms(y, gpost_ref[...])


def _moe_combine(ys, gates, x2, gpost, mod, mod_row, tm):
    rows, d = x2.shape
    rb = pl.BlockSpec((tm, d), lambda i: (i, 0))
    return pl.pallas_call(
        _combine_kernel,
        grid=(rows // tm,),
        in_specs=[rb, pl.BlockSpec((tm, d), lambda i: (rows // tm + i, 0)),
                  pl.BlockSpec((tm, LANES), lambda i: (i, 0)), rb,
                  pl.BlockSpec((1, d), lambda i: (0, 0)),
                  pl.BlockSpec((None, 1, d), lambda i: (mod_row(i), 0, 5))],
        out_specs=rb,
        out_shape=jax.ShapeDtypeStruct((rows, d), F32),
        compiler_params=_cparams(("arbitrary",)),
        name="moe_combine",
    )(ys, ys, gates, x2, gpost, mod)


def _moe_schedule(gates, n_tokens):
    idx = gates[:, 0:2].astype(jnp.int32)
    flat_e = idx.reshape(-1)
    onehot = (flat_e[:, None] == jnp.arange(N_EXPERTS)[None, :]).astype(jnp.int32)
    csum = jnp.cumsum(onehot, axis=0)
    counts = csum[-1]
    rank = jnp.sum(csum * onehot, axis=1) - 1
    padded = ((counts + MOE_TM - 1) // MOE_TM) * MOE_TM
    ends = jnp.cumsum(padded)
    starts = ends - padded
    dest = jnp.sum(starts[None, :] * onehot, axis=1) + rank
    n_pairs = 2 * n_tokens
    n_rows = n_pairs + N_EXPERTS * MOE_TM
    n_tiles = n_rows // MOE_TM
    pair_of_row = jnp.full((n_rows,), -1, jnp.int32).at[dest].set(jnp.arange(n_pairs, dtype=jnp.int32))
    is_pad = pair_of_row < 0
    row_token = jnp.where(is_pad, 0, pair_of_row // 2)
    out_row = jnp.where(is_pad, n_pairs - 1 + jnp.cumsum(is_pad.astype(jnp.int32)),
                        (pair_of_row % 2) * n_tokens + pair_of_row // 2)
    tile_start = jnp.arange(n_tiles, dtype=jnp.int32) * MOE_TM
    tile_expert = jnp.minimum(jnp.sum((ends[None, :] <= tile_start[:, None]).astype(jnp.int32), axis=1),
                              N_EXPERTS - 1)
    n_valid = (ends[-1] // MOE_TM).astype(jnp.int32).reshape(1)
    return row_token, out_row, tile_expert, n_valid


def _rope_tables(seq):
    t = jnp.arange(seq)
    row = (t // GRID_W).astype(F32)
    col = (t % GRID_W).astype(F32)
    n_freq = DN_DK // 4
    inv = ROPE_BASE ** (-jnp.arange(n_freq, dtype=F32) / n_freq)
    ang = jnp.concatenate([row[:, None] * inv, col[:, None] * inv], axis=-1)
    cos, sin = jnp.cos(ang), jnp.sin(ang)
    return jnp.concatenate([cos, cos], axis=-1), jnp.concatenate([-sin, sin], axis=-1)


def kernel(x, c, ctx, c_ctx, ada_w, ada_b, norm_mix_pre, norm_mix_post, norm_ffn_pre, norm_ffn_post,
           w_in, dn_conv, dn_a_log, dn_dt_bias, dn_norm, na_rpb, w_branch_dn, w_branch_na, w_out,
           ffn_w1, ffn_w3, ffn_w2, moe_router, moe_w1, moe_w3, moe_w2):
    batch, seq, d = x.shape
    ctx_len = ctx.shape[1]
    depth = w_in.shape[0]
    nh = DN_HEADS
    dn_w = nh * DN_DK
    na_w = NA_HEADS * NA_DH
    n_rows = seq // GRID_W
    assert d == dn_w and seq % SCAN_TILE == 0 and ctx_len % SCAN_TILE == 0 and n_rows % NA_QROWS == 0
    assert depth <= 2, "context tokens only take the dense FFN path"

    c_rows = jnp.zeros((SUBLANES, d), F32).at[:batch].set(c).at[batch].set(c_ctx)
    mod_all = _mod_vectors(c_rows, ada_w, ada_b)
    cos2, sin2 = _rope_tables(seq)
    ones_t = jnp.ones((SCAN_TILE, LANES), F32)

    x2 = x.reshape(batch * seq, d)
    xc2 = ctx.reshape(batch * ctx_len, d)
    lat_tm = 1024
    lat_row = lambda tm: (lambda i: i // (seq // tm))
    ctx_row = lambda i: batch

    q_col, k_col, v_col = (4 * dn_w + 2 * d) // LANES, (4 * dn_w + 2 * d + na_w) // LANES, \
        (4 * dn_w + 2 * d + 2 * na_w) // LANES

    for l in range(depth):
        last = l == depth - 1
        mod = mod_all[l].reshape(SUBLANES, 1, 6 * d)
        wl = w_in[l]
        o_ab = 4 * dn_w
        o_na = o_ab + 4 * nh
        o_gate = o_na + 3 * na_w
        w_main = jnp.concatenate([wl[:, :o_ab], wl[:, o_gate:], wl[:, o_na:o_gate]], axis=1).astype(BF16)
        wab = jnp.pad(wl[:, o_ab:o_na], ((0, 0), (0, LANES - 4 * nh)))
        wab_hi = wab.astype(BF16)
        wab_lo = (wab - wab_hi.astype(F32)).astype(BF16)
        gpre = norm_mix_pre[l].reshape(1, d)
        gpost = norm_mix_post[l].reshape(1, d)

        p, ab = _in_proj(x2, gpre, mod, lat_row(lat_tm), 1, 0, w_main, wab_hi, wab_lo, lat_tm, INPROJ_TN)
        pc, abc = _in_proj(xc2, gpre, mod, ctx_row, 1, 0, w_main, wab_hi, wab_lo, batch * ctx_len, INPROJ_TN)

        conv_w3 = jnp.pad(dn_conv[l].T.reshape(DN_CONV, 3, dn_w).transpose(1, 0, 2),
                          ((0, 0), (0, SUBLANES - DN_CONV), (0, 0)))
        gpar = jnp.zeros((SUBLANES, LANES), F32)
        gpar = gpar.at[0, :2 * nh].set(-jnp.exp(dn_a_log[l].reshape(-1)))
        gpar = gpar.at[1, :2 * nh].set(dn_dt_bias[l].reshape(-1))

        qc_, kc_, vc_, gbc, gbtc = _dn_prep(pc, abc, conv_w3, gpar, ones_t, ones_t, batch, ctx_len, False)
        ql_, kl_, vl_, gbl, gbtl = _dn_prep(p, ab, conv_w3, gpar, cos2, sin2, batch, seq, True)
        s0 = jnp.zeros((batch, nh, 2, LANES, LANES), F32)
        oc_f, oc_b, s_ctx = _dn_scan(qc_, kc_, vc_, gbc, gbtc, s0, batch, ctx_len)
        ol_f, ol_b, _ = _dn_scan(ql_, kl_, vl_, gbl, gbtl, s_ctx, batch, seq)

        bias = _na_bias_tables(na_rpb[l], n_rows)
        na_lat = _na_attention(p, pc, bias, batch, seq, ctx_len, q_col, k_col, v_col)

        dnw = dn_norm[l].reshape(1, LANES)
        w_pa = w_branch_dn[l].astype(BF16)
        w_pb = w_branch_na[l].astype(BF16)
        w_o = w_out[l].astype(BF16)
        x2 = _merge(ol_f, ol_b, p, na_lat, x2, dnw, w_pa, w_pb, w_o, gpost, mod, lat_row(512), 2, 512)

        gfpre = norm_ffn_pre[l].reshape(1, d)
        gfpost = norm_ffn_post[l].reshape(1, d)
        if l % 2 == 0:
            w1 = ffn_w1[l // 2].astype(BF16)
            w3 = ffn_w3[l // 2].astype(BF16)
            w2 = ffn_w2[l // 2].astype(BF16)
            tf = w1.shape[1] // 2
            x2 = _dense_ffn(x2, gfpre, gfpost, mod, lat_row(512), w1, w3, w2, 512, tf)
        else:
            rpad = jnp.pad(moe_router[l // 2], ((0, 0), (0, LANES - N_EXPERTS)))
            hb, gates = _router(x2, gfpre, mod, lat_row(512), rpad, 512)
            n_tok = batch * seq
            row_token, out_row, tile_expert, n_valid = _moe_schedule(gates, n_tok)
            ys = _expert_ffn(hb, row_token, out_row, tile_expert, n_valid, moe_w1[l // 2].astype(BF16),
                             moe_w3[l // 2].astype(BF16), moe_w2[l // 2].astype(BF16), 896)
            x2 = _moe_combine(ys, gates, x2, gfpost, mod, lat_row(512), 512)

        if not last:
            na_ctx = _ctx_attention(pc, batch, ctx_len, q_col, k_col, v_col)
            xc2 = _merge(oc_f, oc_b, pc, na_ctx, xc2, dnw, w_pa, w_pb, w_o, gpost, mod, ctx_row, 2, 256)
            xc2 = _dense_ffn(xc2, gfpre, gfpost, mod, ctx_row, w1, w3, w2, 512, tf)
    return x2.reshape(batch, seq, d)
```

```python
import functools

import numpy as np
import jax
import jax.numpy as jnp
from jax import lax
from jax.experimental import pallas as pl
from jax.experimental.pallas import tpu as pltpu

F32 = jnp.float32
BF16 = jnp.bfloat16

GRID_W = 64
DN_HEADS = 8
DN_DK = 128
DN_CONV = 5
DN_CHUNK = 64
NA_HEADS = 8
NA_DH = 64
NA_WIN_R = 8
NA_WIN_W = 16
ROPE_BASE = 10000.0
N_EXPERTS = 8
EPS = 1e-6

LANES = 128
SUBLANES = 8
BF16_SUBLANES = 16
VMEM_LIMIT = 56 * 1024 * 1024

SCAN_TILE = 256
SCAN_HEADS = 8
NA_QROWS = 4
NA_KROWS = NA_QROWS + 8
MOE_TM = 512
INPROJ_TN = 1920


def _cparams(sem):
    return pltpu.CompilerParams(dimension_semantics=sem, vmem_limit_bytes=VMEM_LIMIT)


def _bdot(a, b):
    return jnp.dot(a.astype(BF16), b.astype(BF16), preferred_element_type=F32)


def _dot_nt(a, b):
    return lax.dot_general(a.astype(BF16), b.astype(BF16), (((1,), (1,)), ((), ())),
                           preferred_element_type=F32)


def _dot_tn(a, b):
    return lax.dot_general(a.astype(BF16), b.astype(BF16), (((0,), (0,)), ((), ())),
                           preferred_element_type=F32)


def _split3(x):
    hi = x.astype(BF16)
    r = x - hi.astype(F32)
    mid = r.astype(BF16)
    lo = (r - mid.astype(F32)).astype(BF16)
    return hi, mid, lo


def _rms(x, gain):
    return x * lax.rsqrt(jnp.mean(x * x, axis=-1, keepdims=True) + EPS) * gain


def _mod_kernel(c_ref, w_ref, b_ref, o_ref):
    c = c_ref[...]
    s = c * jax.nn.sigmoid(c)
    o_ref[0] = jnp.dot(s, w_ref[0], precision=lax.Precision.HIGHEST,
                       preferred_element_type=F32) + b_ref[0]


def _mod_vectors(c_rows, ada_w, ada_b):
    depth, d, n = ada_w.shape
    tn = 1536
    return pl.pallas_call(
        _mod_kernel,
        grid=(depth, n // tn),
        in_specs=[pl.BlockSpec((SUBLANES, d), lambda l, j: (0, 0)),
                  pl.BlockSpec((1, d, tn), lambda l, j: (l, 0, j)),
                  pl.BlockSpec((1, 1, tn), lambda l, j: (l, 0, j))],
        out_specs=pl.BlockSpec((1, SUBLANES, tn), lambda l, j: (l, 0, j)),
        out_shape=jax.ShapeDtypeStruct((depth, SUBLANES, n), F32),
        compiler_params=_cparams(("arbitrary", "arbitrary")),
        name="mod_vectors",
    )(c_rows, ada_w, ada_b.reshape(depth, 1, n))


def _inproj_kernel(x_ref, g_ref, sc_ref, sh_ref, w_ref, wab_hi_ref, wab_lo_ref,
                   o_ref, ab_ref, h_scr, hlo_scr):
    j = pl.program_id(1)

    @pl.when(j == 0)
    def _():
        h = _rms(x_ref[...], g_ref[...]) * (1.0 + sc_ref[...]) + sh_ref[...]
        hi = h.astype(BF16)
        lo = (h - hi.astype(F32)).astype(BF16)
        h_scr[...] = hi
        hlo_scr[...] = lo
        ab_ref[...] = (jnp.dot(hi, wab_hi_ref[...], preferred_element_type=F32)
                       + jnp.dot(lo, wab_hi_ref[...], preferred_element_type=F32)
                       + jnp.dot(hi, wab_lo_ref[...], preferred_element_type=F32))

    o_ref[...] = jnp.dot(h_scr[...], w_ref[...], preferred_element_type=F32).astype(BF16)


def _in_proj(x2, gain, mod, mod_row, sc_blk, sh_blk, w_main, wab_hi, wab_lo, tm, tn):
    rows, d = x2.shape
    n = w_main.shape[1]
    return pl.pallas_call(
        _inproj_kernel,
        grid=(rows // tm, n // tn),
        in_specs=[pl.BlockSpec((tm, d), lambda i, j: (i, 0)),
                  pl.BlockSpec((1, d), lambda i, j: (0, 0)),
                  pl.BlockSpec((None, 1, d), lambda i, j: (mod_row(i), 0, sc_blk)),
                  pl.BlockSpec((None, 1, d), lambda i, j: (mod_row(i), 0, sh_blk)),
                  pl.BlockSpec((d, tn), lambda i, j: (0, j)),
                  pl.BlockSpec((d, LANES), lambda i, j: (0, 0)),
                  pl.BlockSpec((d, LANES), lambda i, j: (0, 0))],
        out_specs=[pl.BlockSpec((tm, tn), lambda i, j: (i, j)),
                   pl.BlockSpec((tm, LANES), lambda i, j: (i, 0))],
        out_shape=[jax.ShapeDtypeStruct((rows, n), BF16),
                   jax.ShapeDtypeStruct((rows, LANES), F32)],
        scratch_shapes=[pltpu.VMEM((tm, d), BF16), pltpu.VMEM((tm, d), BF16)],
        compiler_params=_cparams(("arbitrary", "arbitrary")),
        name="in_proj",
    )(x2, gain, mod, mod, w_main, wab_hi, wab_lo)


def _prep_kernel(rope, n_tiles,
                 q_ref, qp_ref, qn_ref, k_ref, kp_ref, kn_ref, v_ref, vp_ref, vn_ref,
                 cw_ref, ab_ref, gpar_ref, cos_ref, sin_ref,
                 qo_ref, ko_ref, vo_ref, gb_ref, gbt_ref, xq_scr, xk_scr, xv_scr):
    t = pl.program_id(1)
    tt = q_ref.shape[0]
    first = t == 0
    last = t == n_tiles - 1
    pad = DN_CONV // 2
    halo = qp_ref.shape[0]

    for scr, m_ref, p_ref, n_ref in ((xq_scr, q_ref, qp_ref, qn_ref), (xk_scr, k_ref, kp_ref, kn_ref),
                                     (xv_scr, v_ref, vp_ref, vn_ref)):
        scr[0:halo, :] = jnp.where(first, 0.0, p_ref[...].astype(F32))
        scr[halo:halo + tt, :] = m_ref[...].astype(F32)
        scr[halo + tt:, :] = jnp.where(last, 0.0, n_ref[...].astype(F32))

    def conv_silu(scr, w, sl):
        acc = scr[halo - pad:halo - pad + tt, sl] * w[0:1]
        for i in range(1, DN_CONV):
            o = halo - pad + i
            acc = acc + scr[o:o + tt, sl] * w[i:i + 1]
        return acc * jax.nn.sigmoid(acc)

    def l2n(x):
        return x * lax.rsqrt(jnp.sum(x * x, axis=-1, keepdims=True) + EPS)

    def rot(x):
        if not rope:
            return x
        return x * cos_ref[...] + pltpu.roll(x, LANES // 2, 1) * sin_ref[...]

    for hh in range(DN_HEADS):
        sl = slice(hh * LANES, (hh + 1) * LANES)
        qo_ref[:, sl] = (rot(l2n(conv_silu(xq_scr, cw_ref[0, :, sl], sl))) * (DN_DK ** -0.5)).astype(BF16)
        ko_ref[:, sl] = rot(l2n(conv_silu(xk_scr, cw_ref[1, :, sl], sl))).astype(BF16)
        vo_ref[:, sl] = conv_silu(xv_scr, cw_ref[2, :, sl], sl).astype(BF16)

    ab = ab_ref[...]
    lane = lax.broadcasted_iota(jnp.int32, ab.shape, 1)
    row = lax.broadcasted_iota(jnp.int32, ab.shape, 0) % DN_CHUNK
    xg = ab + gpar_ref[1:2]
    sp = jnp.maximum(xg, 0.0) + jnp.log1p(jnp.exp(-jnp.abs(xg)))
    g = gpar_ref[0:1] * sp
    beta = jax.nn.sigmoid(ab)
    pre = g
    suf = g
    s = 1
    while s < DN_CHUNK:
        pre = pre + jnp.where(row >= s, pltpu.roll(pre, s, 0), 0.0)
        suf = suf + jnp.where(row < DN_CHUNK - s, pltpu.roll(suf, tt - s, 0), 0.0)
        s *= 2
    nh = DN_HEADS
    gb = jnp.where(lane < nh, pre, jnp.where(lane < 2 * nh, suf, beta))
    gb_ref[...] = gb
    er = lax.broadcasted_iota(jnp.int32, (LANES, 3 * LANES), 0)
    ec = lax.broadcasted_iota(jnp.int32, (LANES, 3 * LANES), 1)
    eye3 = ((ec % LANES) == er).astype(BF16)
    gbt_ref[...] = lax.dot_general(eye3, jnp.concatenate(_split3(gb), axis=1),
                                   (((1,), (1,)), ((), ())), preferred_element_type=F32)


def _dn_prep(p, ab, conv_w3, gpar, cos2, sin2, batch, seq, rope):
    rows = p.shape[0]
    tt = SCAN_TILE
    n_tiles = seq // tt
    halo = BF16_SUBLANES
    hb = tt // halo
    n_hblk = rows // halo
    d = DN_HEADS * LANES

    def main(cb):
        return pl.BlockSpec((tt, d), lambda b, t: (b * n_tiles + t, cb))

    def prev(cb):
        return pl.BlockSpec((halo, d), lambda b, t: (jnp.maximum((b * n_tiles + t) * hb - 1, 0), cb))

    def nxt(cb):
        return pl.BlockSpec((halo, d),
                            lambda b, t: (jnp.minimum((b * n_tiles + t + 1) * hb, n_hblk - 1), cb))

    in_specs = []
    for cb in range(3):
        in_specs += [main(cb), prev(cb), nxt(cb)]
    in_specs += [
        pl.BlockSpec((3, SUBLANES, d), lambda b, t: (0, 0, 0)),
        pl.BlockSpec((tt, LANES), lambda b, t: (b * n_tiles + t, 0)),
        pl.BlockSpec((SUBLANES, LANES), lambda b, t: (0, 0)),
        pl.BlockSpec((tt, LANES), lambda b, t: (t, 0)),
        pl.BlockSpec((tt, LANES), lambda b, t: (t, 0)),
    ]
    out_full = pl.BlockSpec((tt, d), lambda b, t: (b * n_tiles + t, 0))
    return pl.pallas_call(
        functools.partial(_prep_kernel, rope, n_tiles),
        grid=(batch, n_tiles),
        in_specs=in_specs,
        out_specs=[out_full, out_full, out_full,
                   pl.BlockSpec((tt, LANES), lambda b, t: (b * n_tiles + t, 0)),
                   pl.BlockSpec((LANES, tt), lambda b, t: (0, b * n_tiles + t))],
        out_shape=[jax.ShapeDtypeStruct((rows, d), BF16)] * 3
        + [jax.ShapeDtypeStruct((rows, LANES), F32), jax.ShapeDtypeStruct((LANES, rows), F32)],
        scratch_shapes=[pltpu.VMEM((tt + 2 * halo, d), F32)] * 3,
        compiler_params=_cparams(("arbitrary", "arbitrary")),
        name="dn_prep_rope" if rope else "dn_prep",
    )(p, p, p, p, p, p, p, p, p, conv_w3, ab, gpar, cos2, sin2)


def _scan_kernel(n_steps,
                 qf_ref, kf_ref, vf_ref, gf_ref, gtf_ref, qb_ref, kb_ref, vb_ref, gb_ref, gtb_ref, s0_ref,
                 of_ref, ob_ref, sfin_ref, s_scr):
    hg = pl.program_id(1)
    step = pl.program_id(2)
    c = DN_CHUNK
    n_chunks = qf_ref.shape[0] // c
    refs = ((qf_ref, kf_ref, vf_ref, gf_ref, gtf_ref, of_ref),
            (qb_ref, kb_ref, vb_ref, gb_ref, gtb_ref, ob_ref))

    @pl.when(step == 0)
    def _():
        s_scr[...] = s0_ref[...]

    ri = lax.broadcasted_iota(jnp.int32, (c, c), 0)
    ci = lax.broadcasted_iota(jnp.int32, (c, c), 1)
    lane = lax.broadcasted_iota(jnp.int32, (c, LANES), 1)
    eye = (ri == ci).astype(F32)
    incl = (ri >= ci, ri <= ci)
    strict = (ri > ci, ri < ci)

    def pick(tile, idx):
        return jnp.sum(jnp.where(lane == idx, tile, 0.0), axis=1, keepdims=True)

    sub = lax.broadcasted_iota(jnp.int32, (DN_HEADS, qf_ref.shape[0]), 0)
    gc_rows = {(hh, dr): jnp.sum(jnp.where(sub == hg * SCAN_HEADS + hh,
                                           refs[dr][4][dr * DN_HEADS:(dr + 1) * DN_HEADS, :], 0.0),
                                 axis=0, keepdims=True)
               for hh in range(SCAN_HEADS) for dr in range(2)}

    items = [(hh, dr, cc) for hh in range(SCAN_HEADS) for dr in range(2) for cc in range(n_chunks)]
    st = []
    for hh, dr, cc in items:
        q_ref, k_ref, v_ref, g_ref, gt_ref, _ = refs[dr]
        rs = slice(cc * c, (cc + 1) * c)
        ls = slice(hh * LANES, (hh + 1) * LANES)
        head = hg * SCAN_HEADS + hh
        gtile = g_ref[rs, :]
        gc = pick(gtile, dr * DN_HEADS + head)
        beta = pick(gtile, (2 + dr) * DN_HEADS + head)
        gc_row = gc_rows[(hh, dr)][:, rs]
        edge = c - 1 if dr == 0 else 0
        g_last = gc_row[:, edge:edge + 1]
        q = q_ref[rs, ls].astype(F32)
        k = k_ref[rs, ls].astype(F32)
        v = v_ref[rs, ls].astype(F32)
        egc = jnp.exp(gc)
        kbeta = k * beta
        st.append(dict(dr=dr, rs=rs, ls=ls, q=q, k=k, gc=gc, kbeta=kbeta,
                       rhs=jnp.concatenate([v * beta, kbeta * egc], axis=1).astype(BF16),
                       k_dec=(k * jnp.exp(g_last - gc)).astype(BF16),
                       q_dec=(q * egc).astype(BF16),
                       e_last=jnp.exp(g_last)))

    assert n_chunks % 2 == 0
    pr = lax.broadcasted_iota(jnp.int32, (c, 2 * c), 0)
    pl2 = lax.broadcasted_iota(jnp.int32, (c, 2 * c), 1)
    pc = pl2 % c
    left = pl2 < c
    eye2 = (pr == pc).astype(F32)
    incl2 = (pr >= pc, pr <= pc)
    strict2 = (pr > pc, pr < pc)
    zeros_k = jnp.zeros((c, LANES), F32)
    zeros_r = jnp.zeros((c, 2 * LANES), BF16)

    def blockdiag(bp):
        return jnp.concatenate([jnp.where(left, bp, 0.0), jnp.where(left, 0.0, bp)], axis=0)

    def pdot(ap, bp):
        return _bdot(ap, blockdiag(bp))

    pairs = [(st[2 * m], st[2 * m + 1], items[2 * m]) for m in range(len(st) // 2)]
    tms = []
    for s0, s1, (hh, dr, cc) in pairs:
        gcol = jnp.where(left, s0["gc"], s1["gc"])
        grow = gc_rows[(hh, dr)][:, cc * c:(cc + 2) * c]
        dec = jnp.where(incl2[dr], jnp.exp(jnp.where(incl2[dr], gcol - grow, 0.0)), 0.0)
        kk = jnp.concatenate([jnp.concatenate([s0["k"], zeros_k], axis=1),
                              jnp.concatenate([zeros_k, s1["k"]], axis=1)], axis=0)
        tms.append(jnp.where(strict2[dr],
                             _dot_nt(jnp.concatenate([s0["kbeta"], s1["kbeta"]], axis=1), kk) * dec, 0.0))
        s0["attn2"] = (_dot_nt(jnp.concatenate([s0["q"], s1["q"]], axis=1), kk) * dec).astype(BF16)

    m8 = (pr // 8) == (pc // 8)
    pw = [-jnp.where(m8, t, 0.0) for t in tms]
    p2 = [pdot(p, p) for p in pw]
    p4 = [pdot(p, p) for p in p2]
    xs = [eye2 + p for p in pw]
    xs = [x + pdot(p, x) for x, p in zip(xs, p2)]
    xs = [x + pdot(p, x) for x, p in zip(xs, p4)]
    blk = 8
    while blk < c:
        off = ((pr // (2 * blk)) == (pc // (2 * blk))) & ((pr // blk) != (pc // blk))
        lx = [pdot(jnp.where(off, t, 0.0), x) for t, x in zip(tms, xs)]
        xs = [x - pdot(x, y) for x, y in zip(xs, lx)]
        blk *= 2
    uw = []
    for (s0, s1, _), x in zip(pairs, xs):
        both = _bdot(x, jnp.concatenate([jnp.concatenate([s0["rhs"], zeros_r], axis=1),
                                         jnp.concatenate([zeros_r, s1["rhs"]], axis=1)], axis=0))
        uw += [both[:, :2 * LANES], both[:, 2 * LANES:]]
        s0["attn"] = s0["attn2"][:, :c]
        s1["attn"] = s0["attn2"][:, c:]

    by_key = {it: (s, y) for it, s, y in zip(items, st, uw)}
    chains = [(hh, dr) for hh in range(SCAN_HEADS) for dr in range(2)]
    state = {ch: s_scr[ch[0], ch[1]] for ch in chains}
    for i in range(n_chunks):
        cur = {ch: by_key[(ch[0], ch[1], i if ch[1] == 0 else n_chunks - 1 - i)] for ch in chains}
        wsqs = {ch: _bdot(jnp.concatenate([cur[ch][1][:, LANES:].astype(BF16), cur[ch][0]["q_dec"]], axis=0),
                          state[ch]) for ch in chains}
        v_new = {ch: cur[ch][1][:, :LANES] - wsqs[ch][:c] for ch in chains}
        for ch in chains:
            s = cur[ch][0]
            refs[ch[1]][5][s["rs"], s["ls"]] = wsqs[ch][c:] + _bdot(s["attn"], v_new[ch])
        state = {ch: state[ch] * cur[ch][0]["e_last"] + _dot_tn(cur[ch][0]["k_dec"], v_new[ch])
                 for ch in chains}
    for ch in chains:
        s_scr[ch[0], ch[1]] = state[ch]

    @pl.when(step == n_steps - 1)
    def _():
        sfin_ref[...] = s_scr[...]


def _dn_scan(qn, kn, vv, gb, gbt, s0, batch, seq):
    rows, d = qn.shape
    tt = SCAN_TILE
    n_steps = seq // tt
    n_groups = DN_HEADS // SCAN_HEADS
    w = SCAN_HEADS * LANES
    fwd_t = lambda b, s: b * n_steps + s
    bwd_t = lambda b, s: b * n_steps + n_steps - 1 - s

    def specs(tile):
        wide = pl.BlockSpec((tt, w), lambda b, g, s: (tile(b, s), g))
        return wide, [wide, wide, wide,
                      pl.BlockSpec((tt, LANES), lambda b, g, s: (tile(b, s), 0)),
                      pl.BlockSpec((LANES, tt), lambda b, g, s: (0, tile(b, s)))]

    out_f, in_f = specs(fwd_t)
    out_b, in_b = specs(bwd_t)
    st_spec = pl.BlockSpec((None, SCAN_HEADS, 2, LANES, LANES), lambda b, g, s: (b, g, 0, 0, 0))
    return pl.pallas_call(
        functools.partial(_scan_kernel, n_steps),
        grid=(batch, n_groups, n_steps),
        in_specs=in_f + in_b + [st_spec],
        out_specs=[out_f, out_b, st_spec],
        out_shape=[jax.ShapeDtypeStruct((rows, d), F32), jax.ShapeDtypeStruct((rows, d), F32),
                   jax.ShapeDtypeStruct((batch, DN_HEADS, 2, LANES, LANES), F32)],
        scratch_shapes=[pltpu.VMEM((SCAN_HEADS, 2, LANES, LANES), F32)],
        compiler_params=_cparams(("arbitrary", "arbitrary", "arbitrary")),
        name="dn_scan",
    )(qn, kn, vv, gb, gbt, qn, kn, vv, gb, gbt, s0)


def _na_kernel(n_rows, q_ref, k_ref, v_ref, kc_ref, vc_ref, bias_ref, o_ref):
    t = pl.program_id(2)
    kw = NA_KROWS * GRID_W
    ks = jnp.clip(t * NA_QROWS - NA_WIN_R // 2, 0, n_rows - NA_KROWS)
    start = pl.multiple_of(ks * GRID_W, GRID_W)
    q = q_ref[...] * (NA_DH ** -0.5)
    kwin = k_ref[pl.ds(start, kw), :]
    vwin = v_ref[pl.ds(start, kw), :]
    kc = kc_ref[...]
    vc = vc_ref[...]
    lane = lax.broadcasted_iota(jnp.int32, q.shape, 1)
    outs = []
    for hh in range(2):
        sel = (lane < NA_DH) if hh == 0 else (lane >= NA_DH)
        qh = jnp.where(sel, q, jnp.zeros_like(q))
        s_loc = lax.dot_general(qh, kwin, (((1,), (1,)), ((), ())),
                                preferred_element_type=F32) + bias_ref[hh]
        s_ctx = lax.dot_general(qh, kc, (((1,), (1,)), ((), ())), preferred_element_type=F32)
        m = jnp.maximum(jnp.max(s_loc, axis=1, keepdims=True), jnp.max(s_ctx, axis=1, keepdims=True))
        p_loc = jnp.exp(s_loc - m)
        p_ctx = jnp.exp(s_ctx - m)
        denom = jnp.sum(p_loc, axis=1, keepdims=True) + jnp.sum(p_ctx, axis=1, keepdims=True)
        o = (jnp.dot(p_loc.astype(BF16), vwin, preferred_element_type=F32)
             + jnp.dot(p_ctx.astype(BF16), vc, preferred_element_type=F32)) / denom
        outs.append(o)
    o_ref[...] = jnp.where(lane < NA_DH, outs[0], outs[1]).astype(BF16)


def _na_attention(p, pc, bias, batch, seq, ctx_len, q_col, k_col, v_col):
    rows = p.shape[0]
    n_rows = seq // GRID_W
    qt = NA_QROWS * GRID_W
    n_tiles = n_rows // NA_QROWS
    kw = NA_KROWS * GRID_W
    n_pairs = NA_HEADS // 2

    def geom(t):
        return jnp.where(t == 0, 0, jnp.where(t == n_tiles - 1, 2, 1))

    return pl.pallas_call(
        functools.partial(_na_kernel, n_rows),
        grid=(batch, n_pairs, n_tiles),
        in_specs=[pl.BlockSpec((qt, LANES), lambda b, pr, t: (b * n_tiles + t, q_col + pr)),
                  pl.BlockSpec((seq, LANES), lambda b, pr, t: (b, k_col + pr)),
                  pl.BlockSpec((seq, LANES), lambda b, pr, t: (b, v_col + pr)),
                  pl.BlockSpec((ctx_len, LANES), lambda b, pr, t: (b, k_col + pr)),
                  pl.BlockSpec((ctx_len, LANES), lambda b, pr, t: (b, v_col + pr)),
                  pl.BlockSpec((None, 2, qt, kw), lambda b, pr, t: (geom(t), pr, 0, 0))],
        out_specs=pl.BlockSpec((qt, LANES), lambda b, pr, t: (b * n_tiles + t, pr)),
        out_shape=jax.ShapeDtypeStruct((rows, n_pairs * LANES), BF16),
        compiler_params=_cparams(("arbitrary", "arbitrary", "arbitrary")),
        name="na_attention",
    )(p, p, p, pc, pc, bias)


def _ctx_attn_kernel(q_ref, k_ref, v_ref, o_ref):
    q = q_ref[...] * (NA_DH ** -0.5)
    k = k_ref[...]
    v = v_ref[...]
    lane = lax.broadcasted_iota(jnp.int32, q.shape, 1)
    outs = []
    for hh in range(2):
        sel = (lane < NA_DH) if hh == 0 else (lane >= NA_DH)
        qh = jnp.where(sel, q, jnp.zeros_like(q))
        s = lax.dot_general(qh, k, (((1,), (1,)), ((), ())), preferred_element_type=F32)
        pm = jnp.exp(s - jnp.max(s, axis=1, keepdims=True))
        outs.append(jnp.dot(pm.astype(BF16), v, preferred_element_type=F32)
                    / jnp.sum(pm, axis=1, keepdims=True))
    o_ref[...] = jnp.where(lane < NA_DH, outs[0], outs[1]).astype(BF16)


def _ctx_attention(pc, batch, ctx_len, q_col, k_col, v_col):
    n_pairs = NA_HEADS // 2
    return pl.pallas_call(
        _ctx_attn_kernel,
        grid=(batch, n_pairs),
        in_specs=[pl.BlockSpec((ctx_len, LANES), lambda b, pr: (b, q_col + pr)),
                  pl.BlockSpec((ctx_len, LANES), lambda b, pr: (b, k_col + pr)),
                  pl.BlockSpec((ctx_len, LANES), lambda b, pr: (b, v_col + pr))],
        out_specs=pl.BlockSpec((ctx_len, LANES), lambda b, pr: (b, pr)),
        out_shape=jax.ShapeDtypeStruct((pc.shape[0], n_pairs * LANES), BF16),
        compiler_params=_cparams(("arbitrary", "arbitrary")),
        name="ctx_attention",
    )(pc, pc, pc)


def _na_bias_tables(rpb, n_rows):
    n_tiles = n_rows // NA_QROWS
    n_roff = 2 * NA_WIN_R - 1
    n_coff = 2 * NA_WIN_W - 1
    col = np.arange(GRID_W)
    c0 = np.clip(col - NA_WIN_W // 2, 0, GRID_W - NA_WIN_W)
    col_in = (col[None, :] >= c0[:, None]) & (col[None, :] < c0[:, None] + NA_WIN_W)
    coff = np.clip(col[None, :] - col[:, None] + (NA_WIN_W - 1), 0, n_coff - 1)
    col_sel = (coff[:, :, None] == np.arange(n_coff)).astype(np.float32)
    row_sel = np.zeros((3, NA_QROWS, NA_KROWS, n_roff), np.float32)
    mask = np.zeros((3, NA_QROWS, GRID_W, NA_KROWS, GRID_W), np.float32)
    for g, t in enumerate((0, 1, n_tiles - 1)):
        rs = t * NA_QROWS
        ks = min(max(rs - NA_WIN_R // 2, 0), n_rows - NA_KROWS)
        qrow = rs + np.arange(NA_QROWS)
        krow = ks + np.arange(NA_KROWS)
        r0 = np.clip(qrow - NA_WIN_R // 2, 0, n_rows - NA_WIN_R)
        row_in = (krow[None, :] >= r0[:, None]) & (krow[None, :] < r0[:, None] + NA_WIN_R)
        roff = np.clip(krow[None, :] - qrow[:, None] + (NA_WIN_R - 1), 0, n_roff - 1)
        row_sel[g] = (roff[:, :, None] == np.arange(n_roff)) & row_in[:, :, None]
        ok = row_in[:, None, :, None] & col_in[None, :, None, :]
        mask[g] = np.where(ok, 0.0, -np.inf)
    hp = lax.Precision.HIGHEST
    by_col = jnp.einsum("hrc,qkc->hrqk", rpb, jnp.asarray(col_sel), precision=hp)
    tab = jnp.einsum("gair,hrqk->ghaqik", jnp.asarray(row_sel), by_col, precision=hp)
    tab = tab + jnp.asarray(mask)[:, None]
    return tab.reshape(3, rpb.shape[0], NA_QROWS * GRID_W, NA_KROWS * GRID_W)


def _merge_kernel(of_ref, ob_ref, z_ref, na_ref, gd_ref, gn_ref, x_ref, dnw_ref, wpa_ref, wpb_ref,
                  wout_ref, gpost_ref, g1_ref, o_ref, dn_scr):
    o = of_ref[...] + ob_ref[...]
    z = z_ref[...].astype(F32)
    for hh in range(DN_HEADS):
        sl = slice(hh * LANES, (hh + 1) * LANES)
        oh = o[:, sl]
        oh = oh * lax.rsqrt(jnp.mean(oh * oh, axis=-1, keepdims=True) + EPS) * dnw_ref[...]
        zh = z[:, sl]
        dn_scr[:, sl] = (oh * (zh * jax.nn.sigmoid(zh))).astype(BF16)
    y = (jax.nn.sigmoid(gd_ref[...].astype(F32)) * jnp.dot(dn_scr[...], wpa_ref[...], preferred_element_type=F32)
         + jax.nn.sigmoid(gn_ref[...].astype(F32)) * _bdot(na_ref[...], wpb_ref[...]))
    out = _bdot(y, wout_ref[...])
    o_ref[...] = x_ref[...] + g1_ref[...] * _rms(out, gpost_ref[...])


def _merge(o_f, o_b, p, na_o, x2, dn_norm, w_pa, w_pb, w_out, gpost, mod, mod_row, g1_blk, tm):
    rows, d = x2.shape
    nw = na_o.shape[1]
    row_blk = lambda c: pl.BlockSpec((tm, d), lambda i: (i, c))
    const = lambda shape: pl.BlockSpec(shape, lambda i: (0,) * len(shape))
    return pl.pallas_call(
        _merge_kernel,
        grid=(rows // tm,),
        in_specs=[row_blk(0), row_blk(0), row_blk(3), pl.BlockSpec((tm, nw), lambda i: (i, 0)),
                  row_blk(4), row_blk(5), row_blk(0),
                  const((1, LANES)), const((d, d)), const((nw, d)), const((d, d)), const((1, d)),
                  pl.BlockSpec((None, 1, d), lambda i: (mod_row(i), 0, g1_blk))],
        out_specs=row_blk(0),
        out_shape=jax.ShapeDtypeStruct((rows, d), F32),
        scratch_shapes=[pltpu.VMEM((tm, d), BF16)],
        compiler_params=_cparams(("arbitrary",)),
        name="merge",
    )(o_f, o_b, p, na_o, p, p, x2, dn_norm, w_pa, w_pb, w_out, gpost, mod)


def _ffn_kernel(n_f, x_ref, gpre_ref, sc_ref, sh_ref, w1_ref, w3_ref, w2_ref, gpost_ref, g2_ref,
                o_ref, h_scr, acc_scr):
    j = pl.program_id(1)

    @pl.when(j == 0)
    def _():
        h = _rms(x_ref[...], gpre_ref[...]) * (1.0 + sc_ref[...]) + sh_ref[...]
        h_scr[...] = h.astype(BF16)
        acc_scr[...] = jnp.zeros_like(acc_scr)

    h = h_scr[...]
    a = jnp.dot(h, w1_ref[...], preferred_element_type=F32)
    b = jnp.dot(h, w3_ref[...], preferred_element_type=F32)
    acc_scr[...] += _bdot(a * jax.nn.sigmoid(a) * b, w2_ref[...])

    @pl.when(j == n_f - 1)
    def _():
        o_ref[...] = x_ref[...] + g2_ref[...] * _rms(acc_scr[...], gpost_ref[...])


def _dense_ffn(x2, gpre, gpost, mod, mod_row, w1, w3, w2, tm, tf):
    rows, d = x2.shape
    f = w1.shape[1]
    n_f = f // tf
    modspec = lambda blk: pl.BlockSpec((None, 1, d), lambda i, j: (mod_row(i), 0, blk))
    return pl.pallas_call(
        functools.partial(_ffn_kernel, n_f),
        grid=(rows // tm, n_f),
        in_specs=[pl.BlockSpec((tm, d), lambda i, j: (i, 0)),
                  pl.BlockSpec((1, d), lambda i, j: (0, 0)),
                  modspec(4), modspec(3),
                  pl.BlockSpec((d, tf), lambda i, j: (0, j)),
                  pl.BlockSpec((d, tf), lambda i, j: (0, j)),
                  pl.BlockSpec((tf, d), lambda i, j: (j, 0)),
                  pl.BlockSpec((1, d), lambda i, j: (0, 0)),
                  modspec(5)],
        out_specs=pl.BlockSpec((tm, d), lambda i, j: (i, 0)),
        out_shape=jax.ShapeDtypeStruct((rows, d), F32),
        scratch_shapes=[pltpu.VMEM((tm, d), BF16), pltpu.VMEM((tm, d), F32)],
        compiler_params=_cparams(("arbitrary", "arbitrary")),
        name="dense_ffn",
    )(x2, gpre, mod, mod, w1, w3, w2, gpost, mod)


def _router_kernel(x_ref, gpre_ref, sc_ref, sh_ref, r_ref, h_ref, gate_ref):
    h = _rms(x_ref[...], gpre_ref[...]) * (1.0 + sc_ref[...]) + sh_ref[...]
    h_ref[...] = h
    logits = jnp.dot(h, r_ref[...], precision=lax.Precision.HIGHEST, preferred_element_type=F32)
    lane = lax.broadcasted_iota(jnp.int32, logits.shape, 1)
    neg = -jnp.inf
    l1 = jnp.where(lane < N_EXPERTS, logits, neg)
    m1 = jnp.max(l1, axis=1, keepdims=True)
    i1 = jnp.min(jnp.where(l1 == m1, lane, LANES), axis=1, keepdims=True)
    l2 = jnp.where(lane == i1, neg, l1)
    m2 = jnp.max(l2, axis=1, keepdims=True)
    i2 = jnp.min(jnp.where(l2 == m2, lane, LANES), axis=1, keepdims=True)
    e = jnp.exp(m2 - m1)
    w1 = 1.0 / (1.0 + e)
    w2 = e / (1.0 + e)
    out = jnp.where(lane == 0, i1.astype(F32), 0.0)
    out = jnp.where(lane == 1, i2.astype(F32), out)
    out = jnp.where(lane == 2, w1, out)
    out = jnp.where(lane == 3, w2, out)
    gate_ref[...] = out


def _router(x2, gpre, mod, mod_row, router_pad, tm):
    rows, d = x2.shape
    modspec = lambda blk: pl.BlockSpec((None, 1, d), lambda i: (mod_row(i), 0, blk))
    return pl.pallas_call(
        _router_kernel,
        grid=(rows // tm,),
        in_specs=[pl.BlockSpec((tm, d), lambda i: (i, 0)),
                  pl.BlockSpec((1, d), lambda i: (0, 0)),
                  modspec(4), modspec(3),
                  pl.BlockSpec((d, LANES), lambda i: (0, 0))],
        out_specs=[pl.BlockSpec((tm, d), lambda i: (i, 0)),
                   pl.BlockSpec((tm, LANES), lambda i: (i, 0))],
        out_shape=[jax.ShapeDtypeStruct((rows, d), F32),
                   jax.ShapeDtypeStruct((rows, LANES), F32)],
        compiler_params=_cparams(("arbitrary",)),
        name="moe_router",
    )(x2, gpre, mod, mod, router_pad)


def _gather_row(h_hbm, xbuf, sem, slot, r, tok):
    return pltpu.make_async_copy(h_hbm.at[pl.ds(tok, 1), :], xbuf.at[slot, pl.ds(r, 1), :], sem.at[slot])


def _scatter_row(stage, out_hbm, sem, slot, r, dst):
    return pltpu.make_async_copy(stage.at[slot, pl.ds(r, 1), :], out_hbm.at[pl.ds(dst, 1), :], sem.at[slot])


def _expert_kernel(n_f, n_tiles, te_ref, nv_ref, tok_ref, tokn_ref, dstp_ref, dst_ref, h_hbm,
                   w1_ref, w3_ref, w2_ref, out_hbm, xbuf, xb16, acc_scr, stage, gsem, ssem):
    i = pl.program_id(0)
    j = pl.program_id(1)
    slot = i % 2
    rows_per_step = MOE_TM // n_f

    def wait_gathers(sl):
        pltpu.make_async_copy(h_hbm.at[pl.ds(0, MOE_TM), :], xbuf.at[sl], gsem.at[sl]).wait()

    def wait_scatters(sl):
        pltpu.make_async_copy(stage.at[sl], out_hbm.at[pl.ds(0, MOE_TM), :], ssem.at[sl]).wait()

    @pl.when((i == 0) & (j == 0))
    def _():
        stage[...] = jnp.zeros_like(stage)

        def body(r, carry):
            _gather_row(h_hbm, xbuf, gsem, slot, r, tok_ref[0, r]).start()
            return carry
        lax.fori_loop(0, MOE_TM, body, 0, unroll=8)

    @pl.when(j == 0)
    def _():
        wait_gathers(slot)
        xb16[...] = xbuf[slot].astype(BF16)
        acc_scr[...] = jnp.zeros_like(acc_scr)

    def row_dmas():
        for k in range(rows_per_step):
            r = pl.multiple_of(j * rows_per_step, SUBLANES) + k
            _gather_row(h_hbm, xbuf, gsem, 1 - slot, r, tokn_ref[0, r]).start()
            _scatter_row(stage, out_hbm, ssem, 1 - slot, r, dstp_ref[0, r]).start()

    valid = i < nv_ref[0]

    @pl.when(valid)
    def _():
        row_dmas()
        x = xb16[...]
        a = jnp.dot(x, w1_ref[...], preferred_element_type=F32)
        b = jnp.dot(x, w3_ref[...], preferred_element_type=F32)
        acc_scr[...] += _bdot(a * jax.nn.sigmoid(a) * b, w2_ref[...])

    @pl.when(jnp.logical_not(valid))
    def _():
        row_dmas()

    @pl.when(j == n_f - 1)
    def _():
        @pl.when(i >= 1)
        def _():
            wait_scatters(slot)

        stage[slot] = acc_scr[...]

        @pl.when(i == n_tiles - 1)
        def _():
            wait_scatters(1 - slot)
            wait_gathers(1 - slot)

            def body(r, carry):
                _scatter_row(stage, out_hbm, ssem, slot, r, dst_ref[0, r]).start()
                return carry
            lax.fori_loop(0, MOE_TM, body, 0, unroll=8)
            wait_scatters(slot)


def _expert_ffn(h, row_token, out_row, tile_expert, n_valid, w1, w3, w2, tf):
    d = h.shape[1]
    prow = row_token.shape[0]
    f = w1.shape[2]
    n_f = f // tf
    n_tiles = prow // MOE_TM
    assert f % tf == 0 and MOE_TM % (n_f * SUBLANES) == 0
    idx_spec =lambda fn: pl.BlockSpec((None, 1, MOE_TM), lambda i, j, te, nv: (fn(i), 0, 0),
                                       memory_space=pltpu.SMEM)
    tok3 = row_token.reshape(n_tiles, 1, MOE_TM)
    dst3 = jnp.concatenate([prow + jnp.arange(MOE_TM, dtype=jnp.int32), out_row]).reshape(n_tiles + 1, 1, MOE_TM)
    grid_spec = pltpu.PrefetchScalarGridSpec(
        num_scalar_prefetch=2,
        grid=(n_tiles, n_f),
        in_specs=[idx_spec(lambda i: i),
                  idx_spec(lambda i: jnp.minimum(i + 1, n_tiles - 1)),
                  idx_spec(lambda i: i),
                  idx_spec(lambda i: i + 1),
                  pl.BlockSpec(memory_space=pl.ANY),
                  pl.BlockSpec((None, d, tf), lambda i, j, te, nv: (te[i], 0, j)),
                  pl.BlockSpec((None, d, tf), lambda i, j, te, nv: (te[i], 0, j)),
                  pl.BlockSpec((None, tf, d), lambda i, j, te, nv: (te[i], j, 0))],
        out_specs=pl.BlockSpec(memory_space=pl.ANY),
        scratch_shapes=[pltpu.VMEM((2, MOE_TM, d), F32), pltpu.VMEM((MOE_TM, d), BF16),
                        pltpu.VMEM((MOE_TM, d), F32), pltpu.VMEM((2, MOE_TM, d), F32),
                        pltpu.SemaphoreType.DMA((2,)), pltpu.SemaphoreType.DMA((2,))],
    )
    return pl.pallas_call(
        functools.partial(_expert_kernel, n_f, n_tiles),
        grid_spec=grid_spec,
        out_shape=jax.ShapeDtypeStruct((prow + MOE_TM, d), F32),
        compiler_params=_cparams(("arbitrary", "arbitrary")),
        name="moe_experts",
    )(tile_expert, n_valid, tok3, tok3, dst3, dst3, h, w1, w3, w2)


def _combine_kernel(y1_ref, y2_ref, gate_ref, x_ref, gpost_ref, g2_ref, o_ref):
    gt = gate_ref[...]
    lane = lax.broadcasted_iota(jnp.int32, gt.shape, 1)
    w1 = jnp.sum(jnp.where(lane == 2, gt, 0.0), axis=1, keepdims=True)
    w2 = jnp.sum(jnp.where(lane == 3, gt, 0.0), axis=1, keepdims=True)
    y = w1 * y1_ref[...] + w2 * y2_ref[...]
    o_ref[...] = x_ref[...] + g2_ref[...] * _rms(y, gpost_ref[...])


def _moe_combine(ys, gates, x2, gpost, mod, mod_row, tm):
    rows, d = x2.shape
    rb = pl.BlockSpec((tm, d), lambda i: (i, 0))
    return pl.pallas_call(
        _combine_kernel,
        grid=(rows // tm,),
        in_specs=[rb, pl.BlockSpec((tm, d), lambda i: (rows // tm + i, 0)),
                  pl.BlockSpec((tm, LANES), lambda i: (i, 0)), rb,
                  pl.BlockSpec((1, d), lambda i: (0, 0)),
                  pl.BlockSpec((None, 1, d), lambda i: (mod_row(i), 0, 5))],
        out_specs=rb,
        out_shape=jax.ShapeDtypeStruct((rows, d), F32),
        compiler_params=_cparams(("arbitrary",)),
        name="moe_combine",
    )(ys, ys, gates, x2, gpost, mod)


def _moe_schedule(gates, n_tokens):
    idx = gates[:, 0:2].astype(jnp.int32)
    flat_e = idx.reshape(-1)
    onehot = (flat_e[:, None] == jnp.arange(N_EXPERTS)[None, :]).astype(jnp.int32)
    csum = jnp.cumsum(onehot, axis=0)
    counts = csum[-1]
    rank = jnp.sum(csum * onehot, axis=1) - 1
    padded = ((counts + MOE_TM - 1) // MOE_TM) * MOE_TM
    ends = jnp.cumsum(padded)
    starts = ends - padded
    dest = jnp.sum(starts[None, :] * onehot, axis=1) + rank
    n_pairs = 2 * n_tokens
    n_rows = n_pairs + N_EXPERTS * MOE_TM
    n_tiles = n_rows // MOE_TM
    pair_of_row = jnp.full((n_rows,), -1, jnp.int32).at[dest].set(jnp.arange(n_pairs, dtype=jnp.int32))
    is_pad = pair_of_row < 0
    row_token = jnp.where(is_pad, 0, pair_of_row // 2)
    out_row = jnp.where(is_pad, n_pairs - 1 + jnp.cumsum(is_pad.astype(jnp.int32)),
                        (pair_of_row % 2) * n_tokens + pair_of_row // 2)
    tile_start = jnp.arange(n_tiles, dtype=jnp.int32) * MOE_TM
    tile_expert = jnp.minimum(jnp.sum((ends[None, :] <= tile_start[:, None]).astype(jnp.int32), axis=1),
                              N_EXPERTS - 1)
    n_valid = (ends[-1] // MOE_TM).astype(jnp.int32).reshape(1)
    return row_token, out_row, tile_expert, n_valid


def _rope_tables(seq):
    t = jnp.arange(seq)
    row = (t // GRID_W).astype(F32)
    col = (t % GRID_W).astype(F32)
    n_freq = DN_DK // 4
    inv = ROPE_BASE ** (-jnp.arange(n_freq, dtype=F32) / n_freq)
    ang = jnp.concatenate([row[:, None] * inv, col[:, None] * inv], axis=-1)
    cos, sin = jnp.cos(ang), jnp.sin(ang)
    return jnp.concatenate([cos, cos], axis=-1), jnp.concatenate([-sin, sin], axis=-1)


def kernel(x, c, ctx, c_ctx, ada_w, ada_b, norm_mix_pre, norm_mix_post, norm_ffn_pre, norm_ffn_post,
           w_in, dn_conv, dn_a_log, dn_dt_bias, dn_norm, na_rpb, w_branch_dn, w_branch_na, w_out,
           ffn_w1, ffn_w3, ffn_w2, moe_router, moe_w1, moe_w3, moe_w2):
    batch, seq, d = x.shape
    ctx_len = ctx.shape[1]
    depth = w_in.shape[0]
    nh = DN_HEADS
    dn_w = nh * DN_DK
    na_w = NA_HEADS * NA_DH
    n_rows = seq // GRID_W
    assert d == dn_w and seq % SCAN_TILE == 0 and ctx_len % SCAN_TILE == 0 and n_rows % NA_QROWS == 0
    assert depth <= 2, "context tokens only take the dense FFN path"

    c_rows = jnp.zeros((SUBLANES, d), F32).at[:batch].set(c).at[batch].set(c_ctx)
    mod_all = _mod_vectors(c_rows, ada_w, ada_b)
    cos2, sin2 = _rope_tables(seq)
    ones_t = jnp.ones((SCAN_TILE, LANES), F32)

    x2 = x.reshape(batch * seq, d)
    xc2 = ctx.reshape(batch * ctx_len, d)
    lat_tm = 1024
    lat_row = lambda tm: (lambda i: i // (seq // tm))
    ctx_row = lambda i: batch

    q_col, k_col, v_col = (4 * dn_w + 2 * d) // LANES, (4 * dn_w + 2 * d + na_w) // LANES, \
        (4 * dn_w + 2 * d + 2 * na_w) // LANES

    for l in range(depth):
        last = l == depth - 1
        mod = mod_all[l].reshape(SUBLANES, 1, 6 * d)
        wl = w_in[l]
        o_ab = 4 * dn_w
        o_na = o_ab + 4 * nh
        o_gate = o_na + 3 * na_w
        w_main = jnp.concatenate([wl[:, :o_ab], wl[:, o_gate:], wl[:, o_na:o_gate]], axis=1).astype(BF16)
        wab = jnp.pad(wl[:, o_ab:o_na], ((0, 0), (0, LANES - 4 * nh)))
        wab_hi = wab.astype(BF16)
        wab_lo = (wab - wab_hi.astype(F32)).astype(BF16)
        gpre = norm_mix_pre[l].reshape(1, d)
        gpost = norm_mix_post[l].reshape(1, d)

        p, ab = _in_proj(x2, gpre, mod, lat_row(lat_tm), 1, 0, w_main, wab_hi, wab_lo, lat_tm, INPROJ_TN)
        pc, abc = _in_proj(xc2, gpre, mod, ctx_row, 1, 0, w_main, wab_hi, wab_lo, batch * ctx_len, INPROJ_TN)

        conv_w3 = jnp.pad(dn_conv[l].T.reshape(DN_CONV, 3, dn_w).transpose(1, 0, 2),
                          ((0, 0), (0, SUBLANES - DN_CONV), (0, 0)))
        gpar = jnp.zeros((SUBLANES, LANES), F32)
        gpar = gpar.at[0, :2 * nh].set(-jnp.exp(dn_a_log[l].reshape(-1)))
        gpar = gpar.at[1, :2 * nh].set(dn_dt_bias[l].reshape(-1))

        qc_, kc_, vc_, gbc, gbtc = _dn_prep(pc, abc, conv_w3, gpar, ones_t, ones_t, batch, ctx_len, False)
        ql_, kl_, vl_, gbl, gbtl = _dn_prep(p, ab, conv_w3, gpar, cos2, sin2, batch, seq, True)
        s0 = jnp.zeros((batch, nh, 2, LANES, LANES), F32)
        oc_f, oc_b, s_ctx = _dn_scan(qc_, kc_, vc_, gbc, gbtc, s0, batch, ctx_len)
        ol_f, ol_b, _ = _dn_scan(ql_, kl_, vl_, gbl, gbtl, s_ctx, batch, seq)

        bias = _na_bias_tables(na_rpb[l], n_rows)
        na_lat = _na_attention(p, pc, bias, batch, seq, ctx_len, q_col, k_col, v_col)

        dnw = dn_norm[l].reshape(1, LANES)
        w_pa = w_branch_dn[l].astype(BF16)
        w_pb = w_branch_na[l].astype(BF16)
        w_o = w_out[l].astype(BF16)
        x2 = _merge(ol_f, ol_b, p, na_lat, x2, dnw, w_pa, w_pb, w_o, gpost, mod, lat_row(512), 2, 512)

        gfpre = norm_ffn_pre[l].reshape(1, d)
        gfpost = norm_ffn_post[l].reshape(1, d)
        if l % 2 == 0:
            w1 = ffn_w1[l // 2].astype(BF16)
            w3 = ffn_w3[l // 2].astype(BF16)
            w2 = ffn_w2[l // 2].astype(BF16)
            tf = w1.shape[1] // 2
            x2 = _dense_ffn(x2, gfpre, gfpost, mod, lat_row(512), w1, w3, w2, 512, tf)
        else:
            rpad = jnp.pad(moe_router[l // 2], ((0, 0), (0, LANES - N_EXPERTS)))
            hb, gates = _router(x2, gfpre, mod, lat_row(512), rpad, 512)
            n_tok = batch * seq
            row_token, out_row, tile_expert, n_valid = _moe_schedule(gates, n_tok)
            ys = _expert_ffn(hb, row_token, out_row, tile_expert, n_valid, moe_w1[l // 2].astype(BF16),
                             moe_w3[l // 2].astype(BF16), moe_w2[l // 2].astype(BF16), 896)
            x2 = _moe_combine(ys, gates, x2, gfpost, mod, lat_row(512), 512)

        if not last:
            na_ctx = _ctx_attention(pc, batch, ctx_len, q_col, k_col, v_col)
            xc2 = _merge(oc_f, oc_b, pc, na_ctx, xc2, dnw, w_pa, w_pb, w_o, gpost, mod, ctx_row, 2, 256)
            xc2 = _dense_ffn(xc2, gfpre, gfpost, mod, ctx_row, w1, w3, w2, 512, tf)
    return x2.reshape(batch, seq, d)
```

```python
import functools

import numpy as np
import jax
import jax.numpy as jnp
from jax import lax
from jax.experimental import pallas as pl
from jax.experimental.pallas import tpu as pltpu

F32 = jnp.float32
BF16 = jnp.bfloat16

GRID_W = 64
DN_HEADS = 8
DN_DK = 128
DN_CONV = 5
DN_CHUNK = 64
NA_HEADS = 8
NA_DH = 64
NA_WIN_R = 8
NA_WIN_W = 16
ROPE_BASE = 10000.0
N_EXPERTS = 8
EPS = 1e-6

LANES = 128
SUBLANES = 8
BF16_SUBLANES = 16
VMEM_LIMIT = 56 * 1024 * 1024

SCAN_TILE = 256
SCAN_HEADS = 8
NA_QROWS = 4
NA_KROWS = NA_QROWS + 8
MOE_TM = 512
INPROJ_TN = 1920


def _cparams(sem):
    return pltpu.CompilerParams(dimension_semantics=sem, vmem_limit_bytes=VMEM_LIMIT)


def _bdot(a, b):
    return jnp.dot(a.astype(BF16), b.astype(BF16), preferred_element_type=F32)


def _dot_nt(a, b):
    return lax.dot_general(a.astype(BF16), b.astype(BF16), (((1,), (1,)), ((), ())),
                           preferred_element_type=F32)


def _dot_tn(a, b):
    return lax.dot_general(a.astype(BF16), b.astype(BF16), (((0,), (0,)), ((), ())),
                           preferred_element_type=F32)


def _split3(x):
    hi = x.astype(BF16)
    r = x - hi.astype(F32)
    mid = r.astype(BF16)
    lo = (r - mid.astype(F32)).astype(BF16)
    return hi, mid, lo


def _rms(x, gain):
    return x * lax.rsqrt(jnp.mean(x * x, axis=-1, keepdims=True) + EPS) * gain


def _mod_kernel(c_ref, w_ref, b_ref, o_ref):
    c = c_ref[...]
    s = c * jax.nn.sigmoid(c)
    o_ref[0] = jnp.dot(s, w_ref[0], precision=lax.Precision.HIGHEST,
                       preferred_element_type=F32) + b_ref[0]


def _mod_vectors(c_rows, ada_w, ada_b):
    depth, d, n = ada_w.shape
    tn = 1536
    return pl.pallas_call(
        _mod_kernel,
        grid=(depth, n // tn),
        in_specs=[pl.BlockSpec((SUBLANES, d), lambda l, j: (0, 0)),
                  pl.BlockSpec((1, d, tn), lambda l, j: (l, 0, j)),
                  pl.BlockSpec((1, 1, tn), lambda l, j: (l, 0, j))],
        out_specs=pl.BlockSpec((1, SUBLANES, tn), lambda l, j: (l, 0, j)),
        out_shape=jax.ShapeDtypeStruct((depth, SUBLANES, n), F32),
        compiler_params=_cparams(("arbitrary", "arbitrary")),
        name="mod_vectors",
    )(c_rows, ada_w, ada_b.reshape(depth, 1, n))


def _inproj_kernel(x_ref, g_ref, sc_ref, sh_ref, w_ref, wab_hi_ref, wab_lo_ref,
                   o_ref, ab_ref, h_scr, hlo_scr):
    j = pl.program_id(1)

    @pl.when(j == 0)
    def _():
        h = _rms(x_ref[...], g_ref[...]) * (1.0 + sc_ref[...]) + sh_ref[...]
        hi = h.astype(BF16)
        lo = (h - hi.astype(F32)).astype(BF16)
        h_scr[...] = hi
        hlo_scr[...] = lo
        ab_ref[...] = (jnp.dot(hi, wab_hi_ref[...], preferred_element_type=F32)
                       + jnp.dot(lo, wab_hi_ref[...], preferred_element_type=F32)
                       + jnp.dot(hi, wab_lo_ref[...], preferred_element_type=F32))

    o_ref[...] = jnp.dot(h_scr[...], w_ref[...], preferred_element_type=F32).astype(BF16)


def _in_proj(x2, gain, mod, mod_row, sc_blk, sh_blk, w_main, wab_hi, wab_lo, tm, tn):
    rows, d = x2.shape
    n = w_main.shape[1]
    return pl.pallas_call(
        _inproj_kernel,
        grid=(rows // tm, n // tn),
        in_specs=[pl.BlockSpec((tm, d), lambda i, j: (i, 0)),
                  pl.BlockSpec((1, d), lambda i, j: (0, 0)),
                  pl.BlockSpec((None, 1, d), lambda i, j: (mod_row(i), 0, sc_blk)),
                  pl.BlockSpec((None, 1, d), lambda i, j: (mod_row(i), 0, sh_blk)),
                  pl.BlockSpec((d, tn), lambda i, j: (0, j)),
                  pl.BlockSpec((d, LANES), lambda i, j: (0, 0)),
                  pl.BlockSpec((d, LANES), lambda i, j: (0, 0))],
        out_specs=[pl.BlockSpec((tm, tn), lambda i, j: (i, j)),
                   pl.BlockSpec((tm, LANES), lambda i, j: (i, 0))],
        out_shape=[jax.ShapeDtypeStruct((rows, n), BF16),
                   jax.ShapeDtypeStruct((rows, LANES), F32)],
        scratch_shapes=[pltpu.VMEM((tm, d), BF16), pltpu.VMEM((tm, d), BF16)],
        compiler_params=_cparams(("arbitrary", "arbitrary")),
        name="in_proj",
    )(x2, gain, mod, mod, w_main, wab_hi, wab_lo)


def _prep_kernel(rope, n_tiles,
                 q_ref, qp_ref, qn_ref, k_ref, kp_ref, kn_ref, v_ref, vp_ref, vn_ref,
                 cw_ref, ab_ref, gpar_ref, cos_ref, sin_ref,
                 qo_ref, ko_ref, vo_ref, gb_ref, gbt_ref, xq_scr, xk_scr, xv_scr):
    t = pl.program_id(1)
    tt = q_ref.shape[0]
    first = t == 0
    last = t == n_tiles - 1
    pad = DN_CONV // 2
    halo = qp_ref.shape[0]

    for scr, m_ref, p_ref, n_ref in ((xq_scr, q_ref, qp_ref, qn_ref), (xk_scr, k_ref, kp_ref, kn_ref),
                                     (xv_scr, v_ref, vp_ref, vn_ref)):
        scr[0:halo, :] = jnp.where(first, 0.0, p_ref[...].astype(F32))
        scr[halo:halo + tt, :] = m_ref[...].astype(F32)
        scr[halo + tt:, :] = jnp.where(last, 0.0, n_ref[...].astype(F32))

    def conv_silu(scr, w, sl):
        acc = scr[halo - pad:halo - pad + tt, sl] * w[0:1]
        for i in range(1, DN_CONV):
            o = halo - pad + i
            acc = acc + scr[o:o + tt, sl] * w[i:i + 1]
        return acc * jax.nn.sigmoid(acc)

    def l2n(x):
        return x * lax.rsqrt(jnp.sum(x * x, axis=-1, keepdims=True) + EPS)

    def rot(x):
        if not rope:
            return x
        return x * cos_ref[...] + pltpu.roll(x, LANES // 2, 1) * sin_ref[...]

    for hh in range(DN_HEADS):
        sl = slice(hh * LANES, (hh + 1) * LANES)
        qo_ref[:, sl] = (rot(l2n(conv_silu(xq_scr, cw_ref[0, :, sl], sl))) * (DN_DK ** -0.5)).astype(BF16)
        ko_ref[:, sl] = rot(l2n(conv_silu(xk_scr, cw_ref[1, :, sl], sl))).astype(BF16)
        vo_ref[:, sl] = conv_silu(xv_scr, cw_ref[2, :, sl], sl).astype(BF16)

    ab = ab_ref[...]
    lane = lax.broadcasted_iota(jnp.int32, ab.shape, 1)
    row = lax.broadcasted_iota(jnp.int32, ab.shape, 0) % DN_CHUNK
    xg = ab + gpar_ref[1:2]
    sp = jnp.maximum(xg, 0.0) + jnp.log1p(jnp.exp(-jnp.abs(xg)))
    g = gpar_ref[0:1] * sp
    beta = jax.nn.sigmoid(ab)
    pre = g
    suf = g
    s = 1
    while s < DN_CHUNK:
        pre = pre + jnp.where(row >= s, pltpu.roll(pre, s, 0), 0.0)
        suf = suf + jnp.where(row < DN_CHUNK - s, pltpu.roll(suf, tt - s, 0), 0.0)
        s *= 2
    nh = DN_HEADS
    gb = jnp.where(lane < nh, pre, jnp.where(lane < 2 * nh, suf, beta))
    gb_ref[...] = gb
    er = lax.broadcasted_iota(jnp.int32, (LANES, 3 * LANES), 0)
    ec = lax.broadcasted_iota(jnp.int32, (LANES, 3 * LANES), 1)
    eye3 = ((ec % LANES) == er).astype(BF16)
    gbt_ref[...] = lax.dot_general(eye3, jnp.concatenate(_split3(gb), axis=1),
                                   (((1,), (1,)), ((), ())), preferred_element_type=F32)


def _dn_prep(p, ab, conv_w3, gpar, cos2, sin2, batch, seq, rope):
    rows = p.shape[0]
    tt = SCAN_TILE
    n_tiles = seq // tt
    halo = BF16_SUBLANES
    hb = tt // halo
    n_hblk = rows // halo
    d = DN_HEADS * LANES

    def main(cb):
        return pl.BlockSpec((tt, d), lambda b, t: (b * n_tiles + t, cb))

    def prev(cb):
        return pl.BlockSpec((halo, d), lambda b, t: (jnp.maximum((b * n_tiles + t) * hb - 1, 0), cb))

    def nxt(cb):
        return pl.BlockSpec((halo, d),
                            lambda b, t: (jnp.minimum((b * n_tiles + t + 1) * hb, n_hblk - 1), cb))

    in_specs = []
    for cb in range(3):
        in_specs += [main(cb), prev(cb), nxt(cb)]
    in_specs += [
        pl.BlockSpec((3, SUBLANES, d), lambda b, t: (0, 0, 0)),
        pl.BlockSpec((tt, LANES), lambda b, t: (b * n_tiles + t, 0)),
        pl.BlockSpec((SUBLANES, LANES), lambda b, t: (0, 0)),
        pl.BlockSpec((tt, LANES), lambda b, t: (t, 0)),
        pl.BlockSpec((tt, LANES), lambda b, t: (t, 0)),
    ]
    out_full = pl.BlockSpec((tt, d), lambda b, t: (b * n_tiles + t, 0))
    return pl.pallas_call(
        functools.partial(_prep_kernel, rope, n_tiles),
        grid=(batch, n_tiles),
        in_specs=in_specs,
        out_specs=[out_full, out_full, out_full,
                   pl.BlockSpec((tt, LANES), lambda b, t: (b * n_tiles + t, 0)),
                   pl.BlockSpec((LANES, tt), lambda b, t: (0, b * n_tiles + t))],
        out_shape=[jax.ShapeDtypeStruct((rows, d), BF16)] * 3
        + [jax.ShapeDtypeStruct((rows, LANES), F32), jax.ShapeDtypeStruct((LANES, rows), F32)],
        scratch_shapes=[pltpu.VMEM((tt + 2 * halo, d), F32)] * 3,
        compiler_params=_cparams(("arbitrary", "arbitrary")),
        name="dn_prep_rope" if rope else "dn_prep",
    )(p, p, p, p, p, p, p, p, p, conv_w3, ab, gpar, cos2, sin2)


def _scan_kernel(n_steps,
                 qf_ref, kf_ref, vf_ref, gf_ref, gtf_ref, qb_ref, kb_ref, vb_ref, gb_ref, gtb_ref, s0_ref,
                 of_ref, ob_ref, sfin_ref, s_scr):
    hg = pl.program_id(1)
    step = pl.program_id(2)
    c = DN_CHUNK
    n_chunks = qf_ref.shape[0] // c
    refs = ((qf_ref, kf_ref, vf_ref, gf_ref, gtf_ref, of_ref),
            (qb_ref, kb_ref, vb_ref, gb_ref, gtb_ref, ob_ref))

    @pl.when(step == 0)
    def _():
        s_scr[...] = s0_ref[...]

    ri = lax.broadcasted_iota(jnp.int32, (c, c), 0)
    ci = lax.broadcasted_iota(jnp.int32, (c, c), 1)
    lane = lax.broadcasted_iota(jnp.int32, (c, LANES), 1)
    eye = (ri == ci).astype(F32)
    incl = (ri >= ci, ri <= ci)
    strict = (ri > ci, ri < ci)

    def pick(tile, idx):
        return jnp.sum(jnp.where(lane == idx, tile, 0.0), axis=1, keepdims=True)

    sub = lax.broadcasted_iota(jnp.int32, (DN_HEADS, qf_ref.shape[0]), 0)
    gc_rows = {(hh, dr): jnp.sum(jnp.where(sub == hg * SCAN_HEADS + hh,
                                           refs[dr][4][dr * DN_HEADS:(dr + 1) * DN_HEADS, :], 0.0),
                                 axis=0, keepdims=True)
               for hh in range(SCAN_HEADS) for dr in range(2)}

    items = [(hh, dr, cc) for hh in range(SCAN_HEADS) for dr in range(2) for cc in range(n_chunks)]
    st = []
    for hh, dr, cc in items:
        q_ref, k_ref, v_ref, g_ref, gt_ref, _ = refs[dr]
        rs = slice(cc * c, (cc + 1) * c)
        ls = slice(hh * LANES, (hh + 1) * LANES)
        head = hg * SCAN_HEADS + hh
        gtile = g_ref[rs, :]
        gc = pick(gtile, dr * DN_HEADS + head)
        beta = pick(gtile, (2 + dr) * DN_HEADS + head)
        gc_row = gc_rows[(hh, dr)][:, rs]
        edge = c - 1 if dr == 0 else 0
        g_last = gc_row[:, edge:edge + 1]
        q = q_ref[rs, ls].astype(F32)
        k = k_ref[rs, ls].astype(F32)
        v = v_ref[rs, ls].astype(F32)
        egc = jnp.exp(gc)
        kbeta = k * beta
        st.append(dict(dr=dr, rs=rs, ls=ls, q=q, k=k, gc=gc, kbeta=kbeta,
                       rhs=jnp.concatenate([v * beta, kbeta * egc], axis=1).astype(BF16),
                       k_dec=(k * jnp.exp(g_last - gc)).astype(BF16),
                       q_dec=(q * egc).astype(BF16),
                       e_last=jnp.exp(g_last)))

    assert n_chunks % 2 == 0
    pr = lax.broadcasted_iota(jnp.int32, (c, 2 * c), 0)
    pl2 = lax.broadcasted_iota(jnp.int32, (c, 2 * c), 1)
    pc = pl2 % c
    left = pl2 < c
    eye2 = (pr == pc).astype(F32)
    incl2 = (pr >= pc, pr <= pc)
    strict2 = (pr > pc, pr < pc)
    zeros_k = jnp.zeros((c, LANES), F32)
    zeros_r = jnp.zeros((c, 2 * LANES), BF16)

    def blockdiag(bp):
        return jnp.concatenate([jnp.where(left, bp, 0.0), jnp.where(left, 0.0, bp)], axis=0)

    def pdot(ap, bp):
        return _bdot(ap, blockdiag(bp))

    pairs = [(st[2 * m], st[2 * m + 1], items[2 * m]) for m in range(len(st) // 2)]
    tms = []
    for s0, s1, (hh, dr, cc) in pairs:
        gcol = jnp.where(left, s0["gc"], s1["gc"])
        grow = gc_rows[(hh, dr)][:, cc * c:(cc + 2) * c]
        dec = jnp.where(incl2[dr], jnp.exp(jnp.where(incl2[dr], gcol - grow, 0.0)), 0.0)
        kk = jnp.concatenate([jnp.concatenate([s0["k"], zeros_k], axis=1),
                              jnp.concatenate([zeros_k, s1["k"]], axis=1)], axis=0)
        tms.append(jnp.where(strict2[dr],
                             _dot_nt(jnp.concatenate([s0["kbeta"], s1["kbeta"]], axis=1), kk) * dec, 0.0))
        s0["attn2"] = (_dot_nt(jnp.concatenate([s0["q"], s1["q"]], axis=1), kk) * dec).astype(BF16)

    m8 = (pr // 8) == (pc // 8)
    pw = [-jnp.where(m8, t, 0.0) for t in tms]
    p2 = [pdot(p, p) for p in pw]
    p4 = [pdot(p, p) for p in p2]
    xs = [eye2 + p for p in pw]
    xs = [x + pdot(p, x) for x, p in zip(xs, p2)]
    xs = [x + pdot(p, x) for x, p in zip(xs, p4)]
    blk = 8
    while blk < c:
        off = ((pr // (2 * blk)) == (pc // (2 * blk))) & ((pr // blk) != (pc // blk))
        lx = [pdot(jnp.where(off, t, 0.0), x) for t, x in zip(tms, xs)]
        xs = [x - pdot(x, y) for x, y in zip(xs, lx)]
        blk *= 2
    uw = []
    for (s0, s1, _), x in zip(pairs, xs):
        both = _bdot(x, jnp.concatenate([jnp.concatenate([s0["rhs"], zeros_r], axis=1),
                                         jnp.concatenate([zeros_r, s1["rhs"]], axis=1)], axis=0))
        uw += [both[:, :2 * LANES], both[:, 2 * LANES:]]
        s0["attn"] = s0["attn2"][:, :c]
        s1["attn"] = s0["attn2"][:, c:]

    by_key = {it: (s, y) for it, s, y in zip(items, st, uw)}
    chains = [(hh, dr) for hh in range(SCAN_HEADS) for dr in range(2)]
    state = {ch: s_scr[ch[0], ch[1]] for ch in chains}
    for i in range(n_chunks):
        cur = {ch: by_key[(ch[0], ch[1], i if ch[1] == 0 else n_chunks - 1 - i)] for ch in chains}
        wsqs = {ch: _bdot(jnp.concatenate([cur[ch][1][:, LANES:].astype(BF16), cur[ch][0]["q_dec"]], axis=0),
                          state[ch]) for ch in chains}
        v_new = {ch: cur[ch][1][:, :LANES] - wsqs[ch][:c] for ch in chains}
        for ch in chains:
            s = cur[ch][0]
            refs[ch[1]][5][s["rs"], s["ls"]] = wsqs[ch][c:] + _bdot(s["attn"], v_new[ch])
        state = {ch: state[ch] * cur[ch][0]["e_last"] + _dot_tn(cur[ch][0]["k_dec"], v_new[ch])
                 for ch in chains}
    for ch in chains:
        s_scr[ch[0], ch[1]] = state[ch]

    @pl.when(step == n_steps - 1)
    def _():
        sfin_ref[...] = s_scr[...]


def _dn_scan(qn, kn, vv, gb, gbt, s0, batch, seq):
    rows, d = qn.shape
    tt = SCAN_TILE
    n_steps = seq // tt
    n_groups = DN_HEADS // SCAN_HEADS
    w = SCAN_HEADS * LANES
    fwd_t = lambda b, s: b * n_steps + s
    bwd_t = lambda b, s: b * n_steps + n_steps - 1 - s

    def specs(tile):
        wide = pl.BlockSpec((tt, w), lambda b, g, s: (tile(b, s), g))
        return wide, [wide, wide, wide,
                      pl.BlockSpec((tt, LANES), lambda b, g, s: (tile(b, s), 0)),
                      pl.BlockSpec((LANES, tt), lambda b, g, s: (0, tile(b, s)))]

    out_f, in_f = specs(fwd_t)
    out_b, in_b = specs(bwd_t)
    st_spec = pl.BlockSpec((None, SCAN_HEADS, 2, LANES, LANES), lambda b, g, s: (b, g, 0, 0, 0))
    return pl.pallas_call(
        functools.partial(_scan_kernel, n_steps),
        grid=(batch, n_groups, n_steps),
        in_specs=in_f + in_b + [st_spec],
        out_specs=[out_f, out_b, st_spec],
        out_shape=[jax.ShapeDtypeStruct((rows, d), F32), jax.ShapeDtypeStruct((rows, d), F32),
                   jax.ShapeDtypeStruct((batch, DN_HEADS, 2, LANES, LANES), F32)],
        scratch_shapes=[pltpu.VMEM((SCAN_HEADS, 2, LANES, LANES), F32)],
        compiler_params=_cparams(("arbitrary", "arbitrary", "arbitrary")),
        name="dn_scan",
    )(qn, kn, vv, gb, gbt, qn, kn, vv, gb, gbt, s0)


def _na_kernel(n_rows, q_ref, k_ref, v_ref, kc_ref, vc_ref, bias_ref, o_ref):
    t = pl.program_id(2)
    kw = NA_KROWS * GRID_W
    ks = jnp.clip(t * NA_QROWS - NA_WIN_R // 2, 0, n_rows - NA_KROWS)
    start = pl.multiple_of(ks * GRID_W, GRID_W)
    q = q_ref[...] * (NA_DH ** -0.5)
    kwin = k_ref[pl.ds(start, kw), :]
    vwin = v_ref[pl.ds(start, kw), :]
    kc = kc_ref[...]
    vc = vc_ref[...]
    lane = lax.broadcasted_iota(jnp.int32, q.shape, 1)
    outs = []
    for hh in range(2):
        sel = (lane < NA_DH) if hh == 0 else (lane >= NA_DH)
        qh = jnp.where(sel, q, jnp.zeros_like(q))
        s_loc = lax.dot_general(qh, kwin, (((1,), (1,)), ((), ())),
                                preferred_element_type=F32) + bias_ref[hh]
        s_ctx = lax.dot_general(qh, kc, (((1,), (1,)), ((), ())), preferred_element_type=F32)
        m = jnp.maximum(jnp.max(s_loc, axis=1, keepdims=True), jnp.max(s_ctx, axis=1, keepdims=True))
        p_loc = jnp.exp(s_loc - m)
        p_ctx = jnp.exp(s_ctx - m)
        denom = jnp.sum(p_loc, axis=1, keepdims=True) + jnp.sum(p_ctx, axis=1, keepdims=True)
        o = (jnp.dot(p_loc.astype(BF16), vwin, preferred_element_type=F32)
             + jnp.dot(p_ctx.astype(BF16), vc, preferred_element_type=F32)) / denom
        outs.append(o)
    o_ref[...] = jnp.where(lane < NA_DH, outs[0], outs[1]).astype(BF16)


def _na_attention(p, pc, bias, batch, seq, ctx_len, q_col, k_col, v_col):
    rows = p.shape[0]
    n_rows = seq // GRID_W
    qt = NA_QROWS * GRID_W
    n_tiles = n_rows // NA_QROWS
    kw = NA_KROWS * GRID_W
    n_pairs = NA_HEADS // 2

    def geom(t):
        return jnp.where(t == 0, 0, jnp.where(t == n_tiles - 1, 2, 1))

    return pl.pallas_call(
        functools.partial(_na_kernel, n_rows),
        grid=(batch, n_pairs, n_tiles),
        in_specs=[pl.BlockSpec((qt, LANES), lambda b, pr, t: (b * n_tiles + t, q_col + pr)),
                  pl.BlockSpec((seq, LANES), lambda b, pr, t: (b, k_col + pr)),
                  pl.BlockSpec((seq, LANES), lambda b, pr, t: (b, v_col + pr)),
                  pl.BlockSpec((ctx_len, LANES), lambda b, pr, t: (b, k_col + pr)),
                  pl.BlockSpec((ctx_len, LANES), lambda b, pr, t: (b, v_col + pr)),
                  pl.BlockSpec((None, 2, qt, kw), lambda b, pr, t: (geom(t), pr, 0, 0))],
        out_specs=pl.BlockSpec((qt, LANES), lambda b, pr, t: (b * n_tiles + t, pr)),
        out_shape=jax.ShapeDtypeStruct((rows, n_pairs * LANES), BF16),
        compiler_params=_cparams(("arbitrary", "arbitrary", "arbitrary")),
        name="na_attention",
    )(p, p, p, pc, pc, bias)


def _ctx_attn_kernel(q_ref, k_ref, v_ref, o_ref):
    q = q_ref[...] * (NA_DH ** -0.5)
    k = k_ref[...]
    v = v_ref[...]
    lane = lax.broadcasted_iota(jnp.int32, q.shape, 1)
    outs = []
    for hh in range(2):
        sel = (lane < NA_DH) if hh == 0 else (lane >= NA_DH)
        qh = jnp.where(sel, q, jnp.zeros_like(q))
        s = lax.dot_general(qh, k, (((1,), (1,)), ((), ())), preferred_element_type=F32)
        pm = jnp.exp(s - jnp.max(s, axis=1, keepdims=True))
        outs.append(jnp.dot(pm.astype(BF16), v, preferred_element_type=F32)
                    / jnp.sum(pm, axis=1, keepdims=True))
    o_ref[...] = jnp.where(lane < NA_DH, outs[0], outs[1]).astype(BF16)


def _ctx_attention(pc, batch, ctx_len, q_col, k_col, v_col):
    n_pairs = NA_HEADS // 2
    return pl.pallas_call(
        _ctx_attn_kernel,
        grid=(batch, n_pairs),
        in_specs=[pl.BlockSpec((ctx_len, LANES), lambda b, pr: (b, q_col + pr)),
                  pl.BlockSpec((ctx_len, LANES), lambda b, pr: (b, k_col + pr)),
                  pl.BlockSpec((ctx_len, LANES), lambda b, pr: (b, v_col + pr))],
        out_specs=pl.BlockSpec((ctx_len, LANES), lambda b, pr: (b, pr)),
        out_shape=jax.ShapeDtypeStruct((pc.shape[0], n_pairs * LANES), BF16),
        compiler_params=_cparams(("arbitrary", "arbitrary")),
        name="ctx_attention",
    )(pc, pc, pc)


def _na_bias_tables(rpb, n_rows):
    n_tiles = n_rows // NA_QROWS
    n_roff = 2 * NA_WIN_R - 1
    n_coff = 2 * NA_WIN_W - 1
    col = np.arange(GRID_W)
    c0 = np.clip(col - NA_WIN_W // 2, 0, GRID_W - NA_WIN_W)
    col_in = (col[None, :] >= c0[:, None]) & (col[None, :] < c0[:, None] + NA_WIN_W)
    coff = np.clip(col[None, :] - col[:, None] + (NA_WIN_W - 1), 0, n_coff - 1)
    col_sel = (coff[:, :, None] == np.arange(n_coff)).astype(np.float32)
    roff_tab = np.zeros((3, NA_QROWS * NA_KROWS), np.int32)
    for g, t in enumerate((0, 1, n_tiles - 1)):
        rs = t * NA_QROWS
        ks = min(max(rs - NA_WIN_R // 2, 0), n_rows - NA_KROWS)
        qrow = rs + np.arange(NA_QROWS)
        krow = ks + np.arange(NA_KROWS)
        r0 = np.clip(qrow - NA_WIN_R // 2, 0, n_rows - NA_WIN_R)
        row_in = (krow[None, :] >= r0[:, None]) & (krow[None, :] < r0[:, None] + NA_WIN_R)
        roff = krow[None, :] - qrow[:, None] + (NA_WIN_R - 1)
        roff_tab[g] = np.where(row_in, roff, -1).reshape(-1)
    by_col = jnp.einsum("hrc,qkc->hrqk", rpb, jnp.asarray(col_sel), precision=lax.Precision.HIGHEST)
    by_col = jnp.where(jnp.asarray(col_in)[None, None], by_col, -jnp.inf)
    n_heads = rpb.shape[0]
    qt, kw = NA_QROWS * GRID_W, NA_KROWS * GRID_W
    return pl.pallas_call(
        _bias_kernel,
        grid_spec=pltpu.PrefetchScalarGridSpec(
            num_scalar_prefetch=1,
            grid=(3, n_heads),
            in_specs=[pl.BlockSpec((None, n_roff, GRID_W, GRID_W), lambda g, h, ro: (h, 0, 0, 0))],
            out_specs=pl.BlockSpec((None, None, qt, kw), lambda g, h, ro: (g, h, 0, 0))),
        out_shape=jax.ShapeDtypeStruct((3, n_heads, qt, kw), F32),
        compiler_params=_cparams(("arbitrary", "arbitrary")),
        name="na_bias_table",
    )(jnp.asarray(roff_tab), by_col)


def _bias_kernel(roff_ref, bycol_ref, o_ref):
    g = pl.program_id(0)
    for a in range(NA_QROWS):
        for i in range(NA_KROWS):
            r = roff_ref[g, a * NA_KROWS + i]
            blk = bycol_ref[jnp.maximum(r, 0)]
            o_ref[a * GRID_W:(a + 1) * GRID_W, i * GRID_W:(i + 1) * GRID_W] = jnp.where(r >= 0, blk, -jnp.inf)


def _merge_kernel(of_ref, ob_ref, z_ref, na_ref, gd_ref, gn_ref, x_ref, dnw_ref, wpa_ref, wpb_ref,
                  wout_ref, gpost_ref, g1_ref, o_ref, dn_scr):
    o = of_ref[...] + ob_ref[...]
    z = z_ref[...].astype(F32)
    for hh in range(DN_HEADS):
        sl = slice(hh * LANES, (hh + 1) * LANES)
        oh = o[:, sl]
        oh = oh * lax.rsqrt(jnp.mean(oh * oh, axis=-1, keepdims=True) + EPS) * dnw_ref[...]
        zh = z[:, sl]
        dn_scr[:, sl] = (oh * (zh * jax.nn.sigmoid(zh))).astype(BF16)
    y = (jax.nn.sigmoid(gd_ref[...].astype(F32)) * jnp.dot(dn_scr[...], wpa_ref[...], preferred_element_type=F32)
         + jax.nn.sigmoid(gn_ref[...].astype(F32)) * _bdot(na_ref[...], wpb_ref[...]))
    out = _bdot(y, wout_ref[...])
    o_ref[...] = x_ref[...] + g1_ref[...] * _rms(out, gpost_ref[...])


def _merge(o_f, o_b, p, na_o, x2, dn_norm, w_pa, w_pb, w_out, gpost, mod, mod_row, g1_blk, tm):
    rows, d = x2.shape
    nw = na_o.shape[1]
    row_blk = lambda c: pl.BlockSpec((tm, d), lambda i: (i, c))
    const = lambda shape: pl.BlockSpec(shape, lambda i: (0,) * len(shape))
    return pl.pallas_call(
        _merge_kernel,
        grid=(rows // tm,),
        in_specs=[row_blk(0), row_blk(0), row_blk(3), pl.BlockSpec((tm, nw), lambda i: (i, 0)),
                  row_blk(4), row_blk(5), row_blk(0),
                  const((1, LANES)), const((d, d)), const((nw, d)), const((d, d)), const((1, d)),
                  pl.BlockSpec((None, 1, d), lambda i: (mod_row(i), 0, g1_blk))],
        out_specs=row_blk(0),
        out_shape=jax.ShapeDtypeStruct((rows, d), F32),
        scratch_shapes=[pltpu.VMEM((tm, d), BF16)],
        compiler_params=_cparams(("arbitrary",)),
        name="merge",
    )(o_f, o_b, p, na_o, p, p, x2, dn_norm, w_pa, w_pb, w_out, gpost, mod)


def _ffn_kernel(n_f, x_ref, gpre_ref, sc_ref, sh_ref, w1_ref, w3_ref, w2_ref, gpost_ref, g2_ref,
                o_ref, h_scr, acc_scr):
    j = pl.program_id(1)

    @pl.when(j == 0)
    def _():
        h = _rms(x_ref[...], gpre_ref[...]) * (1.0 + sc_ref[...]) + sh_ref[...]
        h_scr[...] = h.astype(BF16)
        acc_scr[...] = jnp.zeros_like(acc_scr)

    h = h_scr[...]
    a = jnp.dot(h, w1_ref[...], preferred_element_type=F32)
    b = jnp.dot(h, w3_ref[...], preferred_element_type=F32)
    acc_scr[...] += _bdot(a * jax.nn.sigmoid(a) * b, w2_ref[...])

    @pl.when(j == n_f - 1)
    def _():
        o_ref[...] = x_ref[...] + g2_ref[...] * _rms(acc_scr[...], gpost_ref[...])


def _dense_ffn(x2, gpre, gpost, mod, mod_row, w1, w3, w2, tm, tf):
    rows, d = x2.shape
    f = w1.shape[1]
    n_f = f // tf
    modspec = lambda blk: pl.BlockSpec((None, 1, d), lambda i, j: (mod_row(i), 0, blk))
    return pl.pallas_call(
        functools.partial(_ffn_kernel, n_f),
        grid=(rows // tm, n_f),
        in_specs=[pl.BlockSpec((tm, d), lambda i, j: (i, 0)),
                  pl.BlockSpec((1, d), lambda i, j: (0, 0)),
                  modspec(4), modspec(3),
                  pl.BlockSpec((d, tf), lambda i, j: (0, j)),
                  pl.BlockSpec((d, tf), lambda i, j: (0, j)),
                  pl.BlockSpec((tf, d), lambda i, j: (j, 0)),
                  pl.BlockSpec((1, d), lambda i, j: (0, 0)),
                  modspec(5)],
        out_specs=pl.BlockSpec((tm, d), lambda i, j: (i, 0)),
        out_shape=jax.ShapeDtypeStruct((rows, d), F32),
        scratch_shapes=[pltpu.VMEM((tm, d), BF16), pltpu.VMEM((tm, d), F32)],
        compiler_params=_cparams(("arbitrary", "arbitrary")),
        name="dense_ffn",
    )(x2, gpre, mod, mod, w1, w3, w2, gpost, mod)


def _router_kernel(x_ref, gpre_ref, sc_ref, sh_ref, r_ref, h_ref, gate_ref):
    h = _rms(x_ref[...], gpre_ref[...]) * (1.0 + sc_ref[...]) + sh_ref[...]
    h_ref[...] = h
    logits = jnp.dot(h, r_ref[...], precision=lax.Precision.HIGHEST, preferred_element_type=F32)
    lane = lax.broadcasted_iota(jnp.int32, logits.shape, 1)
    neg = -jnp.inf
    l1 = jnp.where(lane < N_EXPERTS, logits, neg)
    m1 = jnp.max(l1, axis=1, keepdims=True)
    i1 = jnp.min(jnp.where(l1 == m1, lane, LANES), axis=1, keepdims=True)
    l2 = jnp.where(lane == i1, neg, l1)
    m2 = jnp.max(l2, axis=1, keepdims=True)
    i2 = jnp.min(jnp.where(l2 == m2, lane, LANES), axis=1, keepdims=True)
    e = jnp.exp(m2 - m1)
    w1 = 1.0 / (1.0 + e)
    w2 = e / (1.0 + e)
    out = jnp.where(lane == 0, i1.astype(F32), 0.0)
    out = jnp.where(lane == 1, i2.astype(F32), out)
    out = jnp.where(lane == 2, w1, out)
    out = jnp.where(lane == 3, w2, out)
    gate_ref[...] = out


def _router(x2, gpre, mod, mod_row, router_pad, tm):
    rows, d = x2.shape
    modspec = lambda blk: pl.BlockSpec((None, 1, d), lambda i: (mod_row(i), 0, blk))
    return pl.pallas_call(
        _router_kernel,
        grid=(rows // tm,),
        in_specs=[pl.BlockSpec((tm, d), lambda i: (i, 0)),
                  pl.BlockSpec((1, d), lambda i: (0, 0)),
                  modspec(4), modspec(3),
                  pl.BlockSpec((d, LANES), lambda i: (0, 0))],
        out_specs=[pl.BlockSpec((tm, d), lambda i: (i, 0)),
                   pl.BlockSpec((tm, LANES), lambda i: (i, 0))],
        out_shape=[jax.ShapeDtypeStruct((rows, d), F32),
                   jax.ShapeDtypeStruct((rows, LANES), F32)],
        compiler_params=_cparams(("arbitrary",)),
        name="moe_router",
    )(x2, gpre, mod, mod, router_pad)


def _gather_row(h_hbm, xbuf, sem, slot, r, tok):
    return pltpu.make_async_copy(h_hbm.at[pl.ds(tok, 1), :], xbuf.at[slot, pl.ds(r, 1), :], sem.at[slot])


def _scatter_row(stage, out_hbm, sem, slot, r, dst):
    return pltpu.make_async_copy(stage.at[slot, pl.ds(r, 1), :], out_hbm.at[pl.ds(dst, 1), :], sem.at[slot])


def _expert_kernel(n_f, n_tiles, te_ref, nv_ref, tok_ref, tokn_ref, dstp_ref, dst_ref, h_hbm,
                   w1_ref, w3_ref, w2_ref, out_hbm, xbuf, xb16, acc_scr, stage, gsem, ssem):
    i = pl.program_id(0)
    j = pl.program_id(1)
    slot = i % 2
    rows_per_step = MOE_TM // n_f

    def wait_gathers(sl):
        pltpu.make_async_copy(h_hbm.at[pl.ds(0, MOE_TM), :], xbuf.at[sl], gsem.at[sl]).wait()

    def wait_scatters(sl):
        pltpu.make_async_copy(stage.at[sl], out_hbm.at[pl.ds(0, MOE_TM), :], ssem.at[sl]).wait()

    @pl.when((i == 0) & (j == 0))
    def _():
        stage[...] = jnp.zeros_like(stage)

        def body(r, carry):
            _gather_row(h_hbm, xbuf, gsem, slot, r, tok_ref[0, r]).start()
            return carry
        lax.fori_loop(0, MOE_TM, body, 0, unroll=8)

    @pl.when(j == 0)
    def _():
        wait_gathers(slot)
        xb16[...] = xbuf[slot].astype(BF16)
        acc_scr[...] = jnp.zeros_like(acc_scr)

    def row_dmas():
        for k in range(rows_per_step):
            r = pl.multiple_of(j * rows_per_step, SUBLANES) + k
            _gather_row(h_hbm, xbuf, gsem, 1 - slot, r, tokn_ref[0, r]).start()
            _scatter_row(stage, out_hbm, ssem, 1 - slot, r, dstp_ref[0, r]).start()

    valid = i < nv_ref[0]

    @pl.when(valid)
    def _():
        row_dmas()
        x = xb16[...]
        a = jnp.dot(x, w1_ref[...], preferred_element_type=F32)
        b = jnp.dot(x, w3_ref[...], preferred_element_type=F32)
        acc_scr[...] += _bdot(a * jax.nn.sigmoid(a) * b, w2_ref[...])

    @pl.when(jnp.logical_not(valid))
    def _():
        row_dmas()

    @pl.when(j == n_f - 1)
    def _():
        @pl.when(i >= 1)
        def _():
            wait_scatters(slot)

        stage[slot] = acc_scr[...]

        @pl.when(i == n_tiles - 1)
        def _():
            wait_scatters(1 - slot)
            wait_gathers(1 - slot)

            def body(r, carry):
                _scatter_row(stage, out_hbm, ssem, slot, r, dst_ref[0, r]).start()
                return carry
            lax.fori_loop(0, MOE_TM, body, 0, unroll=8)
            wait_scatters(slot)


def _expert_ffn(h, row_token, out_row, tile_expert, n_valid, w1, w3, w2, tf):
    d = h.shape[1]
    prow = row_token.shape[0]
    f = w1.shape[2]
    n_f = f // tf
    n_tiles = prow // MOE_TM
    assert f % tf == 0 and MOE_TM % (n_f * SUBLANES) == 0
    idx_spec =lambda fn: pl.BlockSpec((None, 1, MOE_TM), lambda i, j, te, nv: (fn(i), 0, 0),
                                       memory_space=pltpu.SMEM)
    tok3 = row_token.reshape(n_tiles, 1, MOE_TM)
    dst3 = jnp.concatenate([prow + jnp.arange(MOE_TM, dtype=jnp.int32), out_row]).reshape(n_tiles + 1, 1, MOE_TM)
    grid_spec = pltpu.PrefetchScalarGridSpec(
        num_scalar_prefetch=2,
        grid=(n_tiles, n_f),
        in_specs=[idx_spec(lambda i: i),
                  idx_spec(lambda i: jnp.minimum(i + 1, n_tiles - 1)),
                  idx_spec(lambda i: i),
                  idx_spec(lambda i: i + 1),
                  pl.BlockSpec(memory_space=pl.ANY),
                  pl.BlockSpec((None, d, tf), lambda i, j, te, nv: (te[i], 0, j)),
                  pl.BlockSpec((None, d, tf), lambda i, j, te, nv: (te[i], 0, j)),
                  pl.BlockSpec((None, tf, d), lambda i, j, te, nv: (te[i], j, 0))],
        out_specs=pl.BlockSpec(memory_space=pl.ANY),
        scratch_shapes=[pltpu.VMEM((2, MOE_TM, d), F32), pltpu.VMEM((MOE_TM, d), BF16),
                        pltpu.VMEM((MOE_TM, d), F32), pltpu.VMEM((2, MOE_TM, d), F32),
                        pltpu.SemaphoreType.DMA((2,)), pltpu.SemaphoreType.DMA((2,))],
    )
    return pl.pallas_call(
        functools.partial(_expert_kernel, n_f, n_tiles),
        grid_spec=grid_spec,
        out_shape=jax.ShapeDtypeStruct((prow + MOE_TM, d), F32),
        compiler_params=_cparams(("arbitrary", "arbitrary")),
        name="moe_experts",
    )(tile_expert, n_valid, tok3, tok3, dst3, dst3, h, w1, w3, w2)


def _combine_kernel(y1_ref, y2_ref, gate_ref, x_ref, gpost_ref, g2_ref, o_ref):
    gt = gate_ref[...]
    lane = lax.broadcasted_iota(jnp.int32, gt.shape, 1)
    w1 = jnp.sum(jnp.where(lane == 2, gt, 0.0), axis=1, keepdims=True)
    w2 = jnp.sum(jnp.where(lane == 3, gt, 0.0), axis=1, keepdims=True)
    y = w1 * y1_ref[...] + w2 * y2_ref[...]
    o_ref[...] = x_ref[...] + g2_ref[...] * _rms(y, gpost_ref[...])


def _moe_combine(ys, gates, x2, gpost, mod, mod_row, tm):
    rows, d = x2.shape
    rb = pl.BlockSpec((tm, d), lambda i: (i, 0))
    return pl.pallas_call(
        _combine_kernel,
        grid=(rows // tm,),
        in_specs=[rb, pl.BlockSpec((tm, d), lambda i: (rows // tm + i, 0)),
                  pl.BlockSpec((tm, LANES), lambda i: (i, 0)), rb,
                  pl.BlockSpec((1, d), lambda i: (0, 0)),
                  pl.BlockSpec((None, 1, d), lambda i: (mod_row(i), 0, 5))],
        out_specs=rb,
        out_shape=jax.ShapeDtypeStruct((rows, d), F32),
        compiler_params=_cparams(("arbitrary",)),
        name="moe_combine",
    )(ys, ys, gates, x2, gpost, mod)


def _moe_schedule(gates, n_tokens):
    idx = gates[:, 0:2].astype(jnp.int32)
    flat_e = idx.reshape(-1)
    onehot = (flat_e[:, None] == jnp.arange(N_EXPERTS)[None, :]).astype(jnp.int32)
    csum = jnp.cumsum(onehot, axis=0)
    counts = csum[-1]
    rank = jnp.sum(csum * onehot, axis=1) - 1
    padded = ((counts + MOE_TM - 1) // MOE_TM) * MOE_TM
    ends = jnp.cumsum(padded)
    starts = ends - padded
    dest = jnp.sum(starts[None, :] * onehot, axis=1) + rank
    n_pairs = 2 * n_tokens
    n_rows = n_pairs + N_EXPERTS * MOE_TM
    n_tiles = n_rows // MOE_TM
    pair_of_row = jnp.full((n_rows,), -1, jnp.int32).at[dest].set(jnp.arange(n_pairs, dtype=jnp.int32))
    is_pad = pair_of_row < 0
    row_token = jnp.where(is_pad, 0, pair_of_row // 2)
    out_row = jnp.where(is_pad, n_pairs - 1 + jnp.cumsum(is_pad.astype(jnp.int32)),
                        (pair_of_row % 2) * n_tokens + pair_of_row // 2)
    tile_start = jnp.arange(n_tiles, dtype=jnp.int32) * MOE_TM
    tile_expert = jnp.minimum(jnp.sum((ends[None, :] <= tile_start[:, None]).astype(jnp.int32), axis=1),
                              N_EXPERTS - 1)
    n_valid = (ends[-1] // MOE_TM).astype(jnp.int32).reshape(1)
    return row_token, out_row, tile_expert, n_valid


def _rope_tables(seq):
    t = jnp.arange(seq)
    row = (t // GRID_W).astype(F32)
    col = (t % GRID_W).astype(F32)
    n_freq = DN_DK // 4
    inv = ROPE_BASE ** (-jnp.arange(n_freq, dtype=F32) / n_freq)
    ang = jnp.concatenate([row[:, None] * inv, col[:, None] * inv], axis=-1)
    cos, sin = jnp.cos(ang), jnp.sin(ang)
    return jnp.concatenate([cos, cos], axis=-1), jnp.concatenate([-sin, sin], axis=-1)


def kernel(x, c, ctx, c_ctx, ada_w, ada_b, norm_mix_pre, norm_mix_post, norm_ffn_pre, norm_ffn_post,
           w_in, dn_conv, dn_a_log, dn_dt_bias, dn_norm, na_rpb, w_branch_dn, w_branch_na, w_out,
           ffn_w1, ffn_w3, ffn_w2, moe_router, moe_w1, moe_w3, moe_w2):
    batch, seq, d = x.shape
    ctx_len = ctx.shape[1]
    depth = w_in.shape[0]
    nh = DN_HEADS
    dn_w = nh * DN_DK
    na_w = NA_HEADS * NA_DH
    n_rows = seq // GRID_W
    assert d == dn_w and seq % SCAN_TILE == 0 and ctx_len % SCAN_TILE == 0 and n_rows % NA_QROWS == 0
    assert depth <= 2, "context tokens only take the dense FFN path"

    c_rows = jnp.zeros((SUBLANES, d), F32).at[:batch].set(c).at[batch].set(c_ctx)
    mod_all = _mod_vectors(c_rows, ada_w, ada_b)
    cos2, sin2 = _rope_tables(seq)
    ones_t = jnp.ones((SCAN_TILE, LANES), F32)

    x2 = x.reshape(batch * seq, d)
    xc2 = ctx.reshape(batch * ctx_len, d)
    lat_tm = 1024
    lat_row = lambda tm: (lambda i: i // (seq // tm))
    ctx_row = lambda i: batch

    q_col, k_col, v_col = (4 * dn_w + 2 * d) // LANES, (4 * dn_w + 2 * d + na_w) // LANES, \
        (4 * dn_w + 2 * d + 2 * na_w) // LANES

    for l in range(depth):
        last = l == depth - 1
        mod = mod_all[l].reshape(SUBLANES, 1, 6 * d)
        wl = w_in[l]
        o_ab = 4 * dn_w
        o_na = o_ab + 4 * nh
        o_gate = o_na + 3 * na_w
        w_main = jnp.concatenate([wl[:, :o_ab], wl[:, o_gate:], wl[:, o_na:o_gate]], axis=1).astype(BF16)
        wab = jnp.pad(wl[:, o_ab:o_na], ((0, 0), (0, LANES - 4 * nh)))
        wab_hi = wab.astype(BF16)
        wab_lo = (wab - wab_hi.astype(F32)).astype(BF16)
        gpre = norm_mix_pre[l].reshape(1, d)
        gpost = norm_mix_post[l].reshape(1, d)

        p, ab = _in_proj(x2, gpre, mod, lat_row(lat_tm), 1, 0, w_main, wab_hi, wab_lo, lat_tm, INPROJ_TN)
        pc, abc = _in_proj(xc2, gpre, mod, ctx_row, 1, 0, w_main, wab_hi, wab_lo, batch * ctx_len, INPROJ_TN)

        conv_w3 = jnp.pad(dn_conv[l].T.reshape(DN_CONV, 3, dn_w).transpose(1, 0, 2),
                          ((0, 0), (0, SUBLANES - DN_CONV), (0, 0)))
        gpar = jnp.zeros((SUBLANES, LANES), F32)
        gpar = gpar.at[0, :2 * nh].set(-jnp.exp(dn_a_log[l].reshape(-1)))
        gpar = gpar.at[1, :2 * nh].set(dn_dt_bias[l].reshape(-1))

        qc_, kc_, vc_, gbc, gbtc = _dn_prep(pc, abc, conv_w3, gpar, ones_t, ones_t, batch, ctx_len, False)
        ql_, kl_, vl_, gbl, gbtl = _dn_prep(p, ab, conv_w3, gpar, cos2, sin2, batch, seq, True)
        s0 = jnp.zeros((batch, nh, 2, LANES, LANES), F32)
        oc_f, oc_b, s_ctx = _dn_scan(qc_, kc_, vc_, gbc, gbtc, s0, batch, ctx_len)
        ol_f, ol_b, _ = _dn_scan(ql_, kl_, vl_, gbl, gbtl, s_ctx, batch, seq)

        bias = _na_bias_tables(na_rpb[l], n_rows)
        na_lat = _na_attention(p, pc, bias, batch, seq, ctx_len, q_col, k_col, v_col)

        dnw = dn_norm[l].reshape(1, LANES)
        w_pa = w_branch_dn[l].astype(BF16)
        w_pb = w_branch_na[l].astype(BF16)
        w_o = w_out[l].astype(BF16)
        x2 = _merge(ol_f, ol_b, p, na_lat, x2, dnw, w_pa, w_pb, w_o, gpost, mod, lat_row(512), 2, 512)

        gfpre = norm_ffn_pre[l].reshape(1, d)
        gfpost = norm_ffn_post[l].reshape(1, d)
        if l % 2 == 0:
            w1 = ffn_w1[l // 2].astype(BF16)
            w3 = ffn_w3[l // 2].astype(BF16)
            w2 = ffn_w2[l // 2].astype(BF16)
            tf = w1.shape[1] // 2
            x2 = _dense_ffn(x2, gfpre, gfpost, mod, lat_row(512), w1, w3, w2, 512, tf)
        else:
            rpad = jnp.pad(moe_router[l // 2], ((0, 0), (0, LANES - N_EXPERTS)))
            hb, gates = _router(x2, gfpre, mod, lat_row(512), rpad, 512)
            n_tok = batch * seq
            row_token, out_row, tile_expert, n_valid = _moe_schedule(gates, n_tok)
            ys = _expert_ffn(hb, row_token, out_row, tile_expert, n_valid, moe_w1[l // 2].astype(BF16),
                             moe_w3[l // 2].astype(BF16), moe_w2[l // 2].astype(BF16), 896)
            x2 = _moe_combine(ys, gates, x2, gfpost, mod, lat_row(512), 512)

        if not last:
            na_ctx = _ctx_attention(pc, batch, ctx_len, q_col, k_col, v_col)
            xc2 = _merge(oc_f, oc_b, pc, na_ctx, xc2, dnw, w_pa, w_pb, w_o, gpost, mod, ctx_row, 2, 256)
            xc2 = _dense_ffn(xc2, gfpre, gfpost, mod, ctx_row, w1, w3, w2, 512, tf)
    return x2.reshape(batch, seq, d)
```

```python
import functools

import numpy as np
import jax
import jax.numpy as jnp
from jax import lax
from jax.experimental import pallas as pl
from jax.experimental.pallas import tpu as pltpu

F32 = jnp.float32
BF16 = jnp.bfloat16

GRID_W = 64
DN_HEADS = 8
DN_DK = 128
DN_CONV = 5
DN_CHUNK = 64
NA_HEADS = 8
NA_DH = 64
NA_WIN_R = 8
NA_WIN_W = 16
ROPE_BASE = 10000.0
N_EXPERTS = 8
EPS = 1e-6

LANES = 128
SUBLANES = 8
BF16_SUBLANES = 16
VMEM_LIMIT = 56 * 1024 * 1024

SCAN_TILE = 256
SCAN_HEADS = 8
NA_QROWS = 4
NA_KROWS = NA_QROWS + 8
MOE_TM = 512
MOE_TF = 1792
INPROJ_TN = 1920


def _cparams(sem):
    return pltpu.CompilerParams(dimension_semantics=sem, vmem_limit_bytes=VMEM_LIMIT)


def _bdot(a, b):
    return jnp.dot(a.astype(BF16), b.astype(BF16), preferred_element_type=F32)


def _dot_nt(a, b):
    return lax.dot_general(a.astype(BF16), b.astype(BF16), (((1,), (1,)), ((), ())),
                           preferred_element_type=F32)


def _dot_tn(a, b):
    return lax.dot_general(a.astype(BF16), b.astype(BF16), (((0,), (0,)), ((), ())),
                           preferred_element_type=F32)


def _split3(x):
    hi = x.astype(BF16)
    r = x - hi.astype(F32)
    mid = r.astype(BF16)
    lo = (r - mid.astype(F32)).astype(BF16)
    return hi, mid, lo


def _rms(x, gain):
    return x * lax.rsqrt(jnp.mean(x * x, axis=-1, keepdims=True) + EPS) * gain


def _mod_kernel(c_ref, w_ref, b_ref, o_ref):
    c = c_ref[...]
    s = c * jax.nn.sigmoid(c)
    o_ref[0] = jnp.dot(s, w_ref[0], precision=lax.Precision.HIGHEST,
                       preferred_element_type=F32) + b_ref[0]


def _mod_vectors(c_rows, ada_w, ada_b):
    depth, d, n = ada_w.shape
    tn = 1536
    return pl.pallas_call(
        _mod_kernel,
        grid=(depth, n // tn),
        in_specs=[pl.BlockSpec((SUBLANES, d), lambda l, j: (0, 0)),
                  pl.BlockSpec((1, d, tn), lambda l, j: (l, 0, j)),
                  pl.BlockSpec((1, 1, tn), lambda l, j: (l, 0, j))],
        out_specs=pl.BlockSpec((1, SUBLANES, tn), lambda l, j: (l, 0, j)),
        out_shape=jax.ShapeDtypeStruct((depth, SUBLANES, n), F32),
        compiler_params=_cparams(("arbitrary", "arbitrary")),
        name="mod_vectors",
    )(c_rows, ada_w, ada_b.reshape(depth, 1, n))


def _inproj_kernel(x_ref, g_ref, sc_ref, sh_ref, w_ref, wab_hi_ref, wab_lo_ref,
                   o_ref, ab_ref, h_scr, hlo_scr):
    j = pl.program_id(1)

    @pl.when(j == 0)
    def _():
        h = _rms(x_ref[...], g_ref[...]) * (1.0 + sc_ref[...]) + sh_ref[...]
        hi = h.astype(BF16)
        lo = (h - hi.astype(F32)).astype(BF16)
        h_scr[...] = hi
        hlo_scr[...] = lo
        ab_ref[...] = (jnp.dot(hi, wab_hi_ref[...], preferred_element_type=F32)
                       + jnp.dot(lo, wab_hi_ref[...], preferred_element_type=F32)
                       + jnp.dot(hi, wab_lo_ref[...], preferred_element_type=F32))

    o_ref[...] = jnp.dot(h_scr[...], w_ref[...], preferred_element_type=F32).astype(BF16)


def _in_proj(x2, gain, mod, mod_row, sc_blk, sh_blk, w_main, wab_hi, wab_lo, tm, tn):
    rows, d = x2.shape
    n = w_main.shape[1]
    return pl.pallas_call(
        _inproj_kernel,
        grid=(rows // tm, n // tn),
        in_specs=[pl.BlockSpec((tm, d), lambda i, j: (i, 0)),
                  pl.BlockSpec((1, d), lambda i, j: (0, 0)),
                  pl.BlockSpec((None, 1, d), lambda i, j: (mod_row(i), 0, sc_blk)),
                  pl.BlockSpec((None, 1, d), lambda i, j: (mod_row(i), 0, sh_blk)),
                  pl.BlockSpec((d, tn), lambda i, j: (0, j)),
                  pl.BlockSpec((d, LANES), lambda i, j: (0, 0)),
                  pl.BlockSpec((d, LANES), lambda i, j: (0, 0))],
        out_specs=[pl.BlockSpec((tm, tn), lambda i, j: (i, j)),
                   pl.BlockSpec((tm, LANES), lambda i, j: (i, 0))],
        out_shape=[jax.ShapeDtypeStruct((rows, n), BF16),
                   jax.ShapeDtypeStruct((rows, LANES), F32)],
        scratch_shapes=[pltpu.VMEM((tm, d), BF16), pltpu.VMEM((tm, d), BF16)],
        compiler_params=_cparams(("arbitrary", "arbitrary")),
        name="in_proj",
    )(x2, gain, mod, mod, w_main, wab_hi, wab_lo)


def _prep_kernel(rope, n_tiles,
                 q_ref, qp_ref, qn_ref, k_ref, kp_ref, kn_ref, v_ref, vp_ref, vn_ref,
                 cw_ref, ab_ref, gpar_ref, cos_ref, sin_ref,
                 qo_ref, ko_ref, vo_ref, gb_ref, gbt_ref, xq_scr, xk_scr, xv_scr):
    t = pl.program_id(1)
    tt = q_ref.shape[0]
    first = t == 0
    last = t == n_tiles - 1
    pad = DN_CONV // 2
    halo = qp_ref.shape[0]

    for scr, m_ref, p_ref, n_ref in ((xq_scr, q_ref, qp_ref, qn_ref), (xk_scr, k_ref, kp_ref, kn_ref),
                                     (xv_scr, v_ref, vp_ref, vn_ref)):
        scr[0:halo, :] = jnp.where(first, 0.0, p_ref[...].astype(F32))
        scr[halo:halo + tt, :] = m_ref[...].astype(F32)
        scr[halo + tt:, :] = jnp.where(last, 0.0, n_ref[...].astype(F32))

    def conv_silu(scr, w, sl):
        acc = scr[halo - pad:halo - pad + tt, sl] * w[0:1]
        for i in range(1, DN_CONV):
            o = halo - pad + i
            acc = acc + scr[o:o + tt, sl] * w[i:i + 1]
        return acc * jax.nn.sigmoid(acc)

    def l2n(x):
        return x * lax.rsqrt(jnp.sum(x * x, axis=-1, keepdims=True) + EPS)

    def rot(x):
        if not rope:
            return x
        return x * cos_ref[...] + pltpu.roll(x, LANES // 2, 1) * sin_ref[...]

    for hh in range(DN_HEADS):
        sl = slice(hh * LANES, (hh + 1) * LANES)
        qo_ref[:, sl] = (rot(l2n(conv_silu(xq_scr, cw_ref[0, :, sl], sl))) * (DN_DK ** -0.5)).astype(BF16)
        ko_ref[:, sl] = rot(l2n(conv_silu(xk_scr, cw_ref[1, :, sl], sl))).astype(BF16)
        vo_ref[:, sl] = conv_silu(xv_scr, cw_ref[2, :, sl], sl).astype(BF16)

    ab = ab_ref[...]
    lane = lax.broadcasted_iota(jnp.int32, ab.shape, 1)
    row = lax.broadcasted_iota(jnp.int32, ab.shape, 0) % DN_CHUNK
    xg = ab + gpar_ref[1:2]
    sp = jnp.maximum(xg, 0.0) + jnp.log1p(jnp.exp(-jnp.abs(xg)))
    g = gpar_ref[0:1] * sp
    beta = jax.nn.sigmoid(ab)
    pre = g
    suf = g
    s = 1
    while s < DN_CHUNK:
        pre = pre + jnp.where(row >= s, pltpu.roll(pre, s, 0), 0.0)
        suf = suf + jnp.where(row < DN_CHUNK - s, pltpu.roll(suf, tt - s, 0), 0.0)
        s *= 2
    nh = DN_HEADS
    gb = jnp.where(lane < nh, pre, jnp.where(lane < 2 * nh, suf, beta))
    gb_ref[...] = gb
    er = lax.broadcasted_iota(jnp.int32, (LANES, 3 * LANES), 0)
    ec = lax.broadcasted_iota(jnp.int32, (LANES, 3 * LANES), 1)
    eye3 = ((ec % LANES) == er).astype(BF16)
    gbt_ref[...] = lax.dot_general(eye3, jnp.concatenate(_split3(gb), axis=1),
                                   (((1,), (1,)), ((), ())), preferred_element_type=F32)


def _dn_prep(p, ab, conv_w3, gpar, cos2, sin2, batch, seq, rope):
    rows = p.shape[0]
    tt = SCAN_TILE
    n_tiles = seq // tt
    halo = BF16_SUBLANES
    hb = tt // halo
    n_hblk = rows // halo
    d = DN_HEADS * LANES

    def main(cb):
        return pl.BlockSpec((tt, d), lambda b, t: (b * n_tiles + t, cb))

    def prev(cb):
        return pl.BlockSpec((halo, d), lambda b, t: (jnp.maximum((b * n_tiles + t) * hb - 1, 0), cb))

    def nxt(cb):
        return pl.BlockSpec((halo, d),
                            lambda b, t: (jnp.minimum((b * n_tiles + t + 1) * hb, n_hblk - 1), cb))

    in_specs = []
    for cb in range(3):
        in_specs += [main(cb), prev(cb), nxt(cb)]
    in_specs += [
        pl.BlockSpec((3, SUBLANES, d), lambda b, t: (0, 0, 0)),
        pl.BlockSpec((tt, LANES), lambda b, t: (b * n_tiles + t, 0)),
        pl.BlockSpec((SUBLANES, LANES), lambda b, t: (0, 0)),
        pl.BlockSpec((tt, LANES), lambda b, t: (t, 0)),
        pl.BlockSpec((tt, LANES), lambda b, t: (t, 0)),
    ]
    out_full = pl.BlockSpec((tt, d), lambda b, t: (b * n_tiles + t, 0))
    return pl.pallas_call(
        functools.partial(_prep_kernel, rope, n_tiles),
        grid=(batch, n_tiles),
        in_specs=in_specs,
        out_specs=[out_full, out_full, out_full,
                   pl.BlockSpec((tt, LANES), lambda b, t: (b * n_tiles + t, 0)),
                   pl.BlockSpec((LANES, tt), lambda b, t: (0, b * n_tiles + t))],
        out_shape=[jax.ShapeDtypeStruct((rows, d), BF16)] * 3
        + [jax.ShapeDtypeStruct((rows, LANES), F32), jax.ShapeDtypeStruct((LANES, rows), F32)],
        scratch_shapes=[pltpu.VMEM((tt + 2 * halo, d), F32)] * 3,
        compiler_params=_cparams(("arbitrary", "arbitrary")),
        name="dn_prep_rope" if rope else "dn_prep",
    )(p, p, p, p, p, p, p, p, p, conv_w3, ab, gpar, cos2, sin2)


def _scan_kernel(n_steps,
                 qf_ref, kf_ref, vf_ref, gf_ref, gtf_ref, qb_ref, kb_ref, vb_ref, gb_ref, gtb_ref, s0_ref,
                 of_ref, ob_ref, sfin_ref, s_scr):
    hg = pl.program_id(1)
    step = pl.program_id(2)
    c = DN_CHUNK
    n_chunks = qf_ref.shape[0] // c
    refs = ((qf_ref, kf_ref, vf_ref, gf_ref, gtf_ref, of_ref),
            (qb_ref, kb_ref, vb_ref, gb_ref, gtb_ref, ob_ref))

    @pl.when(step == 0)
    def _():
        s_scr[...] = s0_ref[...]

    lane = lax.broadcasted_iota(jnp.int32, (c, LANES), 1)

    def pick(tile, idx):
        return jnp.sum(jnp.where(lane == idx, tile, 0.0), axis=1, keepdims=True)

    sub = lax.broadcasted_iota(jnp.int32, (DN_HEADS, qf_ref.shape[0]), 0)
    gc_rows = {(hh, dr): jnp.sum(jnp.where(sub == hg * SCAN_HEADS + hh,
                                           refs[dr][4][dr * DN_HEADS:(dr + 1) * DN_HEADS, :], 0.0),
                                 axis=0, keepdims=True)
               for hh in range(SCAN_HEADS) for dr in range(2)}

    items = [(hh, dr, cc) for hh in range(SCAN_HEADS) for dr in range(2) for cc in range(n_chunks)]
    st = []
    for hh, dr, cc in items:
        q_ref, k_ref, v_ref, g_ref, gt_ref, _ = refs[dr]
        rs = slice(cc * c, (cc + 1) * c)
        ls = slice(hh * LANES, (hh + 1) * LANES)
        head = hg * SCAN_HEADS + hh
        gtile = g_ref[rs, :]
        gc = pick(gtile, dr * DN_HEADS + head)
        beta = pick(gtile, (2 + dr) * DN_HEADS + head)
        gc_row = gc_rows[(hh, dr)][:, rs]
        edge = c - 1 if dr == 0 else 0
        g_last = gc_row[:, edge:edge + 1]
        q = q_ref[rs, ls].astype(F32)
        k = k_ref[rs, ls].astype(F32)
        v = v_ref[rs, ls].astype(F32)
        egc = jnp.exp(gc)
        kbeta = k * beta
        st.append(dict(dr=dr, rs=rs, ls=ls, q=q, k=k, gc=gc, kbeta=kbeta,
                       rhs=jnp.concatenate([v * beta, kbeta * egc], axis=1).astype(BF16),
                       k_dec=(k * jnp.exp(g_last - gc)).astype(BF16),
                       q_dec=(q * egc).astype(BF16),
                       e_last=jnp.exp(g_last)))

    assert n_chunks % 2 == 0
    pr = lax.broadcasted_iota(jnp.int32, (c, 2 * c), 0)
    pl2 = lax.broadcasted_iota(jnp.int32, (c, 2 * c), 1)
    pc = pl2 % c
    left = pl2 < c
    eye2 = (pr == pc).astype(F32)
    incl2 = (pr >= pc, pr <= pc)
    strict2 = (pr > pc, pr < pc)
    zeros_k = jnp.zeros((c, LANES), F32)
    zeros_r = jnp.zeros((c, 2 * LANES), BF16)

    def blockdiag(bp):
        return jnp.concatenate([jnp.where(left, bp, 0.0), jnp.where(left, 0.0, bp)], axis=0)

    def pdot(ap, bp):
        return _bdot(ap, blockdiag(bp))

    pairs = [(st[2 * m], st[2 * m + 1], items[2 * m]) for m in range(len(st) // 2)]
    tms = []
    for s0, s1, (hh, dr, cc) in pairs:
        gcol = jnp.where(left, s0["gc"], s1["gc"])
        grow = gc_rows[(hh, dr)][:, cc * c:(cc + 2) * c]
        dec = jnp.where(incl2[dr], jnp.exp(jnp.where(incl2[dr], gcol - grow, 0.0)), 0.0)
        kk = jnp.concatenate([jnp.concatenate([s0["k"], zeros_k], axis=1),
                              jnp.concatenate([zeros_k, s1["k"]], axis=1)], axis=0)
        tms.append(jnp.where(strict2[dr],
                             _dot_nt(jnp.concatenate([s0["kbeta"], s1["kbeta"]], axis=1), kk) * dec, 0.0))
        s0["attn2"] = (_dot_nt(jnp.concatenate([s0["q"], s1["q"]], axis=1), kk) * dec).astype(BF16)

    m8 = (pr // 8) == (pc // 8)
    pw = [-jnp.where(m8, t, 0.0) for t in tms]
    p2 = [pdot(p, p) for p in pw]
    p4 = [pdot(p, p) for p in p2]
    xs = [eye2 + p for p in pw]
    xs = [x + pdot(p, x) for x, p in zip(xs, p2)]
    xs = [x + pdot(p, x) for x, p in zip(xs, p4)]
    blk = 8
    while blk < c:
        off = ((pr // (2 * blk)) == (pc // (2 * blk))) & ((pr // blk) != (pc // blk))
        lx = [pdot(jnp.where(off, t, 0.0), x) for t, x in zip(tms, xs)]
        xs = [x - pdot(x, y) for x, y in zip(xs, lx)]
        blk *= 2
    uw = []
    for (s0, s1, _), x in zip(pairs, xs):
        both = _bdot(x, jnp.concatenate([jnp.concatenate([s0["rhs"], zeros_r], axis=1),
                                         jnp.concatenate([zeros_r, s1["rhs"]], axis=1)], axis=0))
        uw += [both[:, :2 * LANES], both[:, 2 * LANES:]]
        s0["attn"] = s0["attn2"][:, :c]
        s1["attn"] = s0["attn2"][:, c:]

    by_key = {it: (s, y) for it, s, y in zip(items, st, uw)}
    chains = [(hh, dr) for hh in range(SCAN_HEADS) for dr in range(2)]
    state = {ch: s_scr[ch[0], ch[1]] for ch in chains}
    for i in range(n_chunks):
        cur = {ch: by_key[(ch[0], ch[1], i if ch[1] == 0 else n_chunks - 1 - i)] for ch in chains}
        wsqs = {ch: _bdot(jnp.concatenate([cur[ch][1][:, LANES:].astype(BF16), cur[ch][0]["q_dec"]], axis=0),
                          state[ch]) for ch in chains}
        v_new = {ch: cur[ch][1][:, :LANES] - wsqs[ch][:c] for ch in chains}
        for ch in chains:
            s = cur[ch][0]
            refs[ch[1]][5][s["rs"], s["ls"]] = wsqs[ch][c:] + _bdot(s["attn"], v_new[ch])
        state = {ch: state[ch] * cur[ch][0]["e_last"] + _dot_tn(cur[ch][0]["k_dec"], v_new[ch])
                 for ch in chains}
    for ch in chains:
        s_scr[ch[0], ch[1]] = state[ch]

    @pl.when(step == n_steps - 1)
    def _():
        sfin_ref[...] = s_scr[...]


def _dn_scan(qn, kn, vv, gb, gbt, s0, batch, seq):
    rows, d = qn.shape
    tt = SCAN_TILE
    n_steps = seq // tt
    n_groups = DN_HEADS // SCAN_HEADS
    w = SCAN_HEADS * LANES
    fwd_t = lambda b, s: b * n_steps + s
    bwd_t = lambda b, s: b * n_steps + n_steps - 1 - s

    def specs(tile):
        wide = pl.BlockSpec((tt, w), lambda b, g, s: (tile(b, s), g))
        return wide, [wide, wide, wide,
                      pl.BlockSpec((tt, LANES), lambda b, g, s: (tile(b, s), 0)),
                      pl.BlockSpec((LANES, tt), lambda b, g, s: (0, tile(b, s)))]

    out_f, in_f = specs(fwd_t)
    out_b, in_b = specs(bwd_t)
    st_spec = pl.BlockSpec((None, SCAN_HEADS, 2, LANES, LANES), lambda b, g, s: (b, g, 0, 0, 0))
    return pl.pallas_call(
        functools.partial(_scan_kernel, n_steps),
        grid=(batch, n_groups, n_steps),
        in_specs=in_f + in_b + [st_spec],
        out_specs=[out_f, out_b, st_spec],
        out_shape=[jax.ShapeDtypeStruct((rows, d), F32), jax.ShapeDtypeStruct((rows, d), F32),
                   jax.ShapeDtypeStruct((batch, DN_HEADS, 2, LANES, LANES), F32)],
        scratch_shapes=[pltpu.VMEM((SCAN_HEADS, 2, LANES, LANES), F32)],
        compiler_params=_cparams(("arbitrary", "arbitrary", "arbitrary")),
        name="dn_scan",
    )(qn, kn, vv, gb, gbt, qn, kn, vv, gb, gbt, s0)


def _na_kernel(n_rows, q_ref, k_ref, v_ref, kc_ref, vc_ref, bias_ref, o_ref):
    t = pl.program_id(2)
    kw = NA_KROWS * GRID_W
    ks = jnp.clip(t * NA_QROWS - NA_WIN_R // 2, 0, n_rows - NA_KROWS)
    start = pl.multiple_of(ks * GRID_W, GRID_W)
    q = q_ref[...] * (NA_DH ** -0.5)
    kwin = k_ref[pl.ds(start, kw), :]
    vwin = v_ref[pl.ds(start, kw), :]
    kc = kc_ref[...]
    vc = vc_ref[...]
    lane = lax.broadcasted_iota(jnp.int32, q.shape, 1)
    outs = []
    for hh in range(2):
        sel = (lane < NA_DH) if hh == 0 else (lane >= NA_DH)
        qh = jnp.where(sel, q, jnp.zeros_like(q))
        s_loc = lax.dot_general(qh, kwin, (((1,), (1,)), ((), ())),
                                preferred_element_type=F32) + bias_ref[hh]
        s_ctx = lax.dot_general(qh, kc, (((1,), (1,)), ((), ())), preferred_element_type=F32)
        m = jnp.maximum(jnp.max(s_loc, axis=1, keepdims=True), jnp.max(s_ctx, axis=1, keepdims=True))
        p_loc = jnp.exp(s_loc - m)
        p_ctx = jnp.exp(s_ctx - m)
        denom = jnp.sum(p_loc, axis=1, keepdims=True) + jnp.sum(p_ctx, axis=1, keepdims=True)
        o = (jnp.dot(p_loc.astype(BF16), vwin, preferred_element_type=F32)
             + jnp.dot(p_ctx.astype(BF16), vc, preferred_element_type=F32)) / denom
        outs.append(o)
    o_ref[...] = jnp.where(lane < NA_DH, outs[0], outs[1]).astype(BF16)


def _na_attention(p, pc, bias, batch, seq, ctx_len, q_col, k_col, v_col):
    rows = p.shape[0]
    n_rows = seq // GRID_W
    qt = NA_QROWS * GRID_W
    n_tiles = n_rows // NA_QROWS
    kw = NA_KROWS * GRID_W
    n_pairs = NA_HEADS // 2

    def geom(t):
        return jnp.where(t == 0, 0, jnp.where(t == n_tiles - 1, 2, 1))

    return pl.pallas_call(
        functools.partial(_na_kernel, n_rows),
        grid=(batch, n_pairs, n_tiles),
        in_specs=[pl.BlockSpec((qt, LANES), lambda b, pr, t: (b * n_tiles + t, q_col + pr)),
                  pl.BlockSpec((seq, LANES), lambda b, pr, t: (b, k_col + pr)),
                  pl.BlockSpec((seq, LANES), lambda b, pr, t: (b, v_col + pr)),
                  pl.BlockSpec((ctx_len, LANES), lambda b, pr, t: (b, k_col + pr)),
                  pl.BlockSpec((ctx_len, LANES), lambda b, pr, t: (b, v_col + pr)),
                  pl.BlockSpec((None, 2, qt, kw), lambda b, pr, t: (geom(t), pr, 0, 0))],
        out_specs=pl.BlockSpec((qt, LANES), lambda b, pr, t: (b * n_tiles + t, pr)),
        out_shape=jax.ShapeDtypeStruct((rows, n_pairs * LANES), BF16),
        compiler_params=_cparams(("arbitrary", "arbitrary", "arbitrary")),
        name="na_attention",
    )(p, p, p, pc, pc, bias)


def _ctx_attn_kernel(q_ref, k_ref, v_ref, o_ref):
    q = q_ref[...] * (NA_DH ** -0.5)
    k = k_ref[...]
    v = v_ref[...]
    lane = lax.broadcasted_iota(jnp.int32, q.shape, 1)
    outs = []
    for hh in range(2):
        sel = (lane < NA_DH) if hh == 0 else (lane >= NA_DH)
        qh = jnp.where(sel, q, jnp.zeros_like(q))
        s = lax.dot_general(qh, k, (((1,), (1,)), ((), ())), preferred_element_type=F32)
        pm = jnp.exp(s - jnp.max(s, axis=1, keepdims=True))
        outs.append(jnp.dot(pm.astype(BF16), v, preferred_element_type=F32)
                    / jnp.sum(pm, axis=1, keepdims=True))
    o_ref[...] = jnp.where(lane < NA_DH, outs[0], outs[1]).astype(BF16)


def _ctx_attention(pc, batch, ctx_len, q_col, k_col, v_col):
    n_pairs = NA_HEADS // 2
    return pl.pallas_call(
        _ctx_attn_kernel,
        grid=(batch, n_pairs),
        in_specs=[pl.BlockSpec((ctx_len, LANES), lambda b, pr: (b, q_col + pr)),
                  pl.BlockSpec((ctx_len, LANES), lambda b, pr: (b, k_col + pr)),
                  pl.BlockSpec((ctx_len, LANES), lambda b, pr: (b, v_col + pr))],
        out_specs=pl.BlockSpec((ctx_len, LANES), lambda b, pr: (b, pr)),
        out_shape=jax.ShapeDtypeStruct((pc.shape[0], n_pairs * LANES), BF16),
        compiler_params=_cparams(("arbitrary", "arbitrary")),
        name="ctx_attention",
    )(pc, pc, pc)


def _na_bias_tables(rpb, n_rows):
    n_tiles = n_rows // NA_QROWS
    n_roff = 2 * NA_WIN_R - 1
    n_coff = 2 * NA_WIN_W - 1
    col = np.arange(GRID_W)
    c0 = np.clip(col - NA_WIN_W // 2, 0, GRID_W - NA_WIN_W)
    col_in = (col[None, :] >= c0[:, None]) & (col[None, :] < c0[:, None] + NA_WIN_W)
    coff = np.clip(col[None, :] - col[:, None] + (NA_WIN_W - 1), 0, n_coff - 1)
    col_sel = (coff[:, :, None] == np.arange(n_coff)).astype(np.float32)
    roff_tab = np.zeros((3, NA_QROWS * NA_KROWS), np.int32)
    for g, t in enumerate((0, 1, n_tiles - 1)):
        rs = t * NA_QROWS
        ks = min(max(rs - NA_WIN_R // 2, 0), n_rows - NA_KROWS)
        qrow = rs + np.arange(NA_QROWS)
        krow = ks + np.arange(NA_KROWS)
        r0 = np.clip(qrow - NA_WIN_R // 2, 0, n_rows - NA_WIN_R)
        row_in = (krow[None, :] >= r0[:, None]) & (krow[None, :] < r0[:, None] + NA_WIN_R)
        roff = krow[None, :] - qrow[:, None] + (NA_WIN_R - 1)
        roff_tab[g] = np.where(row_in, roff, -1).reshape(-1)
    by_col = jnp.einsum("hrc,qkc->hrqk", rpb, jnp.asarray(col_sel), precision=lax.Precision.HIGHEST)
    by_col = jnp.where(jnp.asarray(col_in)[None, None], by_col, -jnp.inf)
    n_heads = rpb.shape[0]
    qt, kw = NA_QROWS * GRID_W, NA_KROWS * GRID_W
    return pl.pallas_call(
        _bias_kernel,
        grid_spec=pltpu.PrefetchScalarGridSpec(
            num_scalar_prefetch=1,
            grid=(3, n_heads),
            in_specs=[pl.BlockSpec((None, n_roff, GRID_W, GRID_W), lambda g, h, ro: (h, 0, 0, 0))],
            out_specs=pl.BlockSpec((None, None, qt, kw), lambda g, h, ro: (g, h, 0, 0))),
        out_shape=jax.ShapeDtypeStruct((3, n_heads, qt, kw), F32),
        compiler_params=_cparams(("arbitrary", "arbitrary")),
        name="na_bias_table",
    )(jnp.asarray(roff_tab), by_col)


def _bias_kernel(roff_ref, bycol_ref, o_ref):
    g = pl.program_id(0)
    for a in range(NA_QROWS):
        for i in range(NA_KROWS):
            r = roff_ref[g, a * NA_KROWS + i]
            blk = bycol_ref[jnp.maximum(r, 0)]
            o_ref[a * GRID_W:(a + 1) * GRID_W, i * GRID_W:(i + 1) * GRID_W] = jnp.where(r >= 0, blk, -jnp.inf)


def _merge_kernel(of_ref, ob_ref, z_ref, na_ref, gd_ref, gn_ref, x_ref, dnw_ref, wpa_ref, wpb_ref,
                  wout_ref, gpost_ref, g1_ref, o_ref, dn_scr):
    o = of_ref[...] + ob_ref[...]
    z = z_ref[...].astype(F32)
    for hh in range(DN_HEADS):
        sl = slice(hh * LANES, (hh + 1) * LANES)
        oh = o[:, sl]
        oh = oh * lax.rsqrt(jnp.mean(oh * oh, axis=-1, keepdims=True) + EPS) * dnw_ref[...]
        zh = z[:, sl]
        dn_scr[:, sl] = (oh * (zh * jax.nn.sigmoid(zh))).astype(BF16)
    y = (jax.nn.sigmoid(gd_ref[...].astype(F32)) * jnp.dot(dn_scr[...], wpa_ref[...], preferred_element_type=F32)
         + jax.nn.sigmoid(gn_ref[...].astype(F32)) * _bdot(na_ref[...], wpb_ref[...]))
    out = _bdot(y, wout_ref[...])
    o_ref[...] = x_ref[...] + g1_ref[...] * _rms(out, gpost_ref[...])


def _merge(o_f, o_b, p, na_o, x2, dn_norm, w_pa, w_pb, w_out, gpost, mod, mod_row, g1_blk, tm):
    rows, d = x2.shape
    nw = na_o.shape[1]
    row_blk = lambda c: pl.BlockSpec((tm, d), lambda i: (i, c))
    const = lambda shape: pl.BlockSpec(shape, lambda i: (0,) * len(shape))
    return pl.pallas_call(
        _merge_kernel,
        grid=(rows // tm,),
        in_specs=[row_blk(0), row_blk(0), row_blk(3), pl.BlockSpec((tm, nw), lambda i: (i, 0)),
                  row_blk(4), row_blk(5), row_blk(0),
                  const((1, LANES)), const((d, d)), const((nw, d)), const((d, d)), const((1, d)),
                  pl.BlockSpec((None, 1, d), lambda i: (mod_row(i), 0, g1_blk))],
        out_specs=row_blk(0),
        out_shape=jax.ShapeDtypeStruct((rows, d), F32),
        scratch_shapes=[pltpu.VMEM((tm, d), BF16)],
        compiler_params=_cparams(("arbitrary",)),
        name="merge",
    )(o_f, o_b, p, na_o, p, p, x2, dn_norm, w_pa, w_pb, w_out, gpost, mod)


def _ffn_kernel(n_f, x_ref, gpre_ref, sc_ref, sh_ref, w1_ref, w3_ref, w2_ref, gpost_ref, g2_ref,
                o_ref, h_scr, acc_scr):
    j = pl.program_id(1)

    @pl.when(j == 0)
    def _():
        h = _rms(x_ref[...], gpre_ref[...]) * (1.0 + sc_ref[...]) + sh_ref[...]
        h_scr[...] = h.astype(BF16)
        acc_scr[...] = jnp.zeros_like(acc_scr)

    h = h_scr[...]
    a = jnp.dot(h, w1_ref[...], preferred_element_type=F32)
    b = jnp.dot(h, w3_ref[...], preferred_element_type=F32)
    acc_scr[...] += _bdot(a * jax.nn.sigmoid(a) * b, w2_ref[...])

    @pl.when(j == n_f - 1)
    def _():
        o_ref[...] = x_ref[...] + g2_ref[...] * _rms(acc_scr[...], gpost_ref[...])


def _dense_ffn(x2, gpre, gpost, mod, mod_row, w1, w3, w2, tm, tf):
    rows, d = x2.shape
    f = w1.shape[1]
    n_f = f // tf
    modspec = lambda blk: pl.BlockSpec((None, 1, d), lambda i, j: (mod_row(i), 0, blk))
    return pl.pallas_call(
        functools.partial(_ffn_kernel, n_f),
        grid=(rows // tm, n_f),
        in_specs=[pl.BlockSpec((tm, d), lambda i, j: (i, 0)),
                  pl.BlockSpec((1, d), lambda i, j: (0, 0)),
                  modspec(4), modspec(3),
                  pl.BlockSpec((d, tf), lambda i, j: (0, j)),
                  pl.BlockSpec((d, tf), lambda i, j: (0, j)),
                  pl.BlockSpec((tf, d), lambda i, j: (j, 0)),
                  pl.BlockSpec((1, d), lambda i, j: (0, 0)),
                  modspec(5)],
        out_specs=pl.BlockSpec((tm, d), lambda i, j: (i, 0)),
        out_shape=jax.ShapeDtypeStruct((rows, d), F32),
        scratch_shapes=[pltpu.VMEM((tm, d), BF16), pltpu.VMEM((tm, d), F32)],
        compiler_params=_cparams(("arbitrary", "arbitrary")),
        name="dense_ffn",
    )(x2, gpre, mod, mod, w1, w3, w2, gpost, mod)


def _router_kernel(x_ref, gpre_ref, sc_ref, sh_ref, r_ref, h_ref, gate_ref):
    h = _rms(x_ref[...], gpre_ref[...]) * (1.0 + sc_ref[...]) + sh_ref[...]
    h_ref[...] = h
    logits = jnp.dot(h, r_ref[...], precision=lax.Precision.HIGHEST, preferred_element_type=F32)
    lane = lax.broadcasted_iota(jnp.int32, logits.shape, 1)
    neg = -jnp.inf
    l1 = jnp.where(lane < N_EXPERTS, logits, neg)
    m1 = jnp.max(l1, axis=1, keepdims=True)
    i1 = jnp.min(jnp.where(l1 == m1, lane, LANES), axis=1, keepdims=True)
    l2 = jnp.where(lane == i1, neg, l1)
    m2 = jnp.max(l2, axis=1, keepdims=True)
    i2 = jnp.min(jnp.where(l2 == m2, lane, LANES), axis=1, keepdims=True)
    e = jnp.exp(m2 - m1)
    w1 = 1.0 / (1.0 + e)
    w2 = e / (1.0 + e)
    out = jnp.where(lane == 0, i1.astype(F32), 0.0)
    out = jnp.where(lane == 1, i2.astype(F32), out)
    out = jnp.where(lane == 2, w1, out)
    out = jnp.where(lane == 3, w2, out)
    gate_ref[...] = out


def _router(x2, gpre, mod, mod_row, router_pad, tm):
    rows, d = x2.shape
    modspec = lambda blk: pl.BlockSpec((None, 1, d), lambda i: (mod_row(i), 0, blk))
    return pl.pallas_call(
        _router_kernel,
        grid=(rows // tm,),
        in_specs=[pl.BlockSpec((tm, d), lambda i: (i, 0)),
                  pl.BlockSpec((1, d), lambda i: (0, 0)),
                  modspec(4), modspec(3),
                  pl.BlockSpec((d, LANES), lambda i: (0, 0))],
        out_specs=[pl.BlockSpec((tm, d), lambda i: (i, 0)),
                   pl.BlockSpec((tm, LANES), lambda i: (i, 0))],
        out_shape=[jax.ShapeDtypeStruct((rows, d), F32),
                   jax.ShapeDtypeStruct((rows, LANES), F32)],
        compiler_params=_cparams(("arbitrary",)),
        name="moe_router",
    )(x2, gpre, mod, mod, router_pad)


def _gather_row(h_hbm, xbuf, sem, slot, r, tok):
    return pltpu.make_async_copy(h_hbm.at[pl.ds(tok, 1), :], xbuf.at[slot, pl.ds(r, 1), :], sem.at[slot])


def _scatter_row(stage, out_hbm, sem, slot, r, dst):
    return pltpu.make_async_copy(stage.at[slot, pl.ds(r, 1), :], out_hbm.at[pl.ds(dst, 1), :], sem.at[slot])


def _expert_kernel(n_f, n_tiles, te_ref, nv_ref, tok_ref, tokn_ref, dstp_ref, dst_ref, h_hbm,
                   w1_ref, w3_ref, w2_ref, out_hbm, xbuf, xb16, acc_scr, stage, gsem, ssem):
    i = pl.program_id(0)
    j = pl.program_id(1)
    slot = i % 2
    rows_per_step = MOE_TM // n_f

    def wait_gathers(sl):
        pltpu.make_async_copy(h_hbm.at[pl.ds(0, MOE_TM), :], xbuf.at[sl], gsem.at[sl]).wait()

    def wait_scatters(sl):
        pltpu.make_async_copy(stage.at[sl], out_hbm.at[pl.ds(0, MOE_TM), :], ssem.at[sl]).wait()

    @pl.when((i == 0) & (j == 0))
    def _():
        stage[...] = jnp.zeros_like(stage)

        def body(r, carry):
            _gather_row(h_hbm, xbuf, gsem, slot, r, tok_ref[0, r]).start()
            return carry
        lax.fori_loop(0, MOE_TM, body, 0, unroll=8)

    @pl.when(j == 0)
    def _():
        wait_gathers(slot)
        xb16[...] = xbuf[slot].astype(BF16)
        acc_scr[...] = jnp.zeros_like(acc_scr)

    def row_dmas():
        for k in range(rows_per_step):
            r = pl.multiple_of(j * rows_per_step, SUBLANES) + k
            _gather_row(h_hbm, xbuf, gsem, 1 - slot, r, tokn_ref[0, r]).start()
            _scatter_row(stage, out_hbm, ssem, 1 - slot, r, dstp_ref[0, r]).start()

    valid = i < nv_ref[0]

    @pl.when(valid)
    def _():
        row_dmas()
        x = xb16[...]
        a = jnp.dot(x, w1_ref[...], preferred_element_type=F32)
        b = jnp.dot(x, w3_ref[...], preferred_element_type=F32)
        acc_scr[...] += _bdot(a * jax.nn.sigmoid(a) * b, w2_ref[...])

    @pl.when(jnp.logical_not(valid))
    def _():
        row_dmas()

    @pl.when(j == n_f - 1)
    def _():
        @pl.when(i >= 1)
        def _():
            wait_scatters(slot)

        stage[slot] = acc_scr[...]

        @pl.when(i == n_tiles - 1)
        def _():
            wait_scatters(1 - slot)
            wait_gathers(1 - slot)

            def body(r, carry):
                _scatter_row(stage, out_hbm, ssem, slot, r, dst_ref[0, r]).start()
                return carry
            lax.fori_loop(0, MOE_TM, body, 0, unroll=8)
            wait_scatters(slot)


def _expert_ffn(h, row_token, out_row, tile_expert, n_valid, w1, w3, w2, tf):
    d = h.shape[1]
    prow = row_token.shape[0]
    f = w1.shape[2]
    n_f = f // tf
    n_tiles = prow // MOE_TM
    assert f % tf == 0 and MOE_TM % (n_f * SUBLANES) == 0
    idx_spec =lambda fn: pl.BlockSpec((None, 1, MOE_TM), lambda i, j, te, nv: (fn(i), 0, 0),
                                       memory_space=pltpu.SMEM)
    tok3 = row_token.reshape(n_tiles, 1, MOE_TM)
    dst3 = jnp.concatenate([prow + jnp.arange(MOE_TM, dtype=jnp.int32), out_row]).reshape(n_tiles + 1, 1, MOE_TM)
    grid_spec = pltpu.PrefetchScalarGridSpec(
        num_scalar_prefetch=2,
        grid=(n_tiles, n_f),
        in_specs=[idx_spec(lambda i: i),
                  idx_spec(lambda i: jnp.minimum(i + 1, n_tiles - 1)),
                  idx_spec(lambda i: i),
                  idx_spec(lambda i: i + 1),
                  pl.BlockSpec(memory_space=pl.ANY),
                  pl.BlockSpec((None, d, tf), lambda i, j, te, nv: (te[i], 0, j)),
                  pl.BlockSpec((None, d, tf), lambda i, j, te, nv: (te[i], 0, j)),
                  pl.BlockSpec((None, tf, d), lambda i, j, te, nv: (te[i], j, 0))],
        out_specs=pl.BlockSpec(memory_space=pl.ANY),
        scratch_shapes=[pltpu.VMEM((2, MOE_TM, d), F32), pltpu.VMEM((MOE_TM, d), BF16),
                        pltpu.VMEM((MOE_TM, d), F32), pltpu.VMEM((2, MOE_TM, d), F32),
                        pltpu.SemaphoreType.DMA((2,)), pltpu.SemaphoreType.DMA((2,))],
    )
    return pl.pallas_call(
        functools.partial(_expert_kernel, n_f, n_tiles),
        grid_spec=grid_spec,
        out_shape=jax.ShapeDtypeStruct((prow + MOE_TM, d), F32),
        compiler_params=_cparams(("arbitrary", "arbitrary")),
        name="moe_experts",
    )(tile_expert, n_valid, tok3, tok3, dst3, dst3, h, w1, w3, w2)


def _combine_kernel(y1_ref, y2_ref, gate_ref, x_ref, gpost_ref, g2_ref, o_ref):
    gt = gate_ref[...]
    lane = lax.broadcasted_iota(jnp.int32, gt.shape, 1)
    w1 = jnp.sum(jnp.where(lane == 2, gt, 0.0), axis=1, keepdims=True)
    w2 = jnp.sum(jnp.where(lane == 3, gt, 0.0), axis=1, keepdims=True)
    y = w1 * y1_ref[...] + w2 * y2_ref[...]
    o_ref[...] = x_ref[...] + g2_ref[...] * _rms(y, gpost_ref[...])


def _moe_combine(ys, gates, x2, gpost, mod, mod_row, tm):
    rows, d = x2.shape
    rb = pl.BlockSpec((tm, d), lambda i: (i, 0))
    return pl.pallas_call(
        _combine_kernel,
        grid=(rows // tm,),
        in_specs=[rb, pl.BlockSpec((tm, d), lambda i: (rows // tm + i, 0)),
                  pl.BlockSpec((tm, LANES), lambda i: (i, 0)), rb,
                  pl.BlockSpec((1, d), lambda i: (0, 0)),
                  pl.BlockSpec((None, 1, d), lambda i: (mod_row(i), 0, 5))],
        out_specs=rb,
        out_shape=jax.ShapeDtypeStruct((rows, d), F32),
        compiler_params=_cparams(("arbitrary",)),
        name="moe_combine",
    )(ys, ys, gates, x2, gpost, mod)


def _moe_schedule(gates, n_tokens):
    idx = gates[:, 0:2].astype(jnp.int32)
    flat_e = idx.reshape(-1)
    onehot = (flat_e[:, None] == jnp.arange(N_EXPERTS)[None, :]).astype(jnp.int32)
    csum = jnp.cumsum(onehot, axis=0)
    counts = csum[-1]
    rank = jnp.sum(csum * onehot, axis=1) - 1
    padded = ((counts + MOE_TM - 1) // MOE_TM) * MOE_TM
    ends = jnp.cumsum(padded)
    starts = ends - padded
    dest = jnp.sum(starts[None, :] * onehot, axis=1) + rank
    n_pairs = 2 * n_tokens
    n_rows = n_pairs + N_EXPERTS * MOE_TM
    n_tiles = n_rows // MOE_TM
    pair_of_row = jnp.full((n_rows,), -1, jnp.int32).at[dest].set(jnp.arange(n_pairs, dtype=jnp.int32))
    is_pad = pair_of_row < 0
    row_token = jnp.where(is_pad, 0, pair_of_row // 2)
    out_row = jnp.where(is_pad, n_pairs - 1 + jnp.cumsum(is_pad.astype(jnp.int32)),
                        (pair_of_row % 2) * n_tokens + pair_of_row // 2)
    tile_start = jnp.arange(n_tiles, dtype=jnp.int32) * MOE_TM
    tile_expert = jnp.minimum(jnp.sum((ends[None, :] <= tile_start[:, None]).astype(jnp.int32), axis=1),
                              N_EXPERTS - 1)
    n_valid = (ends[-1] // MOE_TM).astype(jnp.int32).reshape(1)
    return row_token, out_row, tile_expert, n_valid


def _rope_tables(seq):
    t = jnp.arange(seq)
    row = (t // GRID_W).astype(F32)
    col = (t % GRID_W).astype(F32)
    n_freq = DN_DK // 4
    inv = ROPE_BASE ** (-jnp.arange(n_freq, dtype=F32) / n_freq)
    ang = jnp.concatenate([row[:, None] * inv, col[:, None] * inv], axis=-1)
    cos, sin = jnp.cos(ang), jnp.sin(ang)
    return jnp.concatenate([cos, cos], axis=-1), jnp.concatenate([-sin, sin], axis=-1)


def kernel(x, c, ctx, c_ctx, ada_w, ada_b, norm_mix_pre, norm_mix_post, norm_ffn_pre, norm_ffn_post,
           w_in, dn_conv, dn_a_log, dn_dt_bias, dn_norm, na_rpb, w_branch_dn, w_branch_na, w_out,
           ffn_w1, ffn_w3, ffn_w2, moe_router, moe_w1, moe_w3, moe_w2):
    batch, seq, d = x.shape
    ctx_len = ctx.shape[1]
    depth = w_in.shape[0]
    nh = DN_HEADS
    dn_w = nh * DN_DK
    na_w = NA_HEADS * NA_DH
    n_rows = seq // GRID_W
    assert d == dn_w and seq % SCAN_TILE == 0 and ctx_len % SCAN_TILE == 0 and n_rows % NA_QROWS == 0
    assert depth <= 2, "context tokens only take the dense FFN path"

    c_rows = jnp.zeros((SUBLANES, d), F32).at[:batch].set(c).at[batch].set(c_ctx)
    mod_all = _mod_vectors(c_rows, ada_w, ada_b)
    cos2, sin2 = _rope_tables(seq)
    ones_t = jnp.ones((SCAN_TILE, LANES), F32)

    x2 = x.reshape(batch * seq, d)
    xc2 = ctx.reshape(batch * ctx_len, d)
    lat_tm = 1024
    lat_row = lambda tm: (lambda i: i // (seq // tm))
    ctx_row = lambda i: batch

    q_col, k_col, v_col = (4 * dn_w + 2 * d) // LANES, (4 * dn_w + 2 * d + na_w) // LANES, \
        (4 * dn_w + 2 * d + 2 * na_w) // LANES

    for l in range(depth):
        last = l == depth - 1
        mod = mod_all[l].reshape(SUBLANES, 1, 6 * d)
        wl = w_in[l]
        o_ab = 4 * dn_w
        o_na = o_ab + 4 * nh
        o_gate = o_na + 3 * na_w
        w_main = jnp.concatenate([wl[:, :o_ab], wl[:, o_gate:], wl[:, o_na:o_gate]], axis=1).astype(BF16)
        wab = jnp.pad(wl[:, o_ab:o_na], ((0, 0), (0, LANES - 4 * nh)))
        wab_hi = wab.astype(BF16)
        wab_lo = (wab - wab_hi.astype(F32)).astype(BF16)
        gpre = norm_mix_pre[l].reshape(1, d)
        gpost = norm_mix_post[l].reshape(1, d)

        p, ab = _in_proj(x2, gpre, mod, lat_row(lat_tm), 1, 0, w_main, wab_hi, wab_lo, lat_tm, INPROJ_TN)
        pc, abc = _in_proj(xc2, gpre, mod, ctx_row, 1, 0, w_main, wab_hi, wab_lo, batch * ctx_len, INPROJ_TN)

        conv_w3 = jnp.pad(dn_conv[l].T.reshape(DN_CONV, 3, dn_w).transpose(1, 0, 2),
                          ((0, 0), (0, SUBLANES - DN_CONV), (0, 0)))
        gpar = jnp.zeros((SUBLANES, LANES), F32)
        gpar = gpar.at[0, :2 * nh].set(-jnp.exp(dn_a_log[l].reshape(-1)))
        gpar = gpar.at[1, :2 * nh].set(dn_dt_bias[l].reshape(-1))

        qc_, kc_, vc_, gbc, gbtc = _dn_prep(pc, abc, conv_w3, gpar, ones_t, ones_t, batch, ctx_len, False)
        ql_, kl_, vl_, gbl, gbtl = _dn_prep(p, ab, conv_w3, gpar, cos2, sin2, batch, seq, True)
        s0 = jnp.zeros((batch, nh, 2, LANES, LANES), F32)
        oc_f, oc_b, s_ctx = _dn_scan(qc_, kc_, vc_, gbc, gbtc, s0, batch, ctx_len)
        ol_f, ol_b, _ = _dn_scan(ql_, kl_, vl_, gbl, gbtl, s_ctx, batch, seq)

        bias = _na_bias_tables(na_rpb[l], n_rows)
        na_lat = _na_attention(p, pc, bias, batch, seq, ctx_len, q_col, k_col, v_col)

        dnw = dn_norm[l].reshape(1, LANES)
        w_pa = w_branch_dn[l].astype(BF16)
        w_pb = w_branch_na[l].astype(BF16)
        w_o = w_out[l].astype(BF16)
        x2 = _merge(ol_f, ol_b, p, na_lat, x2, dnw, w_pa, w_pb, w_o, gpost, mod, lat_row(512), 2, 512)

        gfpre = norm_ffn_pre[l].reshape(1, d)
        gfpost = norm_ffn_post[l].reshape(1, d)
        if l % 2 == 0:
            w1 = ffn_w1[l // 2].astype(BF16)
            w3 = ffn_w3[l // 2].astype(BF16)
            w2 = ffn_w2[l // 2].astype(BF16)
            tf = w1.shape[1] // 2
            x2 = _dense_ffn(x2, gfpre, gfpost, mod, lat_row(512), w1, w3, w2, 512, tf)
        else:
            rpad = jnp.pad(moe_router[l // 2], ((0, 0), (0, LANES - N_EXPERTS)))
            hb, gates = _router(x2, gfpre, mod, lat_row(512), rpad, 512)
            n_tok = batch * seq
            row_token, out_row, tile_expert, n_valid = _moe_schedule(gates, n_tok)
            ys = _expert_ffn(hb, row_token, out_row, tile_expert, n_valid, moe_w1[l // 2].astype(BF16),
                             moe_w3[l // 2].astype(BF16), moe_w2[l // 2].astype(BF16), MOE_TF)
            x2 = _moe_combine(ys, gates, x2, gfpost, mod, lat_row(512), 512)

        if not last:
            na_ctx = _ctx_attention(pc, batch, ctx_len, q_col, k_col, v_col)
            xc2 = _merge(oc_f, oc_b, pc, na_ctx, xc2, dnw, w_pa, w_pb, w_o, gpost, mod, ctx_row, 2, 256)
            xc2 = _dense_ffn(xc2, gfpre, gfpost, mod, ctx_row, w1, w3, w2, 512, tf)
    return x2.reshape(batch, seq, d)
```

```python
import functools

import numpy as np
import jax
import jax.numpy as jnp
from jax import lax
from jax.experimental import pallas as pl
from jax.experimental.pallas import tpu as pltpu

F32 = jnp.float32
BF16 = jnp.bfloat16

GRID_W = 64
DN_HEADS = 8
DN_DK = 128
DN_CONV = 5
DN_CHUNK = 64
NA_HEADS = 8
NA_DH = 64
NA_WIN_R = 8
NA_WIN_W = 16
ROPE_BASE = 10000.0
N_EXPERTS = 8
EPS = 1e-6

LANES = 128
SUBLANES = 8
BF16_SUBLANES = 16
VMEM_LIMIT = 56 * 1024 * 1024

SCAN_TILE = 256
SCAN_HEADS = 8
NA_QROWS = 4
NA_KROWS = NA_QROWS + 8
MOE_TM = 512
MOE_TF = 1792
INPROJ_TN = 1920


def _cparams(sem):
    return pltpu.CompilerParams(dimension_semantics=sem, vmem_limit_bytes=VMEM_LIMIT)


def _bdot(a, b):
    return jnp.dot(a.astype(BF16), b.astype(BF16), preferred_element_type=F32)


def _dot_nt(a, b):
    return lax.dot_general(a.astype(BF16), b.astype(BF16), (((1,), (1,)), ((), ())),
                           preferred_element_type=F32)


def _dot_tn(a, b):
    return lax.dot_general(a.astype(BF16), b.astype(BF16), (((0,), (0,)), ((), ())),
                           preferred_element_type=F32)


def _split3(x):
    hi = x.astype(BF16)
    r = x - hi.astype(F32)
    mid = r.astype(BF16)
    lo = (r - mid.astype(F32)).astype(BF16)
    return hi, mid, lo


def _rms(x, gain):
    return x * lax.rsqrt(jnp.mean(x * x, axis=-1, keepdims=True) + EPS) * gain


def _mod_kernel(c_ref, w_ref, b_ref, o_ref):
    c = c_ref[...]
    s = c * jax.nn.sigmoid(c)
    o_ref[0] = jnp.dot(s, w_ref[0], precision=lax.Precision.HIGHEST,
                       preferred_element_type=F32) + b_ref[0]


def _mod_vectors(c_rows, ada_w, ada_b):
    depth, d, n = ada_w.shape
    tn = 1536
    return pl.pallas_call(
        _mod_kernel,
        grid=(depth, n // tn),
        in_specs=[pl.BlockSpec((SUBLANES, d), lambda l, j: (0, 0)),
                  pl.BlockSpec((1, d, tn), lambda l, j: (l, 0, j)),
                  pl.BlockSpec((1, 1, tn), lambda l, j: (l, 0, j))],
        out_specs=pl.BlockSpec((1, SUBLANES, tn), lambda l, j: (l, 0, j)),
        out_shape=jax.ShapeDtypeStruct((depth, SUBLANES, n), F32),
        compiler_params=_cparams(("arbitrary", "arbitrary")),
        name="mod_vectors",
    )(c_rows, ada_w, ada_b.reshape(depth, 1, n))


def _inproj_kernel(x_ref, g_ref, sc_ref, sh_ref, w_ref, wab_hi_ref, wab_lo_ref,
                   o_ref, ab_ref, h_scr, hlo_scr):
    j = pl.program_id(1)

    @pl.when(j == 0)
    def _():
        h = _rms(x_ref[...], g_ref[...]) * (1.0 + sc_ref[...]) + sh_ref[...]
        hi = h.astype(BF16)
        lo = (h - hi.astype(F32)).astype(BF16)
        h_scr[...] = hi
        hlo_scr[...] = lo
        ab_ref[...] = (jnp.dot(hi, wab_hi_ref[...], preferred_element_type=F32)
                       + jnp.dot(lo, wab_hi_ref[...], preferred_element_type=F32)
                       + jnp.dot(hi, wab_lo_ref[...], preferred_element_type=F32))

    o_ref[...] = jnp.dot(h_scr[...], w_ref[...], preferred_element_type=F32).astype(BF16)


def _in_proj(x2, gain, mod, mod_row, sc_blk, sh_blk, w_main, wab_hi, wab_lo, tm, tn):
    rows, d = x2.shape
    n = w_main.shape[1]
    return pl.pallas_call(
        _inproj_kernel,
        grid=(rows // tm, n // tn),
        in_specs=[pl.BlockSpec((tm, d), lambda i, j: (i, 0)),
                  pl.BlockSpec((1, d), lambda i, j: (0, 0)),
                  pl.BlockSpec((None, 1, d), lambda i, j: (mod_row(i), 0, sc_blk)),
                  pl.BlockSpec((None, 1, d), lambda i, j: (mod_row(i), 0, sh_blk)),
                  pl.BlockSpec((d, tn), lambda i, j: (0, j)),
                  pl.BlockSpec((d, LANES), lambda i, j: (0, 0)),
                  pl.BlockSpec((d, LANES), lambda i, j: (0, 0))],
        out_specs=[pl.BlockSpec((tm, tn), lambda i, j: (i, j)),
                   pl.BlockSpec((tm, LANES), lambda i, j: (i, 0))],
        out_shape=[jax.ShapeDtypeStruct((rows, n), BF16),
                   jax.ShapeDtypeStruct((rows, LANES), F32)],
        scratch_shapes=[pltpu.VMEM((tm, d), BF16), pltpu.VMEM((tm, d), BF16)],
        compiler_params=_cparams(("arbitrary", "arbitrary")),
        name="in_proj",
    )(x2, gain, mod, mod, w_main, wab_hi, wab_lo)


def _prep_kernel(rope, n_tiles,
                 q_ref, qp_ref, qn_ref, k_ref, kp_ref, kn_ref, v_ref, vp_ref, vn_ref,
                 cw_ref, ab_ref, gpar_ref, cos_ref, sin_ref,
                 qo_ref, ko_ref, vo_ref, gb_ref, gbt_ref, xq_scr, xk_scr, xv_scr):
    t = pl.program_id(1)
    tt = q_ref.shape[0]
    first = t == 0
    last = t == n_tiles - 1
    pad = DN_CONV // 2
    halo = qp_ref.shape[0]

    for scr, m_ref, p_ref, n_ref in ((xq_scr, q_ref, qp_ref, qn_ref), (xk_scr, k_ref, kp_ref, kn_ref),
                                     (xv_scr, v_ref, vp_ref, vn_ref)):
        scr[0:halo, :] = jnp.where(first, 0.0, p_ref[...].astype(F32))
        scr[halo:halo + tt, :] = m_ref[...].astype(F32)
        scr[halo + tt:, :] = jnp.where(last, 0.0, n_ref[...].astype(F32))

    def conv_silu(scr, w, sl):
        acc = scr[halo - pad:halo - pad + tt, sl] * w[0:1]
        for i in range(1, DN_CONV):
            o = halo - pad + i
            acc = acc + scr[o:o + tt, sl] * w[i:i + 1]
        return acc * jax.nn.sigmoid(acc)

    def l2n(x):
        return x * lax.rsqrt(jnp.sum(x * x, axis=-1, keepdims=True) + EPS)

    def rot(x):
        if not rope:
            return x
        return x * cos_ref[...] + pltpu.roll(x, LANES // 2, 1) * sin_ref[...]

    for hh in range(DN_HEADS):
        sl = slice(hh * LANES, (hh + 1) * LANES)
        qo_ref[:, sl] = (rot(l2n(conv_silu(xq_scr, cw_ref[0, :, sl], sl))) * (DN_DK ** -0.5)).astype(BF16)
        ko_ref[:, sl] = rot(l2n(conv_silu(xk_scr, cw_ref[1, :, sl], sl))).astype(BF16)
        vo_ref[:, sl] = conv_silu(xv_scr, cw_ref[2, :, sl], sl).astype(BF16)

    ab = ab_ref[...]
    lane = lax.broadcasted_iota(jnp.int32, ab.shape, 1)
    row = lax.broadcasted_iota(jnp.int32, ab.shape, 0) % DN_CHUNK
    xg = ab + gpar_ref[1:2]
    sp = jnp.maximum(xg, 0.0) + jnp.log1p(jnp.exp(-jnp.abs(xg)))
    g = gpar_ref[0:1] * sp
    beta = jax.nn.sigmoid(ab)
    pre = g
    suf = g
    s = 1
    while s < DN_CHUNK:
        pre = pre + jnp.where(row >= s, pltpu.roll(pre, s, 0), 0.0)
        suf = suf + jnp.where(row < DN_CHUNK - s, pltpu.roll(suf, tt - s, 0), 0.0)
        s *= 2
    nh = DN_HEADS
    gb = jnp.where(lane < nh, pre, jnp.where(lane < 2 * nh, suf, beta))
    gb_ref[...] = gb
    er = lax.broadcasted_iota(jnp.int32, (LANES, 3 * LANES), 0)
    ec = lax.broadcasted_iota(jnp.int32, (LANES, 3 * LANES), 1)
    eye3 = ((ec % LANES) == er).astype(BF16)
    gbt_ref[...] = lax.dot_general(eye3, jnp.concatenate(_split3(gb), axis=1),
                                   (((1,), (1,)), ((), ())), preferred_element_type=F32)


def _dn_prep(p, ab, conv_w3, gpar, cos2, sin2, batch, seq, rope):
    rows = p.shape[0]
    tt = SCAN_TILE
    n_tiles = seq // tt
    halo = BF16_SUBLANES
    hb = tt // halo
    n_hblk = rows // halo
    d = DN_HEADS * LANES

    def main(cb):
        return pl.BlockSpec((tt, d), lambda b, t: (b * n_tiles + t, cb))

    def prev(cb):
        return pl.BlockSpec((halo, d), lambda b, t: (jnp.maximum((b * n_tiles + t) * hb - 1, 0), cb))

    def nxt(cb):
        return pl.BlockSpec((halo, d),
                            lambda b, t: (jnp.minimum((b * n_tiles + t + 1) * hb, n_hblk - 1), cb))

    in_specs = []
    for cb in range(3):
        in_specs += [main(cb), prev(cb), nxt(cb)]
    in_specs += [
        pl.BlockSpec((3, SUBLANES, d), lambda b, t: (0, 0, 0)),
        pl.BlockSpec((tt, LANES), lambda b, t: (b * n_tiles + t, 0)),
        pl.BlockSpec((SUBLANES, LANES), lambda b, t: (0, 0)),
        pl.BlockSpec((tt, LANES), lambda b, t: (t, 0)),
        pl.BlockSpec((tt, LANES), lambda b, t: (t, 0)),
    ]
    out_full = pl.BlockSpec((tt, d), lambda b, t: (b * n_tiles + t, 0))
    return pl.pallas_call(
        functools.partial(_prep_kernel, rope, n_tiles),
        grid=(batch, n_tiles),
        in_specs=in_specs,
        out_specs=[out_full, out_full, out_full,
                   pl.BlockSpec((tt, LANES), lambda b, t: (b * n_tiles + t, 0)),
                   pl.BlockSpec((LANES, tt), lambda b, t: (0, b * n_tiles + t))],
        out_shape=[jax.ShapeDtypeStruct((rows, d), BF16)] * 3
        + [jax.ShapeDtypeStruct((rows, LANES), F32), jax.ShapeDtypeStruct((LANES, rows), F32)],
        scratch_shapes=[pltpu.VMEM((tt + 2 * halo, d), F32)] * 3,
        compiler_params=_cparams(("arbitrary", "arbitrary")),
        name="dn_prep_rope" if rope else "dn_prep",
    )(p, p, p, p, p, p, p, p, p, conv_w3, ab, gpar, cos2, sin2)


def _scan_kernel(n_steps,
                 qf_ref, kf_ref, vf_ref, gf_ref, gtf_ref, qb_ref, kb_ref, vb_ref, gb_ref, gtb_ref, s0_ref,
                 of_ref, ob_ref, sfin_ref, s_scr):
    hg = pl.program_id(1)
    step = pl.program_id(2)
    c = DN_CHUNK
    n_chunks = qf_ref.shape[0] // c
    refs = ((qf_ref, kf_ref, vf_ref, gf_ref, gtf_ref, of_ref),
            (qb_ref, kb_ref, vb_ref, gb_ref, gtb_ref, ob_ref))

    @pl.when(step == 0)
    def _():
        s_scr[...] = s0_ref[...]

    lane = lax.broadcasted_iota(jnp.int32, (c, LANES), 1)

    def pick(tile, idx):
        return jnp.sum(jnp.where(lane == idx, tile, 0.0), axis=1, keepdims=True)

    sub = lax.broadcasted_iota(jnp.int32, (DN_HEADS, qf_ref.shape[0]), 0)
    gc_rows = {(hh, dr): jnp.sum(jnp.where(sub == hg * SCAN_HEADS + hh,
                                           refs[dr][4][dr * DN_HEADS:(dr + 1) * DN_HEADS, :], 0.0),
                                 axis=0, keepdims=True)
               for hh in range(SCAN_HEADS) for dr in range(2)}

    items = [(hh, dr, cc) for hh in range(SCAN_HEADS) for dr in range(2) for cc in range(n_chunks)]
    st = []
    for hh, dr, cc in items:
        q_ref, k_ref, v_ref, g_ref, gt_ref, _ = refs[dr]
        rs = slice(cc * c, (cc + 1) * c)
        ls = slice(hh * LANES, (hh + 1) * LANES)
        head = hg * SCAN_HEADS + hh
        gtile = g_ref[rs, :]
        gc = pick(gtile, dr * DN_HEADS + head)
        beta = pick(gtile, (2 + dr) * DN_HEADS + head)
        gc_row = gc_rows[(hh, dr)][:, rs]
        edge = c - 1 if dr == 0 else 0
        g_last = gc_row[:, edge:edge + 1]
        q = q_ref[rs, ls].astype(F32)
        k = k_ref[rs, ls].astype(F32)
        v = v_ref[rs, ls].astype(F32)
        egc = jnp.exp(gc)
        kbeta = k * beta
        st.append(dict(dr=dr, rs=rs, ls=ls, q=q, k=k, gc=gc, kbeta=kbeta,
                       rhs=jnp.concatenate([v * beta, kbeta * egc], axis=1).astype(BF16),
                       k_dec=(k * jnp.exp(g_last - gc)).astype(BF16),
                       q_dec=(q * egc).astype(BF16),
                       e_last=jnp.exp(g_last)))

    assert n_chunks % 2 == 0
    pr = lax.broadcasted_iota(jnp.int32, (c, 2 * c), 0)
    pl2 = lax.broadcasted_iota(jnp.int32, (c, 2 * c), 1)
    pc = pl2 % c
    left = pl2 < c
    eye2 = (pr == pc).astype(F32)
    incl2 = (pr >= pc, pr <= pc)
    strict2 = (pr > pc, pr < pc)
    zeros_k = jnp.zeros((c, LANES), F32)
    zeros_r = jnp.zeros((c, 2 * LANES), BF16)

    def blockdiag(bp):
        return jnp.concatenate([jnp.where(left, bp, 0.0), jnp.where(left, 0.0, bp)], axis=0)

    def pdot(ap, bp):
        return _bdot(ap, blockdiag(bp))

    pairs = [(st[2 * m], st[2 * m + 1], items[2 * m]) for m in range(len(st) // 2)]
    tms = []
    for s0, s1, (hh, dr, cc) in pairs:
        gcol = jnp.where(left, s0["gc"], s1["gc"])
        grow = gc_rows[(hh, dr)][:, cc * c:(cc + 2) * c]
        dec = jnp.where(incl2[dr], jnp.exp(jnp.where(incl2[dr], gcol - grow, 0.0)), 0.0)
        kk = jnp.concatenate([jnp.concatenate([s0["k"], zeros_k], axis=1),
                              jnp.concatenate([zeros_k, s1["k"]], axis=1)], axis=0)
        tms.append(jnp.where(strict2[dr],
                             _dot_nt(jnp.concatenate([s0["kbeta"], s1["kbeta"]], axis=1), kk) * dec, 0.0))
        s0["attn2"] = (_dot_nt(jnp.concatenate([s0["q"], s1["q"]], axis=1), kk) * dec).astype(BF16)

    m8 = (pr // 8) == (pc // 8)
    pw = [-jnp.where(m8, t, 0.0) for t in tms]
    p2 = [pdot(p, p) for p in pw]
    p4 = [pdot(p, p) for p in p2]
    xs = [eye2 + p for p in pw]
    xs = [x + pdot(p, x) for x, p in zip(xs, p2)]
    xs = [x + pdot(p, x) for x, p in zip(xs, p4)]
    blk = 8
    while blk < c:
        off = ((pr // (2 * blk)) == (pc // (2 * blk))) & ((pr // blk) != (pc // blk))
        lx = [pdot(jnp.where(off, t, 0.0), x) for t, x in zip(tms, xs)]
        xs = [x - pdot(x, y) for x, y in zip(xs, lx)]
        blk *= 2
    uw = []
    for (s0, s1, _), x in zip(pairs, xs):
        both = _bdot(x, jnp.concatenate([jnp.concatenate([s0["rhs"], zeros_r], axis=1),
                                         jnp.concatenate([zeros_r, s1["rhs"]], axis=1)], axis=0))
        uw += [both[:, :2 * LANES], both[:, 2 * LANES:]]
        s0["attn"] = s0["attn2"][:, :c]
        s1["attn"] = s0["attn2"][:, c:]

    by_key = {it: (s, y) for it, s, y in zip(items, st, uw)}
    chains = [(hh, dr) for hh in range(SCAN_HEADS) for dr in range(2)]
    state = {ch: s_scr[ch[0], ch[1]] for ch in chains}
    for i in range(n_chunks):
        cur = {ch: by_key[(ch[0], ch[1], i if ch[1] == 0 else n_chunks - 1 - i)] for ch in chains}
        wsqs = {ch: _bdot(jnp.concatenate([cur[ch][1][:, LANES:].astype(BF16), cur[ch][0]["q_dec"]], axis=0),
                          state[ch]) for ch in chains}
        v_new = {ch: cur[ch][1][:, :LANES] - wsqs[ch][:c] for ch in chains}
        for ch in chains:
            s = cur[ch][0]
            refs[ch[1]][5][s["rs"], s["ls"]] = wsqs[ch][c:] + _bdot(s["attn"], v_new[ch])
        state = {ch: state[ch] * cur[ch][0]["e_last"] + _dot_tn(cur[ch][0]["k_dec"], v_new[ch])
                 for ch in chains}
    for ch in chains:
        s_scr[ch[0], ch[1]] = state[ch]

    @pl.when(step == n_steps - 1)
    def _():
        sfin_ref[...] = s_scr[...]


def _dn_scan(qn, kn, vv, gb, gbt, s0, batch, seq):
    rows, d = qn.shape
    tt = SCAN_TILE
    n_steps = seq // tt
    n_groups = DN_HEADS // SCAN_HEADS
    w = SCAN_HEADS * LANES
    fwd_t = lambda b, s: b * n_steps + s
    bwd_t = lambda b, s: b * n_steps + n_steps - 1 - s

    def specs(tile):
        wide = pl.BlockSpec((tt, w), lambda b, g, s: (tile(b, s), g))
        return wide, [wide, wide, wide,
                      pl.BlockSpec((tt, LANES), lambda b, g, s: (tile(b, s), 0)),
                      pl.BlockSpec((LANES, tt), lambda b, g, s: (0, tile(b, s)))]

    out_f, in_f = specs(fwd_t)
    out_b, in_b = specs(bwd_t)
    st_spec = pl.BlockSpec((None, SCAN_HEADS, 2, LANES, LANES), lambda b, g, s: (b, g, 0, 0, 0))
    return pl.pallas_call(
        functools.partial(_scan_kernel, n_steps),
        grid=(batch, n_groups, n_steps),
        in_specs=in_f + in_b + [st_spec],
        out_specs=[out_f, out_b, st_spec],
        out_shape=[jax.ShapeDtypeStruct((rows, d), F32), jax.ShapeDtypeStruct((rows, d), F32),
                   jax.ShapeDtypeStruct((batch, DN_HEADS, 2, LANES, LANES), F32)],
        scratch_shapes=[pltpu.VMEM((SCAN_HEADS, 2, LANES, LANES), F32)],
        compiler_params=_cparams(("arbitrary", "arbitrary", "arbitrary")),
        name="dn_scan",
    )(qn, kn, vv, gb, gbt, qn, kn, vv, gb, gbt, s0)


def _na_kernel(n_rows, q_ref, k_ref, v_ref, kc_ref, vc_ref, bias_ref, o_ref):
    t = pl.program_id(2)
    kw = NA_KROWS * GRID_W
    ks = jnp.clip(t * NA_QROWS - NA_WIN_R // 2, 0, n_rows - NA_KROWS)
    start = pl.multiple_of(ks * GRID_W, GRID_W)
    q = q_ref[...] * (NA_DH ** -0.5)
    kwin = k_ref[pl.ds(start, kw), :]
    vwin = v_ref[pl.ds(start, kw), :]
    kc = kc_ref[...]
    vc = vc_ref[...]
    lane = lax.broadcasted_iota(jnp.int32, q.shape, 1)
    outs = []
    for hh in range(2):
        sel = (lane < NA_DH) if hh == 0 else (lane >= NA_DH)
        qh = jnp.where(sel, q, jnp.zeros_like(q))
        s_loc = lax.dot_general(qh, kwin, (((1,), (1,)), ((), ())),
                                preferred_element_type=F32) + bias_ref[hh]
        s_ctx = lax.dot_general(qh, kc, (((1,), (1,)), ((), ())), preferred_element_type=F32)
        m = jnp.maximum(jnp.max(s_loc, axis=1, keepdims=True), jnp.max(s_ctx, axis=1, keepdims=True))
        p_loc = jnp.exp(s_loc - m)
        p_ctx = jnp.exp(s_ctx - m)
        denom = jnp.sum(p_loc, axis=1, keepdims=True) + jnp.sum(p_ctx, axis=1, keepdims=True)
        o = (jnp.dot(p_loc.astype(BF16), vwin, preferred_element_type=F32)
             + jnp.dot(p_ctx.astype(BF16), vc, preferred_element_type=F32)) / denom
        outs.append(o)
    o_ref[...] = jnp.where(lane < NA_DH, outs[0], outs[1]).astype(BF16)


def _na_attention(p, pc, bias, batch, seq, ctx_len, q_col, k_col, v_col):
    rows = p.shape[0]
    n_rows = seq // GRID_W
    qt = NA_QROWS * GRID_W
    n_tiles = n_rows // NA_QROWS
    kw = NA_KROWS * GRID_W
    n_pairs = NA_HEADS // 2

    def geom(t):
        return jnp.where(t == 0, 0, jnp.where(t == n_tiles - 1, 2, 1))

    return pl.pallas_call(
        functools.partial(_na_kernel, n_rows),
        grid=(batch, n_pairs, n_tiles),
        in_specs=[pl.BlockSpec((qt, LANES), lambda b, pr, t: (b * n_tiles + t, q_col + pr)),
                  pl.BlockSpec((seq, LANES), lambda b, pr, t: (b, k_col + pr)),
                  pl.BlockSpec((seq, LANES), lambda b, pr, t: (b, v_col + pr)),
                  pl.BlockSpec((ctx_len, LANES), lambda b, pr, t: (b, k_col + pr)),
                  pl.BlockSpec((ctx_len, LANES), lambda b, pr, t: (b, v_col + pr)),
                  pl.BlockSpec((None, 2, qt, kw), lambda b, pr, t: (geom(t), pr, 0, 0))],
        out_specs=pl.BlockSpec((qt, LANES), lambda b, pr, t: (b * n_tiles + t, pr)),
        out_shape=jax.ShapeDtypeStruct((rows, n_pairs * LANES), BF16),
        compiler_params=_cparams(("arbitrary", "arbitrary", "arbitrary")),
        name="na_attention",
    )(p, p, p, pc, pc, bias)


def _ctx_attn_kernel(q_ref, k_ref, v_ref, o_ref):
    q = q_ref[...] * (NA_DH ** -0.5)
    k = k_ref[...]
    v = v_ref[...]
    lane = lax.broadcasted_iota(jnp.int32, q.shape, 1)
    outs = []
    for hh in range(2):
        sel = (lane < NA_DH) if hh == 0 else (lane >= NA_DH)
        qh = jnp.where(sel, q, jnp.zeros_like(q))
        s = lax.dot_general(qh, k, (((1,), (1,)), ((), ())), preferred_element_type=F32)
        pm = jnp.exp(s - jnp.max(s, axis=1, keepdims=True))
        outs.append(jnp.dot(pm.astype(BF16), v, preferred_element_type=F32)
                    / jnp.sum(pm, axis=1, keepdims=True))
    o_ref[...] = jnp.where(lane < NA_DH, outs[0], outs[1]).astype(BF16)


def _ctx_attention(pc, batch, ctx_len, q_col, k_col, v_col):
    n_pairs = NA_HEADS // 2
    return pl.pallas_call(
        _ctx_attn_kernel,
        grid=(batch, n_pairs),
        in_specs=[pl.BlockSpec((ctx_len, LANES), lambda b, pr: (b, q_col + pr)),
                  pl.BlockSpec((ctx_len, LANES), lambda b, pr: (b, k_col + pr)),
                  pl.BlockSpec((ctx_len, LANES), lambda b, pr: (b, v_col + pr))],
        out_specs=pl.BlockSpec((ctx_len, LANES), lambda b, pr: (b, pr)),
        out_shape=jax.ShapeDtypeStruct((pc.shape[0], n_pairs * LANES), BF16),
        compiler_params=_cparams(("arbitrary", "arbitrary")),
        name="ctx_attention",
    )(pc, pc, pc)


def _na_bias_tables(rpb, n_rows):
    n_tiles = n_rows // NA_QROWS
    n_roff = 2 * NA_WIN_R - 1
    n_coff = 2 * NA_WIN_W - 1
    col = np.arange(GRID_W)
    c0 = np.clip(col - NA_WIN_W // 2, 0, GRID_W - NA_WIN_W)
    col_in = (col[None, :] >= c0[:, None]) & (col[None, :] < c0[:, None] + NA_WIN_W)
    coff = np.clip(col[None, :] - col[:, None] + (NA_WIN_W - 1), 0, n_coff - 1)
    col_sel = (coff[:, :, None] == np.arange(n_coff)).astype(np.float32)
    roff_tab = np.zeros((3, NA_QROWS * NA_KROWS), np.int32)
    for g, t in enumerate((0, 1, n_tiles - 1)):
        rs = t * NA_QROWS
        ks = min(max(rs - NA_WIN_R // 2, 0), n_rows - NA_KROWS)
        qrow = rs + np.arange(NA_QROWS)
        krow = ks + np.arange(NA_KROWS)
        r0 = np.clip(qrow - NA_WIN_R // 2, 0, n_rows - NA_WIN_R)
        row_in = (krow[None, :] >= r0[:, None]) & (krow[None, :] < r0[:, None] + NA_WIN_R)
        roff = krow[None, :] - qrow[:, None] + (NA_WIN_R - 1)
        roff_tab[g] = np.where(row_in, roff, -1).reshape(-1)
    by_col = jnp.einsum("hrc,qkc->hrqk", rpb, jnp.asarray(col_sel), precision=lax.Precision.HIGHEST)
    by_col = jnp.where(jnp.asarray(col_in)[None, None], by_col, -jnp.inf)
    n_heads = rpb.shape[0]
    qt, kw = NA_QROWS * GRID_W, NA_KROWS * GRID_W
    return pl.pallas_call(
        _bias_kernel,
        grid_spec=pltpu.PrefetchScalarGridSpec(
            num_scalar_prefetch=1,
            grid=(3, n_heads),
            in_specs=[pl.BlockSpec((None, n_roff, GRID_W, GRID_W), lambda g, h, ro: (h, 0, 0, 0))],
            out_specs=pl.BlockSpec((None, None, qt, kw), lambda g, h, ro: (g, h, 0, 0))),
        out_shape=jax.ShapeDtypeStruct((3, n_heads, qt, kw), F32),
        compiler_params=_cparams(("arbitrary", "arbitrary")),
        name="na_bias_table",
    )(jnp.asarray(roff_tab), by_col)


def _bias_kernel(roff_ref, bycol_ref, o_ref):
    g = pl.program_id(0)
    for a in range(NA_QROWS):
        for i in range(NA_KROWS):
            r = roff_ref[g, a * NA_KROWS + i]
            blk = bycol_ref[jnp.maximum(r, 0)]
            o_ref[a * GRID_W:(a + 1) * GRID_W, i * GRID_W:(i + 1) * GRID_W] = jnp.where(r >= 0, blk, -jnp.inf)


def _merge_kernel(of_ref, ob_ref, z_ref, na_ref, gd_ref, gn_ref, x_ref, dnw_ref, wpa_ref, wpb_ref,
                  wout_ref, gpost_ref, g1_ref, o_ref, dn_scr):
    o = of_ref[...] + ob_ref[...]
    z = z_ref[...].astype(F32)
    for hh in range(DN_HEADS):
        sl = slice(hh * LANES, (hh + 1) * LANES)
        oh = o[:, sl]
        oh = oh * lax.rsqrt(jnp.mean(oh * oh, axis=-1, keepdims=True) + EPS) * dnw_ref[...]
        zh = z[:, sl]
        dn_scr[:, sl] = (oh * (zh * jax.nn.sigmoid(zh))).astype(BF16)
    y = (jax.nn.sigmoid(gd_ref[...].astype(F32)) * jnp.dot(dn_scr[...], wpa_ref[...], preferred_element_type=F32)
         + jax.nn.sigmoid(gn_ref[...].astype(F32)) * _bdot(na_ref[...], wpb_ref[...]))
    out = _bdot(y, wout_ref[...])
    o_ref[...] = x_ref[...] + g1_ref[...] * _rms(out, gpost_ref[...])


def _merge(o_f, o_b, p, na_o, x2, dn_norm, w_pa, w_pb, w_out, gpost, mod, mod_row, g1_blk, tm):
    rows, d = x2.shape
    nw = na_o.shape[1]
    row_blk = lambda c: pl.BlockSpec((tm, d), lambda i: (i, c))
    const = lambda shape: pl.BlockSpec(shape, lambda i: (0,) * len(shape))
    return pl.pallas_call(
        _merge_kernel,
        grid=(rows // tm,),
        in_specs=[row_blk(0), row_blk(0), row_blk(3), pl.BlockSpec((tm, nw), lambda i: (i, 0)),
                  row_blk(4), row_blk(5), row_blk(0),
                  const((1, LANES)), const((d, d)), const((nw, d)), const((d, d)), const((1, d)),
                  pl.BlockSpec((None, 1, d), lambda i: (mod_row(i), 0, g1_blk))],
        out_specs=row_blk(0),
        out_shape=jax.ShapeDtypeStruct((rows, d), F32),
        scratch_shapes=[pltpu.VMEM((tm, d), BF16)],
        compiler_params=_cparams(("arbitrary",)),
        name="merge",
    )(o_f, o_b, p, na_o, p, p, x2, dn_norm, w_pa, w_pb, w_out, gpost, mod)


def _ffn_kernel(n_f, x_ref, gpre_ref, sc_ref, sh_ref, w1_ref, w3_ref, w2_ref, gpost_ref, g2_ref,
                o_ref, h_scr, acc_scr):
    j = pl.program_id(1)

    @pl.when(j == 0)
    def _():
        h = _rms(x_ref[...], gpre_ref[...]) * (1.0 + sc_ref[...]) + sh_ref[...]
        h_scr[...] = h.astype(BF16)
        acc_scr[...] = jnp.zeros_like(acc_scr)

    h = h_scr[...]
    a = jnp.dot(h, w1_ref[...], preferred_element_type=F32)
    b = jnp.dot(h, w3_ref[...], preferred_element_type=F32)
    acc_scr[...] += _bdot(a * jax.nn.sigmoid(a) * b, w2_ref[...])

    @pl.when(j == n_f - 1)
    def _():
        o_ref[...] = x_ref[...] + g2_ref[...] * _rms(acc_scr[...], gpost_ref[...])


def _dense_ffn(x2, gpre, gpost, mod, mod_row, w1, w3, w2, tm, tf):
    rows, d = x2.shape
    f = w1.shape[1]
    n_f = f // tf
    modspec = lambda blk: pl.BlockSpec((None, 1, d), lambda i, j: (mod_row(i), 0, blk))
    return pl.pallas_call(
        functools.partial(_ffn_kernel, n_f),
        grid=(rows // tm, n_f),
        in_specs=[pl.BlockSpec((tm, d), lambda i, j: (i, 0)),
                  pl.BlockSpec((1, d), lambda i, j: (0, 0)),
                  modspec(4), modspec(3),
                  pl.BlockSpec((d, tf), lambda i, j: (0, j)),
                  pl.BlockSpec((d, tf), lambda i, j: (0, j)),
                  pl.BlockSpec((tf, d), lambda i, j: (j, 0)),
                  pl.BlockSpec((1, d), lambda i, j: (0, 0)),
                  modspec(5)],
        out_specs=pl.BlockSpec((tm, d), lambda i, j: (i, 0)),
        out_shape=jax.ShapeDtypeStruct((rows, d), F32),
        scratch_shapes=[pltpu.VMEM((tm, d), BF16), pltpu.VMEM((tm, d), F32)],
        compiler_params=_cparams(("arbitrary", "arbitrary")),
        name="dense_ffn",
    )(x2, gpre, mod, mod, w1, w3, w2, gpost, mod)


def _router_kernel(x_ref, gpre_ref, sc_ref, sh_ref, r_ref, h_ref, gate_ref):
    h = _rms(x_ref[...], gpre_ref[...]) * (1.0 + sc_ref[...]) + sh_ref[...]
    h_ref[...] = h
    logits = jnp.dot(h, r_ref[...], precision=lax.Precision.HIGHEST, preferred_element_type=F32)
    lane = lax.broadcasted_iota(jnp.int32, logits.shape, 1)
    neg = -jnp.inf
    l1 = jnp.where(lane < N_EXPERTS, logits, neg)
    m1 = jnp.max(l1, axis=1, keepdims=True)
    i1 = jnp.min(jnp.where(l1 == m1, lane, LANES), axis=1, keepdims=True)
    l2 = jnp.where(lane == i1, neg, l1)
    m2 = jnp.max(l2, axis=1, keepdims=True)
    i2 = jnp.min(jnp.where(l2 == m2, lane, LANES), axis=1, keepdims=True)
    e = jnp.exp(m2 - m1)
    w1 = 1.0 / (1.0 + e)
    w2 = e / (1.0 + e)
    out = jnp.where(lane == 0, i1.astype(F32), 0.0)
    out = jnp.where(lane == 1, i2.astype(F32), out)
    out = jnp.where(lane == 2, w1, out)
    out = jnp.where(lane == 3, w2, out)
    gate_ref[...] = out


def _router(x2, gpre, mod, mod_row, router_pad, tm):
    rows, d = x2.shape
    modspec = lambda blk: pl.BlockSpec((None, 1, d), lambda i: (mod_row(i), 0, blk))
    return pl.pallas_call(
        _router_kernel,
        grid=(rows // tm,),
        in_specs=[pl.BlockSpec((tm, d), lambda i: (i, 0)),
                  pl.BlockSpec((1, d), lambda i: (0, 0)),
                  modspec(4), modspec(3),
                  pl.BlockSpec((d, LANES), lambda i: (0, 0))],
        out_specs=[pl.BlockSpec((tm, d), lambda i: (i, 0)),
                   pl.BlockSpec((tm, LANES), lambda i: (i, 0))],
        out_shape=[jax.ShapeDtypeStruct((rows, d), F32),
                   jax.ShapeDtypeStruct((rows, LANES), F32)],
        compiler_params=_cparams(("arbitrary",)),
        name="moe_router",
    )(x2, gpre, mod, mod, router_pad)


def _gather_row(h_hbm, xbuf, sem, slot, r, tok):
    return pltpu.make_async_copy(h_hbm.at[pl.ds(tok, 1), :], xbuf.at[slot, pl.ds(r, 1), :], sem.at[slot])


def _scatter_row(stage, out_hbm, sem, slot, r, dst):
    return pltpu.make_async_copy(stage.at[slot, pl.ds(r, 1), :], out_hbm.at[pl.ds(dst, 1), :], sem.at[slot])


def _expert_kernel(n_f, n_tiles, te_ref, nv_ref, tok_ref, tokn_ref, dstp_ref, dst_ref, h_hbm,
                   w1_ref, w3_ref, w2_ref, out_hbm, xbuf, xb16, acc_scr, stage, gsem, ssem):
    i = pl.program_id(0)
    j = pl.program_id(1)
    slot = i % 2
    rows_per_step = MOE_TM // n_f

    def wait_gathers(sl):
        pltpu.make_async_copy(h_hbm.at[pl.ds(0, MOE_TM), :], xbuf.at[sl], gsem.at[sl]).wait()

    def wait_scatters(sl):
        pltpu.make_async_copy(stage.at[sl], out_hbm.at[pl.ds(0, MOE_TM), :], ssem.at[sl]).wait()

    @pl.when((i == 0) & (j == 0))
    def _():
        stage[...] = jnp.zeros_like(stage)

        def body(r, carry):
            _gather_row(h_hbm, xbuf, gsem, slot, r, tok_ref[0, r]).start()
            return carry
        lax.fori_loop(0, MOE_TM, body, 0, unroll=8)

    @pl.when(j == 0)
    def _():
        wait_gathers(slot)
        xb16[...] = xbuf[slot].astype(BF16)
        acc_scr[...] = jnp.zeros_like(acc_scr)

    def row_dmas():
        for k in range(rows_per_step):
            r = pl.multiple_of(j * rows_per_step, SUBLANES) + k
            _gather_row(h_hbm, xbuf, gsem, 1 - slot, r, tokn_ref[0, r]).start()
            _scatter_row(stage, out_hbm, ssem, 1 - slot, r, dstp_ref[0, r]).start(priority=k % 2)

    valid = i < nv_ref[0]

    @pl.when(valid)
    def _():
        row_dmas()
        x = xb16[...]
        a = jnp.dot(x, w1_ref[...], preferred_element_type=F32)
        b = jnp.dot(x, w3_ref[...], preferred_element_type=F32)
        acc_scr[...] += _bdot(a * jax.nn.sigmoid(a) * b, w2_ref[...])

    @pl.when(jnp.logical_not(valid))
    def _():
        row_dmas()

    @pl.when(j == n_f - 1)
    def _():
        @pl.when(i >= 1)
        def _():
            wait_scatters(slot)

        stage[slot] = acc_scr[...]

        @pl.when(i == n_tiles - 1)
        def _():
            wait_scatters(1 - slot)
            wait_gathers(1 - slot)

            def body(r, carry):
                _scatter_row(stage, out_hbm, ssem, slot, r, dst_ref[0, r]).start()
                return carry
            lax.fori_loop(0, MOE_TM, body, 0, unroll=8)
            wait_scatters(slot)


def _expert_ffn(h, row_token, out_row, tile_expert, n_valid, w1, w3, w2, tf):
    d = h.shape[1]
    prow = row_token.shape[0]
    f = w1.shape[2]
    n_f = f // tf
    n_tiles = prow // MOE_TM
    assert f % tf == 0 and MOE_TM % (n_f * SUBLANES) == 0
    idx_spec =lambda fn: pl.BlockSpec((None, 1, MOE_TM), lambda i, j, te, nv: (fn(i), 0, 0),
                                       memory_space=pltpu.SMEM)
    tok3 = row_token.reshape(n_tiles, 1, MOE_TM)
    dst3 = jnp.concatenate([prow + jnp.arange(MOE_TM, dtype=jnp.int32), out_row]).reshape(n_tiles + 1, 1, MOE_TM)
    grid_spec = pltpu.PrefetchScalarGridSpec(
        num_scalar_prefetch=2,
        grid=(n_tiles, n_f),
        in_specs=[idx_spec(lambda i: i),
                  idx_spec(lambda i: jnp.minimum(i + 1, n_tiles - 1)),
                  idx_spec(lambda i: i),
                  idx_spec(lambda i: i + 1),
                  pl.BlockSpec(memory_space=pl.ANY),
                  pl.BlockSpec((None, d, tf), lambda i, j, te, nv: (te[i], 0, j)),
                  pl.BlockSpec((None, d, tf), lambda i, j, te, nv: (te[i], 0, j)),
                  pl.BlockSpec((None, tf, d), lambda i, j, te, nv: (te[i], j, 0))],
        out_specs=pl.BlockSpec(memory_space=pl.ANY),
        scratch_shapes=[pltpu.VMEM((2, MOE_TM, d), F32), pltpu.VMEM((MOE_TM, d), BF16),
                        pltpu.VMEM((MOE_TM, d), F32), pltpu.VMEM((2, MOE_TM, d), F32),
                        pltpu.SemaphoreType.DMA((2,)), pltpu.SemaphoreType.DMA((2,))],
    )
    return pl.pallas_call(
        functools.partial(_expert_kernel, n_f, n_tiles),
        grid_spec=grid_spec,
        out_shape=jax.ShapeDtypeStruct((prow + MOE_TM, d), F32),
        compiler_params=_cparams(("arbitrary", "arbitrary")),
        name="moe_experts",
    )(tile_expert, n_valid, tok3, tok3, dst3, dst3, h, w1, w3, w2)


def _combine_kernel(y1_ref, y2_ref, gate_ref, x_ref, gpost_ref, g2_ref, o_ref):
    gt = gate_ref[...]
    lane = lax.broadcasted_iota(jnp.int32, gt.shape, 1)
    w1 = jnp.sum(jnp.where(lane == 2, gt, 0.0), axis=1, keepdims=True)
    w2 = jnp.sum(jnp.where(lane == 3, gt, 0.0), axis=1, keepdims=True)
    y = w1 * y1_ref[...] + w2 * y2_ref[...]
    o_ref[...] = x_ref[...] + g2_ref[...] * _rms(y, gpost_ref[...])


def _moe_combine(ys, gates, x2, gpost, mod, mod_row, tm):
    rows, d = x2.shape
    rb = pl.BlockSpec((tm, d), lambda i: (i, 0))
    return pl.pallas_call(
        _combine_kernel,
        grid=(rows // tm,),
        in_specs=[rb, pl.BlockSpec((tm, d), lambda i: (rows // tm + i, 0)),
                  pl.BlockSpec((tm, LANES), lambda i: (i, 0)), rb,
                  pl.BlockSpec((1, d), lambda i: (0, 0)),
                  pl.BlockSpec((None, 1, d), lambda i: (mod_row(i), 0, 5))],
        out_specs=rb,
        out_shape=jax.ShapeDtypeStruct((rows, d), F32),
        compiler_params=_cparams(("arbitrary",)),
        name="moe_combine",
    )(ys, ys, gates, x2, gpost, mod)


def _moe_schedule(gates, n_tokens):
    idx = gates[:, 0:2].astype(jnp.int32)
    flat_e = idx.reshape(-1)
    onehot = (flat_e[:, None] == jnp.arange(N_EXPERTS)[None, :]).astype(jnp.int32)
    csum = jnp.cumsum(onehot, axis=0)
    counts = csum[-1]
    rank = jnp.sum(csum * onehot, axis=1) - 1
    padded = ((counts + MOE_TM - 1) // MOE_TM) * MOE_TM
    ends = jnp.cumsum(padded)
    starts = ends - padded
    dest = jnp.sum(starts[None, :] * onehot, axis=1) + rank
    n_pairs = 2 * n_tokens
    n_rows = n_pairs + N_EXPERTS * MOE_TM
    n_tiles = n_rows // MOE_TM
    pair_of_row = jnp.full((n_rows,), -1, jnp.int32).at[dest].set(jnp.arange(n_pairs, dtype=jnp.int32))
    is_pad = pair_of_row < 0
    row_token = jnp.where(is_pad, 0, pair_of_row // 2)
    out_row = jnp.where(is_pad, n_pairs - 1 + jnp.cumsum(is_pad.astype(jnp.int32)),
                        (pair_of_row % 2) * n_tokens + pair_of_row // 2)
    tile_start = jnp.arange(n_tiles, dtype=jnp.int32) * MOE_TM
    tile_expert = jnp.minimum(jnp.sum((ends[None, :] <= tile_start[:, None]).astype(jnp.int32), axis=1),
                              N_EXPERTS - 1)
    n_valid = (ends[-1] // MOE_TM).astype(jnp.int32).reshape(1)
    return row_token, out_row, tile_expert, n_valid


def _rope_tables(seq):
    t = jnp.arange(seq)
    row = (t // GRID_W).astype(F32)
    col = (t % GRID_W).astype(F32)
    n_freq = DN_DK // 4
    inv = ROPE_BASE ** (-jnp.arange(n_freq, dtype=F32) / n_freq)
    ang = jnp.concatenate([row[:, None] * inv, col[:, None] * inv], axis=-1)
    cos, sin = jnp.cos(ang), jnp.sin(ang)
    return jnp.concatenate([cos, cos], axis=-1), jnp.concatenate([-sin, sin], axis=-1)


def kernel(x, c, ctx, c_ctx, ada_w, ada_b, norm_mix_pre, norm_mix_post, norm_ffn_pre, norm_ffn_post,
           w_in, dn_conv, dn_a_log, dn_dt_bias, dn_norm, na_rpb, w_branch_dn, w_branch_na, w_out,
           ffn_w1, ffn_w3, ffn_w2, moe_router, moe_w1, moe_w3, moe_w2):
    batch, seq, d = x.shape
    ctx_len = ctx.shape[1]
    depth = w_in.shape[0]
    nh = DN_HEADS
    dn_w = nh * DN_DK
    na_w = NA_HEADS * NA_DH
    n_rows = seq // GRID_W
    assert d == dn_w and seq % SCAN_TILE == 0 and ctx_len % SCAN_TILE == 0 and n_rows % NA_QROWS == 0
    assert depth <= 2, "context tokens only take the dense FFN path"

    c_rows = jnp.zeros((SUBLANES, d), F32).at[:batch].set(c).at[batch].set(c_ctx)
    mod_all = _mod_vectors(c_rows, ada_w, ada_b)
    cos2, sin2 = _rope_tables(seq)
    ones_t = jnp.ones((SCAN_TILE, LANES), F32)

    x2 = x.reshape(batch * seq, d)
    xc2 = ctx.reshape(batch * ctx_len, d)
    lat_tm = 1024
    lat_row = lambda tm: (lambda i: i // (seq // tm))
    ctx_row = lambda i: batch

    q_col, k_col, v_col = (4 * dn_w + 2 * d) // LANES, (4 * dn_w + 2 * d + na_w) // LANES, \
        (4 * dn_w + 2 * d + 2 * na_w) // LANES

    for l in range(depth):
        last = l == depth - 1
        mod = mod_all[l].reshape(SUBLANES, 1, 6 * d)
        wl = w_in[l]
        o_ab = 4 * dn_w
        o_na = o_ab + 4 * nh
        o_gate = o_na + 3 * na_w
        w_main = jnp.concatenate([wl[:, :o_ab], wl[:, o_gate:], wl[:, o_na:o_gate]], axis=1).astype(BF16)
        wab = jnp.pad(wl[:, o_ab:o_na], ((0, 0), (0, LANES - 4 * nh)))
        wab_hi = wab.astype(BF16)
        wab_lo = (wab - wab_hi.astype(F32)).astype(BF16)
        gpre = norm_mix_pre[l].reshape(1, d)
        gpost = norm_mix_post[l].reshape(1, d)

        p, ab = _in_proj(x2, gpre, mod, lat_row(lat_tm), 1, 0, w_main, wab_hi, wab_lo, lat_tm, INPROJ_TN)
        pc, abc = _in_proj(xc2, gpre, mod, ctx_row, 1, 0, w_main, wab_hi, wab_lo, batch * ctx_len, INPROJ_TN)

        conv_w3 = jnp.pad(dn_conv[l].T.reshape(DN_CONV, 3, dn_w).transpose(1, 0, 2),
                          ((0, 0), (0, SUBLANES - DN_CONV), (0, 0)))
        gpar = jnp.zeros((SUBLANES, LANES), F32)
        gpar = gpar.at[0, :2 * nh].set(-jnp.exp(dn_a_log[l].reshape(-1)))
        gpar = gpar.at[1, :2 * nh].set(dn_dt_bias[l].reshape(-1))

        qc_, kc_, vc_, gbc, gbtc = _dn_prep(pc, abc, conv_w3, gpar, ones_t, ones_t, batch, ctx_len, False)
        ql_, kl_, vl_, gbl, gbtl = _dn_prep(p, ab, conv_w3, gpar, cos2, sin2, batch, seq, True)
        s0 = jnp.zeros((batch, nh, 2, LANES, LANES), F32)
        oc_f, oc_b, s_ctx = _dn_scan(qc_, kc_, vc_, gbc, gbtc, s0, batch, ctx_len)
        ol_f, ol_b, _ = _dn_scan(ql_, kl_, vl_, gbl, gbtl, s_ctx, batch, seq)

        bias = _na_bias_tables(na_rpb[l], n_rows)
        na_lat = _na_attention(p, pc, bias, batch, seq, ctx_len, q_col, k_col, v_col)

        dnw = dn_norm[l].reshape(1, LANES)
        w_pa = w_branch_dn[l].astype(BF16)
        w_pb = w_branch_na[l].astype(BF16)
        w_o = w_out[l].astype(BF16)
        x2 = _merge(ol_f, ol_b, p, na_lat, x2, dnw, w_pa, w_pb, w_o, gpost, mod, lat_row(512), 2, 512)

        gfpre = norm_ffn_pre[l].reshape(1, d)
        gfpost = norm_ffn_post[l].reshape(1, d)
        if l % 2 == 0:
            w1 = ffn_w1[l // 2].astype(BF16)
            w3 = ffn_w3[l // 2].astype(BF16)
            w2 = ffn_w2[l // 2].astype(BF16)
            tf = w1.shape[1] // 2
            x2 = _dense_ffn(x2, gfpre, gfpost, mod, lat_row(512), w1, w3, w2, 512, tf)
        else:
            rpad = jnp.pad(moe_router[l // 2], ((0, 0), (0, LANES - N_EXPERTS)))
            hb, gates = _router(x2, gfpre, mod, lat_row(512), rpad, 512)
            n_tok = batch * seq
            row_token, out_row, tile_expert, n_valid = _moe_schedule(gates, n_tok)
            ys = _expert_ffn(hb, row_token, out_row, tile_expert, n_valid, moe_w1[l // 2].astype(BF16),
                             moe_w3[l // 2].astype(BF16), moe_w2[l // 2].astype(BF16), MOE_TF)
            x2 = _moe_combine(ys, gates, x2, gfpost, mod, lat_row(512), 512)

        if not last:
            na_ctx = _ctx_attention(pc, batch, ctx_len, q_col, k_col, v_col)
            xc2 = _merge(oc_f, oc_b, pc, na_ctx, xc2, dnw, w_pa, w_pb, w_o, gpost, mod, ctx_row, 2, 256)
            xc2 = _dense_ffn(xc2, gfpre, gfpost, mod, ctx_row, w1, w3, w2, 512, tf)
    return x2.reshape(batch, seq, d)
```
